```python
import math
import jax, jax.numpy as jnp
from jax import lax
import numpy as np


D_MODEL = 1024
BATCH = 4
SEQ = 4096
DEPTH = 4

N_MIXERS = 3
HEAD_DIM = 64
DIL_PATTERNS = ((128, 1), (512, 4), (2048, 16))
N_DIL_GROUPS = len(DIL_PATTERNS)
DIL_HEADS = D_MODEL // HEAD_DIM
DIFF_HEADS = D_MODEL // (2 * HEAD_DIM)
DIFF_V_DIM = 2 * HEAD_DIM
N_BIAS_HEADS = DIL_HEADS
N_BUCKETS = 32
MAX_DISTANCE = 2048
Q_BLOCK = 128
SSM_D_INNER = 2 * D_MODEL
SSM_HEAD_DIM = 64
SSM_HEADS = SSM_D_INNER // SSM_HEAD_DIM
SSM_GROUPS = 8
SSM_D_STATE = 128
SSM_CONV = 4
SSM_CHUNK = 128
SSM_CONV_DIM = SSM_D_INNER + 2 * SSM_GROUPS * SSM_D_STATE
SSM_IN_DIM = SSM_D_INNER + SSM_CONV_DIM + SSM_HEADS
D_FF = 4 * D_MODEL
NORM_EPS = 1e-5
DT_MIN = 1e-3
DT_MAX = 1e-1

kernel_name = 'hybrid_dilated_diff_ssd_trunk'


def rms_norm(x, gain):
    xf = x.astype(jnp.float32)
    y = xf * lax.rsqrt(jnp.mean(xf * xf, axis=-1, keepdims=True) + NORM_EPS)
    return (y * gain.astype(jnp.float32)).astype(x.dtype)


def rel_bucket(dist):
    exact = N_BUCKETS // 2
    d = jnp.maximum(dist, 1).astype(jnp.float32)
    large = exact + (jnp.log(d / exact) / math.log(MAX_DISTANCE / exact) * (N_BUCKETS - exact)).astype(jnp.int32)
    large = jnp.minimum(large, N_BUCKETS - 1)
    return jnp.where(dist < exact, dist, large)


def dilated_group(q, k, v, rel_bias, window, dilation):
    b, s, h, d = q.shape
    n = window // dilation
    length = s // dilation
    nb = -(-length // n)
    lp = nb * n

    def to_blocks(t):
        t = jnp.moveaxis(t.reshape(b, length, dilation, h, d), 2, 1)
        t = jnp.pad(t, ((0, 0), (0, 0), (0, lp - length), (0, 0), (0, 0)))
        return t.reshape(b, dilation, nb, n, h, d)

    def with_prev(t):
        prev = jnp.pad(t, ((0, 0), (0, 0), (1, 0), (0, 0), (0, 0), (0, 0)))[:, :, :-1]
        return jnp.concatenate([prev, t], axis=3)

    qb = to_blocks(q).astype(jnp.float32)
    kw = with_prev(to_blocks(k)).astype(jnp.float32)
    vw = with_prev(to_blocks(v)).astype(jnp.float32)

    qi = jnp.arange(n)[:, None]
    kj = jnp.arange(2 * n)[None, :]
    steps = n + qi - kj
    band = (steps >= 0) & (steps <= n)
    blk = jnp.arange(nb)[:, None, None]
    valid = band[None] & ((blk - 1) * n + kj[None] >= 0)
    bias = rel_bias[rel_bucket(jnp.clip(steps, 0, n) * dilation)].astype(jnp.float32)
    bias = jnp.moveaxis(bias, -1, 0)

    scores = jnp.einsum('brnqhd,brnkhd->brnhqk', qb, kw) * (HEAD_DIM ** -0.5) + bias[None, None, None]
    scores = jnp.where(valid[None, None, :, None], scores, -jnp.inf)
    m = jnp.max(scores, axis=-1, keepdims=True)
    p = jnp.exp(scores - m)
    den = jnp.sum(p, axis=-1, keepdims=True)
    out = jnp.einsum('brnhqk,brnkhd->brnqhd', p / den, vw)
    lse = (m + jnp.log(den))[..., 0]

    out = out.reshape(b, dilation, lp, h, d)[:, :, :length]
    out = jnp.moveaxis(out, 1, 2).reshape(b, s, h, d)
    lse = jnp.moveaxis(lse, 3, 4).reshape(b, dilation, lp, h)[:, :, :length]
    lse = jnp.moveaxis(lse, 1, 2).reshape(b, s, h)
    return out, lse


def dilated_mixer(h_in, w_qkv, w_o, rel_bias):
    b, s, _ = h_in.shape
    qkv = (h_in @ w_qkv).reshape(b, s, N_DIL_GROUPS, 3, DIL_HEADS, HEAD_DIM)
    outs, lses = [], []
    for g, (window, dilation) in enumerate(DIL_PATTERNS):
        o, l = dilated_group(qkv[:, :, g, 0], qkv[:, :, g, 1], qkv[:, :, g, 2], rel_bias, window, dilation)
        outs.append(o)
        lses.append(l)
    weights = jax.nn.softmax(jnp.stack(lses), axis=0)
    o = jnp.einsum('gbsh,gbshd->bshd', weights, jnp.stack(outs))
    return o.reshape(b, s, DIL_HEADS * HEAD_DIM).astype(h_in.dtype) @ w_o


def diff_lambda_init(layer):
    return 0.8 - 0.6 * math.exp(-0.3 * layer)


def diff_mixer(h_in, w_qkv, lam_q1, lam_k1, lam_q2, lam_k2, subln, w_o, rel_bias, lambda_init):
    b, s, _ = h_in.shape
    qkv = h_in @ w_qkv
    width = DIFF_HEADS * 2 * HEAD_DIM
    q = qkv[..., :width].reshape(b, s, DIFF_HEADS, 2, HEAD_DIM).astype(jnp.float32)
    k = qkv[..., width:2 * width].reshape(b, s, DIFF_HEADS, 2, HEAD_DIM).astype(jnp.float32)
    v = qkv[..., 2 * width:].reshape(b, s, DIFF_HEADS, DIFF_V_DIM).astype(jnp.float32)
    lam = (jnp.exp(jnp.sum(lam_q1.astype(jnp.float32) * lam_k1.astype(jnp.float32)))
           - jnp.exp(jnp.sum(lam_q2.astype(jnp.float32) * lam_k2.astype(jnp.float32))) + lambda_init)
    bias_cols = rel_bias.reshape(N_BUCKETS, 2, DIFF_HEADS).astype(jnp.float32)
    nblk = s // Q_BLOCK
    qb = jnp.moveaxis(q.reshape(b, nblk, Q_BLOCK, DIFF_HEADS, 2, HEAD_DIM), 1, 0)
    starts = jnp.arange(nblk) * Q_BLOCK
    key_pos = jnp.arange(s)

    def block(args):
        q_blk, start = args
        dist = (start + jnp.arange(Q_BLOCK))[:, None] - key_pos[None, :]
        bias = bias_cols[rel_bucket(jnp.maximum(dist, 0))]
        scores = jnp.einsum('bqhmd,bkhmd->bhmqk', q_blk, k) * (HEAD_DIM ** -0.5) + bias.transpose(3, 2, 0, 1)
        scores = jnp.where(dist >= 0, scores, -jnp.inf)
        p = jax.nn.softmax(scores, axis=-1)
        a = p[:, :, 0] - lam * p[:, :, 1]
        return jnp.einsum('bhqk,bkhe->bqhe', a, v)

    o = lax.map(block, (qb, starts))
    o = jnp.moveaxis(o, 0, 1).reshape(b, s, DIFF_HEADS, DIFF_V_DIM)
    o = rms_norm(o, subln) * (1.0 - lambda_init)
    return o.reshape(b, s, DIFF_HEADS * DIFF_V_DIM).astype(h_in.dtype) @ w_o


def ssd_chunked(x, a, bmat, cmat):
    b, s, h, p = x.shape
    g, n = bmat.shape[2], bmat.shape[3]
    hg = h // g
    c, l = s // SSM_CHUNK, SSM_CHUNK
    x = x.reshape(b, c, l, g, hg, p)
    a = jnp.moveaxis(a.reshape(b, c, l, g, hg), 2, -1)
    bmat = bmat.reshape(b, c, l, g, n)
    cmat = cmat.reshape(b, c, l, g, n)
    a_cs = jnp.cumsum(a, axis=-1)
    causal = jnp.tril(jnp.ones((l, l), dtype=bool))
    seg = a_cs[..., :, None] - a_cs[..., None, :]
    decay = jnp.where(causal, jnp.exp(jnp.where(causal, seg, 0.0)), 0.0)
    cb = jnp.einsum('bclgn,bcsgn->bcgls', cmat, bmat)
    y_diag = jnp.einsum('bcgjls,bcsgjp->bclgjp', cb[:, :, :, None] * decay, x)
    state_decay = jnp.exp(a_cs[..., -1:] - a_cs)
    chunk_states = jnp.einsum('bclgn,bcgjl,bclgjp->bcgjpn', bmat, state_decay, x)
    chunk_decay = jnp.exp(a_cs[..., -1])

    def step(state, inp):
        st, dec = inp
        return state * dec[..., None, None] + st, state

    init = jnp.zeros((b, g, hg, p, n), jnp.float32)
    _, prev = lax.scan(step, init, (jnp.moveaxis(chunk_states, 1, 0), jnp.moveaxis(chunk_decay, 1, 0)))
    prev = jnp.moveaxis(prev, 0, 1)
    y_off = jnp.einsum('bclgn,bcgjpn,bcgjl->bclgjp', cmat, prev, jnp.exp(a_cs))
    return (y_diag + y_off).reshape(b, s, h, p)


def ssd_mixer(h_in, w_in, conv_w, conv_b, dt_bias, a_log, d_skip, gate_norm, w_out):
    b, s, _ = h_in.shape
    proj = h_in @ w_in
    z = proj[..., :SSM_D_INNER]
    xbc = proj[..., SSM_D_INNER:SSM_D_INNER + SSM_CONV_DIM]
    dt = proj[..., SSM_D_INNER + SSM_CONV_DIM:]
    xbc = lax.conv_general_dilated(xbc, conv_w[:, None, :], window_strides=(1,), padding=((SSM_CONV - 1, 0),),
                                   dimension_numbers=('NWC', 'WIO', 'NWC'), feature_group_count=SSM_CONV_DIM)
    xbc = jax.nn.silu((xbc + conv_b).astype(jnp.float32))
    xs = xbc[..., :SSM_D_INNER].reshape(b, s, SSM_HEADS, SSM_HEAD_DIM)
    bmat = xbc[..., SSM_D_INNER:SSM_D_INNER + SSM_GROUPS * SSM_D_STATE].reshape(b, s, SSM_GROUPS, SSM_D_STATE)
    cmat = xbc[..., SSM_D_INNER + SSM_GROUPS * SSM_D_STATE:].reshape(b, s, SSM_GROUPS, SSM_D_STATE)
    dt = jax.nn.softplus(dt.astype(jnp.float32) + dt_bias.astype(jnp.float32))
    a = -jnp.exp(a_log.astype(jnp.float32))
    y = ssd_chunked(xs * dt[..., None], dt * a, bmat, cmat) + xs * d_skip.astype(jnp.float32)[:, None]
    y = y.reshape(b, s, SSM_D_INNER) * jax.nn.silu(z.astype(jnp.float32))
    y = rms_norm(y.reshape(b, s, SSM_GROUPS, -1), gate_norm.reshape(SSM_GROUPS, -1)).reshape(b, s, SSM_D_INNER)
    return y.astype(h_in.dtype) @ w_out


def sq_relu_mlp(h, w_up, w_down):
    u = jnp.maximum(h @ w_up, 0)
    return (u * u) @ w_down


def _dense(k, fan_in, fan_out):
    return jax.random.normal(k, (fan_in, fan_out), jnp.float32) * fan_in ** -0.5


def _gain(k, n):
    return 1.0 + 0.02 * jax.random.normal(k, (n,), jnp.float32)


def setup_inputs(seed: int = 0) -> dict:
    key = jax.random.key(seed)
    ks = iter(jax.random.split(key, 64))
    p = {}
    p['x'] = jax.random.normal(next(ks), (BATCH, SEQ, D_MODEL), jnp.float32)
    p['rel_bias'] = 0.2 * jax.random.normal(next(ks), (N_BUCKETS, N_BIAS_HEADS), jnp.float32)
    for i in range(DEPTH):
        pre = f'l{i}_'
        kind = i % N_MIXERS
        p[pre + 'mix_norm'] = _gain(next(ks), D_MODEL)
        if kind == 0:
            p[pre + 'dil_w_qkv'] = _dense(next(ks), D_MODEL, N_DIL_GROUPS * 3 * DIL_HEADS * HEAD_DIM)
            p[pre + 'dil_w_o'] = _dense(next(ks), DIL_HEADS * HEAD_DIM, D_MODEL)
        elif kind == 1:
            p[pre + 'diff_w_qkv'] = _dense(next(ks), D_MODEL, 3 * DIFF_HEADS * 2 * HEAD_DIM)
            for nm in ('diff_lam_q1', 'diff_lam_k1', 'diff_lam_q2', 'diff_lam_k2'):
                p[pre + nm] = 0.1 * jax.random.normal(next(ks), (HEAD_DIM,), jnp.float32)
            p[pre + 'diff_subln'] = _gain(next(ks), DIFF_V_DIM)
            p[pre + 'diff_w_o'] = _dense(next(ks), DIFF_HEADS * DIFF_V_DIM, D_MODEL)
        else:
            p[pre + 'ssm_w_in'] = _dense(next(ks), D_MODEL, SSM_IN_DIM)
            p[pre + 'ssm_conv_w'] = 0.5 * jax.random.normal(next(ks), (SSM_CONV, SSM_CONV_DIM), jnp.float32)
            p[pre + 'ssm_conv_b'] = 0.02 * jax.random.normal(next(ks), (SSM_CONV_DIM,), jnp.float32)
            u = jax.random.uniform(next(ks), (SSM_HEADS,), jnp.float32)
            dt0 = jnp.exp(u * (math.log(DT_MAX) - math.log(DT_MIN)) + math.log(DT_MIN))
            p[pre + 'ssm_dt_bias'] = dt0 + jnp.log(-jnp.expm1(-dt0))
            p[pre + 'ssm_A_log'] = jnp.log(jax.random.uniform(next(ks), (SSM_HEADS,), jnp.float32, 1.0, 16.0))
            p[pre + 'ssm_D'] = 1.0 + 0.1 * jax.random.normal(next(ks), (SSM_HEADS,), jnp.float32)
            p[pre + 'ssm_gate_norm'] = _gain(next(ks), SSM_D_INNER)
            p[pre + 'ssm_w_out'] = _dense(next(ks), SSM_D_INNER, D_MODEL)
        p[pre + 'mlp_norm'] = _gain(next(ks), D_MODEL)
        p[pre + 'mlp_w_up'] = _dense(next(ks), D_MODEL, D_FF)
        p[pre + 'mlp_w_down'] = _dense(next(ks), D_FF, D_MODEL)
    p['final_norm'] = _gain(next(ks), D_MODEL)
    return p


def reference(x, rel_bias,
              l0_mix_norm, l0_dil_w_qkv, l0_dil_w_o, l0_mlp_norm, l0_mlp_w_up, l0_mlp_w_down,
              l1_mix_norm, l1_diff_w_qkv, l1_diff_lam_q1, l1_diff_lam_k1, l1_diff_lam_q2, l1_diff_lam_k2,
              l1_diff_subln, l1_diff_w_o, l1_mlp_norm, l1_mlp_w_up, l1_mlp_w_down,
              l2_mix_norm, l2_ssm_w_in, l2_ssm_conv_w, l2_ssm_conv_b, l2_ssm_dt_bias, l2_ssm_A_log, l2_ssm_D,
              l2_ssm_gate_norm, l2_ssm_w_out, l2_mlp_norm, l2_mlp_w_up, l2_mlp_w_down,
              l3_mix_norm, l3_dil_w_qkv, l3_dil_w_o, l3_mlp_norm, l3_mlp_w_up, l3_mlp_w_down,
              final_norm):
    mix_norms = [l0_mix_norm, l1_mix_norm, l2_mix_norm, l3_mix_norm]
    mixer_args = [
        (l0_dil_w_qkv, l0_dil_w_o),
        (l1_diff_w_qkv, l1_diff_lam_q1, l1_diff_lam_k1, l1_diff_lam_q2, l1_diff_lam_k2, l1_diff_subln, l1_diff_w_o),
        (l2_ssm_w_in, l2_ssm_conv_w, l2_ssm_conv_b, l2_ssm_dt_bias, l2_ssm_A_log, l2_ssm_D, l2_ssm_gate_norm, l2_ssm_w_out),
        (l3_dil_w_qkv, l3_dil_w_o),
    ]
    mlp_args = [
        (l0_mlp_norm, l0_mlp_w_up, l0_mlp_w_down),
        (l1_mlp_norm, l1_mlp_w_up, l1_mlp_w_down),
        (l2_mlp_norm, l2_mlp_w_up, l2_mlp_w_down),
        (l3_mlp_norm, l3_mlp_w_up, l3_mlp_w_down),
    ]
    h = x
    for i in range(DEPTH):
        kind = i % N_MIXERS
        hn = rms_norm(h, mix_norms[i])
        if kind == 0:
            mixed = dilated_mixer(hn, *mixer_args[i], rel_bias)
        elif kind == 1:
            mixed = diff_mixer(hn, *mixer_args[i], rel_bias, diff_lambda_init(i))
        else:
            mixed = ssd_mixer(hn, *mixer_args[i])
        h = h + mixed
        norm_g, w_up, w_down = mlp_args[i]
        h = h + sq_relu_mlp(rms_norm(h, norm_g), w_up, w_down)
    return rms_norm(h, final_norm)
```

```python
import functools
import math

import jax
import jax.numpy as jnp
from jax import lax
from jax.experimental import pallas as pl
from jax.experimental.pallas import tpu as pltpu

F32 = jnp.float32
BF16 = jnp.bfloat16

NORM_EPS = 1e-5
HEAD_DIM = 64
DIL_PATTERNS = ((128, 1), (512, 4), (2048, 16))
DIL_HEADS = 16
DIL_STEPS = 128
DIFF_HEADS = 8
N_BUCKETS = 32
MAX_DISTANCE = 2048
SSM_HEADS = 32
SSM_GROUPS = 8
SSM_D_STATE = 128
SSM_HEAD_DIM = 64
SSM_D_INNER = SSM_HEADS * SSM_HEAD_DIM
SSM_CONV = 4
SSM_CHUNK = 128
LANES = 128
SUBLANES = 8
MASK_VALUE = -1e30
VMEM_LIMIT = 48 * 1024 * 1024

DIFF_TILE = 512


def _bucket_thresholds():
    exact = N_BUCKETS // 2

    def bucket(d):
        if d < exact:
            return d
        big = exact + int(math.log(d / exact) / math.log(MAX_DISTANCE / exact) * (N_BUCKETS - exact))
        return min(big, N_BUCKETS - 1)

    thr = {}
    d = 0
    for b in range(1, N_BUCKETS):
        while bucket(d) < b:
            d += 1
        thr[b] = d
    return thr, bucket


_BUCKET_THR, _bucket_of = _bucket_thresholds()


def _cparams(sem):
    return pltpu.CompilerParams(dimension_semantics=sem, vmem_limit_bytes=VMEM_LIMIT)


def _rms(x, gain):
    ms = jnp.mean(x * x, axis=-1, keepdims=True)
    return x * lax.rsqrt(ms + NORM_EPS) * gain


def _norm_matmul_body(x_ref, g_ref, w_ref, o_ref):
    xn = _rms(x_ref[...], g_ref[...]).astype(BF16)
    o_ref[...] = jnp.dot(xn, w_ref[...], preferred_element_type=F32).astype(o_ref.dtype)


def norm_matmul(x, gain, w, *, dilation=1, tm=256, tn=None, out_dtype=BF16):
    b, s, d = x.shape
    n = w.shape[1]
    r = dilation
    length = s // r
    tm = min(tm, length)
    tn = n if tn is None else tn
    xv = x.reshape(b, length, r * d)
    grid = (n // tn, b, r, length // tm)
    return pl.pallas_call(
        _norm_matmul_body,
        grid=grid,
        in_specs=[
            pl.BlockSpec((None, tm, d), lambda j, bi, rho, i: (bi, i, rho)),
            pl.BlockSpec((1, d), lambda j, bi, rho, i: (0, 0)),
            pl.BlockSpec((d, tn), lambda j, bi, rho, i: (0, j)),
        ],
        out_specs=pl.BlockSpec((None, None, tm, tn), lambda j, bi, rho, i: (bi, rho, i, j)),
        out_shape=jax.ShapeDtypeStruct((b, r, length, n), out_dtype),
        compiler_params=_cparams(("arbitrary",) * 4),
        name="norm_matmul",
    )(xv, gain.reshape(1, d), w)


def _matmul_res_body(a_ref, w_ref, r_ref, o_ref):
    o_ref[...] = r_ref[...] + jnp.dot(a_ref[...], w_ref[...], preferred_element_type=F32)


def matmul_residual(a, w, res, *, tm=512):
    m, k = a.shape
    n = w.shape[1]
    return pl.pallas_call(
        _matmul_res_body,
        grid=(m // tm,),
        in_specs=[
            pl.BlockSpec((tm, k), lambda i: (i, 0)),
            pl.BlockSpec((k, n), lambda i: (0, 0)),
            pl.BlockSpec((tm, n), lambda i: (i, 0)),
        ],
        out_specs=pl.BlockSpec((tm, n), lambda i: (i, 0)),
        out_shape=jax.ShapeDtypeStruct((m, n), F32),
        compiler_params=_cparams(("arbitrary",)),
        name="matmul_residual",
    )(a, w, res)


def _mlp_body(h_ref, g_ref, wu_ref, wd_ref, fg_ref, o_ref, xn_ref, acc_ref, *, final_norm):
    f = pl.program_id(1)

    @pl.when(f == 0)
    def _():
        xn_ref[...] = _rms(h_ref[...], g_ref[...]).astype(BF16)
        acc_ref[...] = jnp.zeros_like(acc_ref)

    u = jnp.maximum(jnp.dot(xn_ref[...], wu_ref[...], preferred_element_type=F32), 0.0)
    acc_ref[...] += jnp.dot((u * u).astype(BF16), wd_ref[...], preferred_element_type=F32)

    @pl.when(f == pl.num_programs(1) - 1)
    def _():
        out = h_ref[...] + acc_ref[...]
        if final_norm:
            out = _rms(out, fg_ref[...])
        o_ref[...] = out


def mlp_block(h, gain, w_up, w_down, final_gain=None, *, tm=512, tf=1024):
    m, d = h.shape
    f = w_up.shape[1]
    final_norm = final_gain is not None
    fg = (final_gain if final_norm else gain).reshape(1, d)
    return pl.pallas_call(
        functools.partial(_mlp_body, final_norm=final_norm),
        grid=(m // tm, f // tf),
        in_specs=[
            pl.BlockSpec((tm, d), lambda i, j: (i, 0)),
            pl.BlockSpec((1, d), lambda i, j: (0, 0)),
            pl.BlockSpec((d, tf), lambda i, j: (0, j)),
            pl.BlockSpec((tf, d), lambda i, j: (j, 0)),
            pl.BlockSpec((1, d), lambda i, j: (0, 0)),
        ],
        out_specs=pl.BlockSpec((tm, d), lambda i, j: (i, 0)),
        out_shape=jax.ShapeDtypeStruct((m, d), F32),
        scratch_shapes=[pltpu.VMEM((tm, d), BF16), pltpu.VMEM((tm, d), F32)],
        compiler_params=_cparams(("arbitrary", "arbitrary")),
        name="mlp_block",
    )(h, gain.reshape(1, d), w_up, w_down, fg)


def _bias_from_dist(dist, col_of_bucket, d_lo, d_hi):
    val = jnp.full(dist.shape, col_of_bucket(_bucket_of(d_lo)), F32)
    for b in range(1, N_BUCKETS):
        t = _BUCKET_THR[b]
        if d_lo < t <= d_hi:
            val = jnp.where(dist >= t, col_of_bucket(b), val)
    return val


def _dil_bias_body(tab_ref, o_ref, *, dilation):
    c = pl.program_id(0)
    n = DIL_STEPS
    qi = lax.broadcasted_iota(jnp.int32, (n, 2 * n), 0)
    kj = lax.broadcasted_iota(jnp.int32, (n, 2 * n), 1)
    steps = n + qi - kj
    band = (steps >= 0) & (steps <= n)
    dist = jnp.clip(steps, 0, n) * dilation
    val = _bias_from_dist(dist, lambda b: tab_ref[b, c], 0, n * dilation)
    later = jnp.where(band, val, MASK_VALUE)
    o_ref[1] = later
    o_ref[0] = jnp.where(kj >= n, later, MASK_VALUE)


def dilated_bias(rel_bias, dilation):
    n = DIL_STEPS
    return pl.pallas_call(
        functools.partial(_dil_bias_body, dilation=dilation),
        grid=(DIL_HEADS,),
        in_specs=[pl.BlockSpec(memory_space=pltpu.SMEM)],
        out_specs=pl.BlockSpec((2, None, n, 2 * n), lambda c: (0, c, 0, 0)),
        out_shape=jax.ShapeDtypeStruct((2, DIL_HEADS, n, 2 * n), F32),
        compiler_params=_cparams(("arbitrary",)),
        name="dilated_bias",
    )(rel_bias)


def _diff_bias_body(tab_ref, o_ref, *, n_delta):
    m = pl.program_id(0)
    h = pl.program_id(1)
    c = m * DIFF_HEADS + h
    t = DIFF_TILE
    qi = lax.broadcasted_iota(jnp.int32, (t, t), 0)
    kj = lax.broadcasted_iota(jnp.int32, (t, t), 1)
    for delta in range(n_delta):
        dist = delta * t + qi - kj
        d_lo = max(delta * t - (t - 1), 0)
        d_hi = delta * t + t - 1
        val = _bias_from_dist(dist, lambda b: tab_ref[b, c], d_lo, d_hi)
        if delta == 0:
            val = jnp.where(dist >= 0, val, MASK_VALUE)
        o_ref[delta] = val


def _diff_n_delta(s):
    t = DIFF_TILE
    last_thr = _BUCKET_THR[N_BUCKETS - 1]
    n_delta = 1
    while n_delta * t - (t - 1) < last_thr:
        n_delta += 1
    return min(n_delta + 1, s // t)


def diff_bias(rel_bias, s):
    t = DIFF_TILE
    n_delta = _diff_n_delta(s)
    return pl.pallas_call(
        functools.partial(_diff_bias_body, n_delta=n_delta),
        grid=(2, DIFF_HEADS),
        in_specs=[pl.BlockSpec(memory_space=pltpu.SMEM)],
        out_specs=pl.BlockSpec((None, None, n_delta, t, t), lambda m, h: (h, m, 0, 0, 0)),
        out_shape=jax.ShapeDtypeStruct((DIFF_HEADS, 2, n_delta, t, t), F32),
        compiler_params=_cparams(("arbitrary", "arbitrary")),
        name="diff_bias",
    )(rel_bias)


def _dil_attn_body(q_ref, kp_ref, kc_ref, vp_ref, vc_ref, bias_ref, o_ref):
    n = DIL_STEPS
    d = HEAD_DIM
    hd = DIL_HEADS * d
    nt = (((1,), (1,)), ((), ()))
    o_ref[:, hd:] = jnp.zeros((n, LANES), F32)
    for h in range(DIL_HEADS):
        sl = slice(h * d, (h + 1) * d)
        q = q_ref[:, sl] * jnp.asarray(HEAD_DIM ** -0.5, BF16)
        bias = bias_ref[h]
        s_p = lax.dot_general(q, kp_ref[:, sl], nt, preferred_element_type=F32) + bias[:, :n]
        s_c = lax.dot_general(q, kc_ref[:, sl], nt, preferred_element_type=F32) + bias[:, n:]
        m = jnp.maximum(jnp.max(s_p, axis=-1, keepdims=True), jnp.max(s_c, axis=-1, keepdims=True))
        p_p = jnp.exp(s_p - m)
        p_c = jnp.exp(s_c - m)
        den = jnp.sum(p_p, axis=-1, keepdims=True) + jnp.sum(p_c, axis=-1, keepdims=True)
        pv = (jnp.dot(p_p.astype(BF16), vp_ref[:, sl], preferred_element_type=F32)
              + jnp.dot(p_c.astype(BF16), vc_ref[:, sl], preferred_element_type=F32))
        o_ref[:, sl] = pv / den
        o_ref[:, hd + h:hd + h + 1] = m + jnp.log(den)


def dilated_attention(qkv, bias, dilation):
    b, r, length, _ = qkv.shape
    n = DIL_STEPS
    hd = DIL_HEADS * HEAD_DIM
    wout = hd + LANES
    nb = length // n

    def cur(col):
        return pl.BlockSpec((None, None, n, hd), lambda bi, rho, i: (bi, rho, i, col))

    def prev(col):
        return pl.BlockSpec((None, None, n, hd), lambda bi, rho, i: (bi, rho, jnp.maximum(i - 1, 0), col))

    return pl.pallas_call(
        _dil_attn_body,
        grid=(b, r, nb),
        in_specs=[cur(0), prev(1), cur(1), prev(2), cur(2),
                  pl.BlockSpec((None, DIL_HEADS, n, 2 * n), lambda bi, rho, i: (jnp.minimum(i, 1), 0, 0, 0))],
        out_specs=pl.BlockSpec((None, n, wout), lambda bi, rho, i: (bi, i, rho)),
        out_shape=jax.ShapeDtypeStruct((b, length, r * wout), F32),
        compiler_params=_cparams(("arbitrary",) * 3),
        name=f"dilated_attention_r{dilation}",
    )(qkv, qkv, qkv, qkv, qkv, bias)


def _dil_combine_body(o1_ref, o2_ref, o3_ref, w_ref, h_ref, out_ref, mix_ref):
    hd = DIL_HEADS * HEAD_DIM
    groups = (o1_ref, o2_ref, o3_ref)
    lses = [g[:, hd:] for g in groups]
    mx = jnp.maximum(jnp.maximum(lses[0], lses[1]), lses[2])
    es = [jnp.exp(l - mx) for l in lses]
    inv = 1.0 / (es[0] + es[1] + es[2])
    ws = [e * inv for e in es]
    lane = lax.broadcasted_iota(jnp.int32, (1, LANES), 1)
    for sl in range(hd // LANES):
        acc = None
        for g, w in zip(groups, ws):
            wl = jnp.where(lane < HEAD_DIM, w[:, 2 * sl:2 * sl + 1], w[:, 2 * sl + 1:2 * sl + 2])
            term = wl * g[:, sl * LANES:(sl + 1) * LANES]
            acc = term if acc is None else acc + term
        mix_ref[:, sl * LANES:(sl + 1) * LANES] = acc.astype(BF16)
    out_ref[...] = h_ref[...] + jnp.dot(mix_ref[...], w_ref[...], preferred_element_type=F32)


def dilated_combine(outs, w_o, h, *, tm=256):
    m, d = h.shape
    wout = outs[0].shape[1]
    ospec = pl.BlockSpec((tm, wout), lambda i: (i, 0))
    return pl.pallas_call(
        _dil_combine_body,
        grid=(m // tm,),
        in_specs=[ospec, ospec, ospec,
                  pl.BlockSpec(w_o.shape, lambda i: (0, 0)),
                  pl.BlockSpec((tm, d), lambda i: (i, 0))],
        out_specs=pl.BlockSpec((tm, d), lambda i: (i, 0)),
        out_shape=jax.ShapeDtypeStruct((m, d), F32),
        scratch_shapes=[pltpu.VMEM((tm, w_o.shape[0]), BF16)],
        compiler_params=_cparams(("arbitrary",)),
        name="dilated_combine",
    )(*outs, w_o, h)


def _diff_attn_body(qi_ref, ki_ref, q_ref, k_ref, v_ref, bias_ref, lam_ref, g_ref, o_ref,
                    m_ref, l_ref, acc_ref, *, lambda_init):
    p = pl.program_id(1)
    qi = qi_ref[p]
    ki = ki_ref[p]
    nbatch = q_ref.shape[0]
    d = HEAD_DIM
    nt = (((1,), (1,)), ((), ()))

    @pl.when(ki == 0)
    def _():
        m_ref[...] = jnp.full(m_ref.shape, MASK_VALUE, F32)
        l_ref[...] = jnp.zeros_like(l_ref)
        acc_ref[...] = jnp.zeros_like(acc_ref)

    def batch_step(bi, carry):
        q = q_ref[bi] * jnp.asarray(HEAD_DIM ** -0.5, BF16)
        k = k_ref[bi]
        v = v_ref[bi]
        for mp in range(2):
            sl = slice(mp * d, (mp + 1) * d)
            s = lax.dot_general(q[:, sl], k[:, sl], nt, preferred_element_type=F32) + bias_ref[mp]
            m_old = m_ref[bi, mp]
            m_new = jnp.maximum(m_old, jnp.max(s, axis=-1, keepdims=True))
            alpha = jnp.exp(m_old - m_new)
            pr = jnp.exp(s - m_new)
            l_ref[bi, mp] = alpha * l_ref[bi, mp] + jnp.sum(pr, axis=-1, keepdims=True)
            acc_ref[bi, mp] = alpha * acc_ref[bi, mp] + jnp.dot(pr.astype(BF16), v, preferred_element_type=F32)
            m_ref[bi, mp] = m_new
        return carry

    lax.fori_loop(0, nbatch, batch_step, 0)

    @pl.when(ki == qi)
    def _():
        lam = (jnp.exp(jnp.sum(lam_ref[0:1, :] * lam_ref[1:2, :], axis=-1, keepdims=True))
               - jnp.exp(jnp.sum(lam_ref[2:3, :] * lam_ref[3:4, :], axis=-1, keepdims=True)) + lambda_init)
        gain = g_ref[...] * (1.0 - lambda_init)

        def finish(bi, carry):
            o = acc_ref[bi, 0] / l_ref[bi, 0] - lam * (acc_ref[bi, 1] / l_ref[bi, 1])
            o_ref[bi] = _rms(o, gain).astype(o_ref.dtype)
            return carry

        lax.fori_loop(0, nbatch, finish, 0)


def diff_attention(qkv, bias, lam_vecs, subln, lambda_init):
    b, s, _ = qkv.shape
    t = DIFF_TILE
    nq = s // t
    n_delta = bias.shape[2]
    pairs = [(qi, ki) for qi in range(nq) for ki in range(qi + 1)]
    qi_arr = jnp.asarray([pq for pq, _ in pairs], jnp.int32)
    ki_arr = jnp.asarray([pk for _, pk in pairs], jnp.int32)
    hw = 2 * HEAD_DIM
    nh = DIFF_HEADS
    grid_spec = pltpu.PrefetchScalarGridSpec(
        num_scalar_prefetch=2,
        grid=(nh, len(pairs)),
        in_specs=[
            pl.BlockSpec((b, t, hw), lambda h, p, qa, ka: (0, qa[p], h)),
            pl.BlockSpec((b, t, hw), lambda h, p, qa, ka: (0, ka[p], nh + h)),
            pl.BlockSpec((b, t, hw), lambda h, p, qa, ka: (0, ka[p], 2 * nh + h)),
            pl.BlockSpec((None, 2, None, t, t),
                         lambda h, p, qa, ka: (h, 0, jnp.minimum(qa[p] - ka[p], n_delta - 1), 0, 0)),
            pl.BlockSpec((4, HEAD_DIM), lambda h, p, qa, ka: (0, 0)),
            pl.BlockSpec((1, hw), lambda h, p, qa, ka: (0, 0)),
        ],
        out_specs=pl.BlockSpec((b, t, hw), lambda h, p, qa, ka: (0, qa[p], h)),
        scratch_shapes=[pltpu.VMEM((b, 2, t, 1), F32), pltpu.VMEM((b, 2, t, 1), F32),
                        pltpu.VMEM((b, 2, t, hw), F32)],
    )
    return pl.pallas_call(
        functools.partial(_diff_attn_body, lambda_init=lambda_init),
        grid_spec=grid_spec,
        out_shape=jax.ShapeDtypeStruct((b, s, nh * hw), BF16),
        compiler_params=_cparams(("arbitrary", "arbitrary")),
        name="diff_attention",
    )(qi_arr, ki_arr, qkv, qkv, qkv, bias, lam_vecs, subln.reshape(1, hw))


def _conv_silu(cur_ref, prev_ref, w_ref, b_ref, not_first):
    rows = cur_ref.shape[0]
    prev = prev_ref[...] * not_first
    xc = jnp.concatenate([prev, cur_ref[...]], axis=0)
    acc = b_ref[...]
    for k in range(SSM_CONV):
        off = SUBLANES - (SSM_CONV - 1) + k
        acc = acc + w_ref[k:k + 1, :] * xc[off:off + rows, :]
    return acc * jax.nn.sigmoid(acc)


def _cumsum_rows(a):
    rows = a.shape[0]
    ridx = lax.broadcasted_iota(jnp.int32, a.shape, 0)
    sh = 1
    while sh < rows:
        a = a + jnp.where(ridx >= sh, pltpu.roll(a, sh, 0), 0.0)
        sh *= 2
    return a


def _ssd_body(z_ref, x_ref, bm_ref, cm_ref, xp_ref, bp_ref, cp_ref, dt_ref,
              wx_ref, wb_ref, wc_ref, bx_ref, bb_ref, bc_ref, dtb_ref, alog_ref, dsk_ref, gn_ref,
              y_ref, state_ref):
    c = pl.program_id(1)
    l = SSM_CHUNK
    ns = SSM_D_STATE
    hg = SSM_HEADS // SSM_GROUPS
    gw = hg * SSM_HEAD_DIM

    @pl.when(c == 0)
    def _():
        state_ref[...] = jnp.zeros_like(state_ref)

    not_first = (c > 0).astype(F32)
    xs = _conv_silu(x_ref, xp_ref, wx_ref, bx_ref, not_first)
    bmat = _conv_silu(bm_ref, bp_ref, wb_ref, bb_ref, not_first).astype(BF16)
    cmat = _conv_silu(cm_ref, cp_ref, wc_ref, bc_ref, not_first).astype(BF16)

    dt = jax.nn.softplus(dt_ref[...] + dtb_ref[...])
    a_cs = _cumsum_rows(dt * (-jnp.exp(alog_ref[...])))
    a_cs_t = a_cs.T
    last = a_cs[l - 1:l, :]
    exp_a = jnp.exp(a_cs)
    state_decay = jnp.exp(last - a_cs)
    chunk_decay = jnp.exp(last)

    causal = (lax.broadcasted_iota(jnp.int32, (l, l), 0) >= lax.broadcasted_iota(jnp.int32, (l, l), 1))
    lane_head = lax.broadcasted_iota(jnp.int32, (1, gw), 1) // SSM_HEAD_DIM
    tn_dims = (((0,), (0,)), ((), ()))
    nt_dims = (((1,), (1,)), ((), ()))

    for g in range(SSM_GROUPS):
        def per_head(mat):
            out = mat[:, hg * g + hg - 1:hg * g + hg]
            for j in range(hg - 2, -1, -1):
                out = jnp.where(lane_head == j, mat[:, hg * g + j:hg * g + j + 1], out)
            return out

        cg = cmat[:, g * ns:(g + 1) * ns]
        bg = bmat[:, g * ns:(g + 1) * ns]
        xg = xs[:, g * gw:(g + 1) * gw]
        xdt = xg * per_head(dt)
        xdt_b = xdt.astype(BF16)
        cb = lax.dot_general(cg, bg, nt_dims, preferred_element_type=F32)
        parts = []
        for j in range(hg):
            hh = hg * g + j
            seg = a_cs[:, hh:hh + 1] - a_cs_t[hh:hh + 1, :]
            decay = jnp.where(causal, jnp.exp(seg), 0.0)
            parts.append(jnp.dot((cb * decay).astype(BF16), xdt_b[:, j * SSM_HEAD_DIM:(j + 1) * SSM_HEAD_DIM],
                                 preferred_element_type=F32))
        y_diag = jnp.concatenate(parts, axis=1)
        prev = state_ref[g]
        y_off = jnp.dot(cg, prev.astype(BF16), preferred_element_type=F32) * per_head(exp_a)
        new = lax.dot_general(bg, (xdt * per_head(state_decay)).astype(BF16), tn_dims,
                              preferred_element_type=F32)
        state_ref[g] = prev * per_head(chunk_decay) + new
        y = y_diag + y_off + xg * dsk_ref[:, g * gw:(g + 1) * gw]
        zg = z_ref[:, g * gw:(g + 1) * gw]
        y = y * (zg * jax.nn.sigmoid(zg))
        y_ref[:, g * gw:(g + 1) * gw] = _rms(y, gn_ref[:, g * gw:(g + 1) * gw]).astype(y_ref.dtype)


def ssd_core(zx, dt_raw, conv_w, conv_b, dt_bias, a_log, d_skip, gate_norm):
    b, s, _ = zx.shape
    l = SSM_CHUNK
    di = SSM_D_INNER
    gn = SSM_GROUPS * SSM_D_STATE
    per = l // SUBLANES
    z_spec = pl.BlockSpec((None, l, di), lambda bi, c: (bi, c, 0))
    x_spec = pl.BlockSpec((None, l, di), lambda bi, c: (bi, c, 1))
    b_spec = pl.BlockSpec((None, l, gn), lambda bi, c: (bi, c, 2 * di // gn))
    c_spec = pl.BlockSpec((None, l, gn), lambda bi, c: (bi, c, 2 * di // gn + 1))

    def tail(width, col):
        return pl.BlockSpec((None, SUBLANES, width), lambda bi, c: (bi, jnp.maximum(c * per - 1, 0), col))

    def whole(shape, col=0):
        return pl.BlockSpec(shape, lambda bi, c: (0, col))

    return pl.pallas_call(
        _ssd_body,
        grid=(b, s // l),
        in_specs=[z_spec, x_spec, b_spec, c_spec,
                  tail(di, 1), tail(gn, 2 * di // gn), tail(gn, 2 * di // gn + 1),
                  pl.BlockSpec((None, l, LANES), lambda bi, c: (bi, c, 0)),
                  whole((SSM_CONV, di), 0), whole((SSM_CONV, gn), di // gn), whole((SSM_CONV, gn), di // gn + 1),
                  whole((1, di), 0), whole((1, gn), di // gn), whole((1, gn), di // gn + 1),
                  whole((1, LANES)), whole((1, LANES)), whole((1, di)), whole((1, di))],
        out_specs=pl.BlockSpec((None, l, di), lambda bi, c: (bi, c, 0)),
        out_shape=jax.ShapeDtypeStruct((b, s, di), BF16),
        scratch_shapes=[pltpu.VMEM((SSM_GROUPS, SSM_D_STATE, (SSM_HEADS // SSM_GROUPS) * SSM_HEAD_DIM), F32)],
        compiler_params=_cparams(("arbitrary", "arbitrary")),
        name="ssd_core",
    )(zx, zx, zx, zx, zx, zx, zx, dt_raw,
      conv_w, conv_w, conv_w, conv_b, conv_b, conv_b, dt_bias, a_log, d_skip, gate_norm)


def _dilated_layer(h, norm_g, w_qkv, w_o, dil_biases):
    b, s, d = h.shape
    width = 3 * DIL_HEADS * HEAD_DIM
    outs = []
    for g, (_, dilation) in enumerate(DIL_PATTERNS):
        wg = w_qkv[:, g * width:(g + 1) * width].astype(BF16)
        qkv = norm_matmul(h, norm_g, wg, dilation=dilation)
        o = dilated_attention(qkv, dil_biases[g], dilation)
        outs.append(o.reshape(b * s, -1))
    return dilated_combine(outs, w_o.astype(BF16), h.reshape(b * s, d)).reshape(b, s, d)


def _diff_layer(h, norm_g, w_qkv, lam_q1, lam_k1, lam_q2, lam_k2, subln, w_o, bias, lambda_init):
    b, s, d = h.shape
    qkv = norm_matmul(h, norm_g, w_qkv.astype(BF16)).reshape(b, s, -1)
    lam_vecs = jnp.stack([lam_q1, lam_k1, lam_q2, lam_k2])
    o = diff_attention(qkv, bias, lam_vecs, subln, lambda_init)
    return matmul_residual(o.reshape(b * s, -1), w_o.astype(BF16), h.reshape(b * s, d)).reshape(b, s, d)


def _ssd_layer(h, norm_g, w_in, conv_w, conv_b, dt_bias, a_log, d_skip, gate_norm, w_out):
    b, s, d = h.shape
    wide = SSM_D_INNER + conv_w.shape[1]
    pad = LANES - SSM_HEADS
    zx = norm_matmul(h, norm_g, w_in[:, :wide].astype(BF16), tn=wide // 2, out_dtype=F32).reshape(b, s, wide)
    w_dt = jnp.pad(w_in[:, wide:], ((0, 0), (0, pad))).astype(BF16)
    dt_raw = norm_matmul(h, norm_g, w_dt, out_dtype=F32).reshape(b, s, LANES)
    y = ssd_core(zx, dt_raw, conv_w, conv_b.reshape(1, -1),
                 jnp.pad(dt_bias, (0, pad)).reshape(1, LANES), jnp.pad(a_log, (0, pad)).reshape(1, LANES),
                 jnp.repeat(d_skip, SSM_HEAD_DIM).reshape(1, SSM_D_INNER), gate_norm.reshape(1, SSM_D_INNER))
    return matmul_residual(y.reshape(b * s, -1), w_out.astype(BF16), h.reshape(b * s, d)).reshape(b, s, d)


def _diff_lambda_init(layer):
    return 0.8 - 0.6 * math.exp(-0.3 * layer)


def kernel(x, rel_bias,
           l0_mix_norm, l0_dil_w_qkv, l0_dil_w_o, l0_mlp_norm, l0_mlp_w_up, l0_mlp_w_down,
           l1_mix_norm, l1_diff_w_qkv, l1_diff_lam_q1, l1_diff_lam_k1, l1_diff_lam_q2, l1_diff_lam_k2,
           l1_diff_subln, l1_diff_w_o, l1_mlp_norm, l1_mlp_w_up, l1_mlp_w_down,
           l2_mix_norm, l2_ssm_w_in, l2_ssm_conv_w, l2_ssm_conv_b, l2_ssm_dt_bias, l2_ssm_A_log, l2_ssm_D,
           l2_ssm_gate_norm, l2_ssm_w_out, l2_mlp_norm, l2_mlp_w_up, l2_mlp_w_down,
           l3_mix_norm, l3_dil_w_qkv, l3_dil_w_o, l3_mlp_norm, l3_mlp_w_up, l3_mlp_w_down,
           final_norm):
    b, s, d = x.shape
    dil_biases = [dilated_bias(rel_bias, dilation) for _, dilation in DIL_PATTERNS]
    dif_bias = diff_bias(rel_bias, s)

    def mlp(h, g, w_up, w_down, final_gain=None):
        return mlp_block(h.reshape(b * s, d), g, w_up.astype(BF16), w_down.astype(BF16),
                         final_gain).reshape(b, s, d)

    h = _dilated_layer(x, l0_mix_norm, l0_dil_w_qkv, l0_dil_w_o, dil_biases)
    h = mlp(h, l0_mlp_norm, l0_mlp_w_up, l0_mlp_w_down)
    h = _diff_layer(h, l1_mix_norm, l1_diff_w_qkv, l1_diff_lam_q1, l1_diff_lam_k1, l1_diff_lam_q2,
                    l1_diff_lam_k2, l1_diff_subln, l1_diff_w_o, dif_bias, _diff_lambda_init(1))
    h = mlp(h, l1_mlp_norm, l1_mlp_w_up, l1_mlp_w_down)
    h = _ssd_layer(h, l2_mix_norm, l2_ssm_w_in, l2_ssm_conv_w, l2_ssm_conv_b, l2_ssm_dt_bias, l2_ssm_A_log,
                   l2_ssm_D, l2_ssm_gate_norm, l2_ssm_w_out)
    h = mlp(h, l2_mlp_norm, l2_mlp_w_up, l2_mlp_w_down)
    h = _dilated_layer(h, l3_mix_norm, l3_dil_w_qkv, l3_dil_w_o, dil_biases)
    return mlp(h, l3_mlp_norm, l3_mlp_w_up, l3_mlp_w_down, final_norm)
```

```python
import functools
import math

import jax
import jax.numpy as jnp
from jax import lax
from jax.experimental import pallas as pl
from jax.experimental.pallas import tpu as pltpu

F32 = jnp.float32
BF16 = jnp.bfloat16

NORM_EPS = 1e-5
HEAD_DIM = 64
DIL_PATTERNS = ((128, 1), (512, 4), (2048, 16))
DIL_HEADS = 16
DIL_STEPS = 128
DIFF_HEADS = 8
N_BUCKETS = 32
MAX_DISTANCE = 2048
SSM_HEADS = 32
SSM_GROUPS = 8
SSM_D_STATE = 128
SSM_HEAD_DIM = 64
SSM_D_INNER = SSM_HEADS * SSM_HEAD_DIM
SSM_CONV = 4
SSM_CHUNK = 128
LANES = 128
SUBLANES = 8
MASK_VALUE = -1e30
VMEM_LIMIT = 48 * 1024 * 1024

DIFF_TILE = 512


def _bucket_thresholds():
    exact = N_BUCKETS // 2

    def bucket(d):
        if d < exact:
            return d
        big = exact + int(math.log(d / exact) / math.log(MAX_DISTANCE / exact) * (N_BUCKETS - exact))
        return min(big, N_BUCKETS - 1)

    thr = {}
    d = 0
    for b in range(1, N_BUCKETS):
        while bucket(d) < b:
            d += 1
        thr[b] = d
    return thr, bucket


_BUCKET_THR, _bucket_of = _bucket_thresholds()


def _cparams(sem):
    return pltpu.CompilerParams(dimension_semantics=sem, vmem_limit_bytes=VMEM_LIMIT)


def _rms(x, gain):
    ms = jnp.mean(x * x, axis=-1, keepdims=True)
    return x * lax.rsqrt(ms + NORM_EPS) * gain


def _norm_matmul_body(x_ref, g_ref, w_ref, o_ref):
    xn = _rms(x_ref[...], g_ref[...]).astype(BF16)
    o_ref[...] = jnp.dot(xn, w_ref[...], preferred_element_type=F32).astype(o_ref.dtype)


def norm_matmul(x, gain, w, *, dilation=1, tm=256, tn=None, out_dtype=BF16):
    b, s, d = x.shape
    n = w.shape[1]
    r = dilation
    length = s // r
    tm = min(tm, length)
    tn = n if tn is None else tn
    xv = x.reshape(b, length, r * d)
    grid = (n // tn, b, r, length // tm)
    return pl.pallas_call(
        _norm_matmul_body,
        grid=grid,
        in_specs=[
            pl.BlockSpec((None, tm, d), lambda j, bi, rho, i: (bi, i, rho)),
            pl.BlockSpec((1, d), lambda j, bi, rho, i: (0, 0)),
            pl.BlockSpec((d, tn), lambda j, bi, rho, i: (0, j)),
        ],
        out_specs=pl.BlockSpec((None, None, tm, tn), lambda j, bi, rho, i: (bi, rho, i, j)),
        out_shape=jax.ShapeDtypeStruct((b, r, length, n), out_dtype),
        compiler_params=_cparams(("arbitrary",) * 4),
        name="norm_matmul",
    )(xv, gain.reshape(1, d), w)


def _matmul_res_body(a_ref, w_ref, r_ref, o_ref):
    o_ref[...] = r_ref[...] + jnp.dot(a_ref[...], w_ref[...], preferred_element_type=F32)


def matmul_residual(a, w, res, *, tm=512):
    m, k = a.shape
    n = w.shape[1]
    return pl.pallas_call(
        _matmul_res_body,
        grid=(m // tm,),
        in_specs=[
            pl.BlockSpec((tm, k), lambda i: (i, 0)),
            pl.BlockSpec((k, n), lambda i: (0, 0)),
            pl.BlockSpec((tm, n), lambda i: (i, 0)),
        ],
        out_specs=pl.BlockSpec((tm, n), lambda i: (i, 0)),
        out_shape=jax.ShapeDtypeStruct((m, n), F32),
        compiler_params=_cparams(("arbitrary",)),
        name="matmul_residual",
    )(a, w, res)


def _mlp_body(h_ref, g_ref, wu_ref, wd_ref, fg_ref, o_ref, xn_ref, acc_ref, *, final_norm):
    f = pl.program_id(1)

    @pl.when(f == 0)
    def _():
        xn_ref[...] = _rms(h_ref[...], g_ref[...]).astype(BF16)
        acc_ref[...] = jnp.zeros_like(acc_ref)

    u = jnp.maximum(jnp.dot(xn_ref[...], wu_ref[...], preferred_element_type=F32), 0.0)
    acc_ref[...] += jnp.dot((u * u).astype(BF16), wd_ref[...], preferred_element_type=F32)

    @pl.when(f == pl.num_programs(1) - 1)
    def _():
        out = h_ref[...] + acc_ref[...]
        if final_norm:
            out = _rms(out, fg_ref[...])
        o_ref[...] = out


def mlp_block(h, gain, w_up, w_down, final_gain=None, *, tm=512, tf=1024):
    m, d = h.shape
    f = w_up.shape[1]
    final_norm = final_gain is not None
    fg = (final_gain if final_norm else gain).reshape(1, d)
    return pl.pallas_call(
        functools.partial(_mlp_body, final_norm=final_norm),
        grid=(m // tm, f // tf),
        in_specs=[
            pl.BlockSpec((tm, d), lambda i, j: (i, 0)),
            pl.BlockSpec((1, d), lambda i, j: (0, 0)),
            pl.BlockSpec((d, tf), lambda i, j: (0, j)),
            pl.BlockSpec((tf, d), lambda i, j: (j, 0)),
            pl.BlockSpec((1, d), lambda i, j: (0, 0)),
        ],
        out_specs=pl.BlockSpec((tm, d), lambda i, j: (i, 0)),
        out_shape=jax.ShapeDtypeStruct((m, d), F32),
        scratch_shapes=[pltpu.VMEM((tm, d), BF16), pltpu.VMEM((tm, d), F32)],
        compiler_params=_cparams(("arbitrary", "arbitrary")),
        name="mlp_block",
    )(h, gain.reshape(1, d), w_up, w_down, fg)


def _bias_from_dist(dist, col_of_bucket, d_lo, d_hi):
    val = jnp.full(dist.shape, col_of_bucket(_bucket_of(d_lo)), F32)
    for b in range(1, N_BUCKETS):
        t = _BUCKET_THR[b]
        if d_lo < t <= d_hi:
            val = jnp.where(dist >= t, col_of_bucket(b), val)
    return val


def _dil_bias_body(tab_ref, o_ref, *, dilation):
    c = pl.program_id(0)
    n = DIL_STEPS
    qi = lax.broadcasted_iota(jnp.int32, (n, 2 * n), 0)
    kj = lax.broadcasted_iota(jnp.int32, (n, 2 * n), 1)
    steps = n + qi - kj
    band = (steps >= 0) & (steps <= n)
    dist = jnp.clip(steps, 0, n) * dilation
    val = _bias_from_dist(dist, lambda b: tab_ref[b, c], 0, n * dilation)
    later = jnp.where(band, val, MASK_VALUE)
    o_ref[1] = later
    o_ref[0] = jnp.where(kj >= n, later, MASK_VALUE)


def dilated_bias(rel_bias, dilation):
    n = DIL_STEPS
    return pl.pallas_call(
        functools.partial(_dil_bias_body, dilation=dilation),
        grid=(DIL_HEADS,),
        in_specs=[pl.BlockSpec(memory_space=pltpu.SMEM)],
        out_specs=pl.BlockSpec((2, None, n, 2 * n), lambda c: (0, c, 0, 0)),
        out_shape=jax.ShapeDtypeStruct((2, DIL_HEADS, n, 2 * n), F32),
        compiler_params=_cparams(("arbitrary",)),
        name="dilated_bias",
    )(rel_bias)


def _diff_bias_body(tab_ref, o_ref, *, n_delta):
    m = pl.program_id(0)
    h = pl.program_id(1)
    c = m * DIFF_HEADS + h
    t = DIFF_TILE
    qi = lax.broadcasted_iota(jnp.int32, (t, t), 0)
    kj = lax.broadcasted_iota(jnp.int32, (t, t), 1)
    for delta in range(n_delta):
        dist = delta * t + qi - kj
        d_lo = max(delta * t - (t - 1), 0)
        d_hi = delta * t + t - 1
        val = _bias_from_dist(dist, lambda b: tab_ref[b, c], d_lo, d_hi)
        if delta == 0:
            val = jnp.where(dist >= 0, val, MASK_VALUE)
        o_ref[delta] = val


def _diff_n_delta(s):
    t = DIFF_TILE
    last_thr = _BUCKET_THR[N_BUCKETS - 1]
    n_delta = 1
    while n_delta * t - (t - 1) < last_thr:
        n_delta += 1
    return min(n_delta + 1, s // t)


def diff_bias(rel_bias, s):
    t = DIFF_TILE
    n_delta = _diff_n_delta(s)
    return pl.pallas_call(
        functools.partial(_diff_bias_body, n_delta=n_delta),
        grid=(2, DIFF_HEADS),
        in_specs=[pl.BlockSpec(memory_space=pltpu.SMEM)],
        out_specs=pl.BlockSpec((None, None, n_delta, t, t), lambda m, h: (h, m, 0, 0, 0)),
        out_shape=jax.ShapeDtypeStruct((DIFF_HEADS, 2, n_delta, t, t), F32),
        compiler_params=_cparams(("arbitrary", "arbitrary")),
        name="diff_bias",
    )(rel_bias)


def _dil_attn_body(q_ref, kp_ref, kc_ref, vp_ref, vc_ref, bias_ref, o_ref):
    n = DIL_STEPS
    hd = DIL_HEADS * HEAD_DIM
    nt = (((1,), (1,)), ((), ()))
    lane = lax.broadcasted_iota(jnp.int32, (1, LANES), 1)
    low = lane < HEAD_DIM
    ones_low = jnp.where(low, 1, 0).astype(BF16) + jnp.zeros((2 * n, LANES), BF16)
    ones_high = jnp.where(low, 0, 1).astype(BF16) + jnp.zeros((2 * n, LANES), BF16)
    lse_tile = jnp.zeros((n, LANES), F32)
    for pr in range(hd // LANES):
        cs = slice(pr * LANES, (pr + 1) * LANES)
        q2 = q_ref[:, cs] * jnp.asarray(HEAD_DIM ** -0.5, BF16)
        k2 = jnp.concatenate([kp_ref[:, cs], kc_ref[:, cs]], axis=0)
        v2 = jnp.concatenate([vp_ref[:, cs], vc_ref[:, cs]], axis=0)
        qs = jnp.concatenate([jnp.where(low, q2, 0), jnp.where(low, 0, q2)], axis=0)
        bias = bias_ref[2 * pr:2 * pr + 2].reshape(2 * n, 2 * n)
        s = lax.dot_general(qs, k2, nt, preferred_element_type=F32) + bias
        m = jnp.max(jnp.maximum(s[:, :n], s[:, n:]), axis=-1, keepdims=True)
        p = jnp.exp(s - m).astype(BF16)
        w = jnp.concatenate(
            [jnp.concatenate([jnp.where(low, v2, 0), ones_low], axis=1),
             jnp.concatenate([jnp.where(low, 0, v2), ones_high], axis=1)], axis=0)
        r = jnp.dot(jnp.concatenate([p[:n], p[n:]], axis=1), w, preferred_element_type=F32)
        den = r[:, LANES:]
        o_ref[:, cs] = r[:, :LANES] / den
        lse = jnp.where(low, m[:n], m[n:]) + jnp.log(den)
        lse_tile = jnp.where((lane == pr) | (lane == HEAD_DIM + pr), lse, lse_tile)
    o_ref[:, hd:] = lse_tile


def dilated_attention(qkv, bias, dilation):
    b, r, length, _ = qkv.shape
    n = DIL_STEPS
    hd = DIL_HEADS * HEAD_DIM
    wout = hd + LANES
    nb = length // n

    def cur(col):
        return pl.BlockSpec((None, None, n, hd), lambda bi, rho, i: (bi, rho, i, col))

    def prev(col):
        return pl.BlockSpec((None, None, n, hd), lambda bi, rho, i: (bi, rho, jnp.maximum(i - 1, 0), col))

    return pl.pallas_call(
        _dil_attn_body,
        grid=(b, r, nb),
        in_specs=[cur(0), prev(1), cur(1), prev(2), cur(2),
                  pl.BlockSpec((None, DIL_HEADS, n, 2 * n), lambda bi, rho, i: (jnp.minimum(i, 1), 0, 0, 0))],
        out_specs=pl.BlockSpec((None, n, wout), lambda bi, rho, i: (bi, i, rho)),
        out_shape=jax.ShapeDtypeStruct((b, length, r * wout), F32),
        compiler_params=_cparams(("arbitrary",) * 3),
        name=f"dilated_attention_r{dilation}",
    )(qkv, qkv, qkv, qkv, qkv, bias)


def _dil_combine_body(o1_ref, o2_ref, o3_ref, w_ref, h_ref, out_ref, mix_ref):
    hd = DIL_HEADS * HEAD_DIM
    groups = (o1_ref, o2_ref, o3_ref)
    lses = [g[:, hd:] for g in groups]
    mx = jnp.maximum(jnp.maximum(lses[0], lses[1]), lses[2])
    es = [jnp.exp(l - mx) for l in lses]
    inv = 1.0 / (es[0] + es[1] + es[2])
    ws = [e * inv for e in es]
    lane = lax.broadcasted_iota(jnp.int32, (1, LANES), 1)
    for sl in range(hd // LANES):
        acc = None
        for g, w in zip(groups, ws):
            wl = jnp.where(lane < HEAD_DIM, w[:, sl:sl + 1], w[:, HEAD_DIM + sl:HEAD_DIM + sl + 1])
            term = wl * g[:, sl * LANES:(sl + 1) * LANES]
            acc = term if acc is None else acc + term
        mix_ref[:, sl * LANES:(sl + 1) * LANES] = acc.astype(BF16)
    out_ref[...] = h_ref[...] + jnp.dot(mix_ref[...], w_ref[...], preferred_element_type=F32)


def dilated_combine(outs, w_o, h, *, tm=256):
    m, d = h.shape
    wout = outs[0].shape[1]
    ospec = pl.BlockSpec((tm, wout), lambda i: (i, 0))
    return pl.pallas_call(
        _dil_combine_body,
        grid=(m // tm,),
        in_specs=[ospec, ospec, ospec,
                  pl.BlockSpec(w_o.shape, lambda i: (0, 0)),
                  pl.BlockSpec((tm, d), lambda i: (i, 0))],
        out_specs=pl.BlockSpec((tm, d), lambda i: (i, 0)),
        out_shape=jax.ShapeDtypeStruct((m, d), F32),
        scratch_shapes=[pltpu.VMEM((tm, w_o.shape[0]), BF16)],
        compiler_params=_cparams(("arbitrary",)),
        name="dilated_combine",
    )(*outs, w_o, h)


def _diff_attn_body(qi_ref, ki_ref, q_ref, k_ref, v_ref, bias_ref, lam_ref, g_ref, o_ref,
                    m_ref, acc_ref, *, lambda_init):
    p = pl.program_id(1)
    qi = qi_ref[p]
    ki = ki_ref[p]
    nbatch, t, hw = q_ref.shape
    nt = (((1,), (1,)), ((), ()))
    low = lax.broadcasted_iota(jnp.int32, (1, hw), 1) < HEAD_DIM

    @pl.when(ki == 0)
    def _():
        m_ref[...] = jnp.full(m_ref.shape, MASK_VALUE, F32)
        acc_ref[...] = jnp.zeros_like(acc_ref)

    def batch_step(bi, carry):
        q = q_ref[bi] * jnp.asarray(HEAD_DIM ** -0.5, BF16)
        k = k_ref[bi]
        v1 = jnp.concatenate([v_ref[bi], jnp.ones((t, hw), BF16)], axis=1)
        for mp in range(2):
            qm = jnp.where(low, q, 0) if mp == 0 else jnp.where(low, 0, q)
            s = lax.dot_general(qm, k, nt, preferred_element_type=F32) + bias_ref[mp]
            m_old = m_ref[bi, mp]
            m_new = jnp.maximum(m_old, jnp.max(s, axis=-1, keepdims=True))
            alpha = jnp.exp(m_old - m_new)
            pr = jnp.exp(s - jnp.concatenate([m_new] * (t // hw), axis=1))
            acc_ref[bi, mp] = (jnp.concatenate([alpha, alpha], axis=1) * acc_ref[bi, mp]
                               + jnp.dot(pr.astype(BF16), v1, preferred_element_type=F32))
            m_ref[bi, mp] = m_new
        return carry

    lax.fori_loop(0, nbatch, batch_step, 0)

    @pl.when(ki == qi)
    def _():
        lam = (jnp.exp(jnp.sum(lam_ref[0:1, :] * lam_ref[1:2, :], axis=-1, keepdims=True))
               - jnp.exp(jnp.sum(lam_ref[2:3, :] * lam_ref[3:4, :], axis=-1, keepdims=True)) + lambda_init)
        gain = g_ref[...] * (1.0 - lambda_init)

        def finish(bi, carry):
            a0 = acc_ref[bi, 0]
            a1 = acc_ref[bi, 1]
            o = a0[:, :hw] / a0[:, hw:] - lam * (a1[:, :hw] / a1[:, hw:])
            o_ref[bi] = _rms(o, gain).astype(o_ref.dtype)
            return carry

        lax.fori_loop(0, nbatch, finish, 0)


def diff_attention(qkv, bias, lam_vecs, subln, lambda_init):
    b, s, _ = qkv.shape
    t = DIFF_TILE
    nq = s // t
    n_delta = bias.shape[2]
    pairs = [(qi, ki) for qi in range(nq) for ki in range(qi + 1)]
    qi_arr = jnp.asarray([pq for pq, _ in pairs], jnp.int32)
    ki_arr = jnp.asarray([pk for _, pk in pairs], jnp.int32)
    hw = 2 * HEAD_DIM
    nh = DIFF_HEADS
    grid_spec = pltpu.PrefetchScalarGridSpec(
        num_scalar_prefetch=2,
        grid=(nh, len(pairs)),
        in_specs=[
            pl.BlockSpec((b, t, hw), lambda h, p, qa, ka: (0, qa[p], h)),
            pl.BlockSpec((b, t, hw), lambda h, p, qa, ka: (0, ka[p], nh + h)),
            pl.BlockSpec((b, t, hw), lambda h, p, qa, ka: (0, ka[p], 2 * nh + h)),
            pl.BlockSpec((None, 2, None, t, t),
                         lambda h, p, qa, ka: (h, 0, jnp.minimum(qa[p] - ka[p], n_delta - 1), 0, 0)),
            pl.BlockSpec((4, HEAD_DIM), lambda h, p, qa, ka: (0, 0)),
            pl.BlockSpec((1, hw), lambda h, p, qa, ka: (0, 0)),
        ],
        out_specs=pl.BlockSpec((b, t, hw), lambda h, p, qa, ka: (0, qa[p], h)),
        scratch_shapes=[pltpu.VMEM((b, 2, t, hw), F32), pltpu.VMEM((b, 2, t, 2 * hw), F32)],
    )
    return pl.pallas_call(
        functools.partial(_diff_attn_body, lambda_init=lambda_init),
        grid_spec=grid_spec,
        out_shape=jax.ShapeDtypeStruct((b, s, nh * hw), BF16),
        compiler_params=_cparams(("arbitrary", "arbitrary")),
        name="diff_attention",
    )(qi_arr, ki_arr, qkv, qkv, qkv, bias, lam_vecs, subln.reshape(1, hw))


def _conv_silu(cur_ref, prev_ref, w_ref, b_ref, not_first):
    rows = cur_ref.shape[0]
    prev = prev_ref[...] * not_first
    xc = jnp.concatenate([prev, cur_ref[...]], axis=0)
    acc = b_ref[...]
    for k in range(SSM_CONV):
        off = SUBLANES - (SSM_CONV - 1) + k
        acc = acc + w_ref[k:k + 1, :] * xc[off:off + rows, :]
    return acc * jax.nn.sigmoid(acc)


def _cumsum_rows(a):
    rows = a.shape[0]
    ridx = lax.broadcasted_iota(jnp.int32, a.shape, 0)
    sh = 1
    while sh < rows:
        a = a + jnp.where(ridx >= sh, pltpu.roll(a, sh, 0), 0.0)
        sh *= 2
    return a


def _ssd_body(z_ref, x_ref, bm_ref, cm_ref, xp_ref, bp_ref, cp_ref, dt_ref,
              wx_ref, wb_ref, wc_ref, bx_ref, bb_ref, bc_ref, dtb_ref, alog_ref, dsk_ref, gn_ref,
              y_ref, state_ref):
    c = pl.program_id(1)
    l = SSM_CHUNK
    ns = SSM_D_STATE
    hg = SSM_HEADS // SSM_GROUPS
    gw = hg * SSM_HEAD_DIM

    @pl.when(c == 0)
    def _():
        state_ref[...] = jnp.zeros_like(state_ref)

    not_first = (c > 0).astype(F32)
    xs = _conv_silu(x_ref, xp_ref, wx_ref, bx_ref, not_first)
    bmat = _conv_silu(bm_ref, bp_ref, wb_ref, bb_ref, not_first).astype(BF16)
    cmat = _conv_silu(cm_ref, cp_ref, wc_ref, bc_ref, not_first).astype(BF16)

    dt = jax.nn.softplus(dt_ref[...] + dtb_ref[...])
    a_cs = _cumsum_rows(dt * (-jnp.exp(alog_ref[...])))
    a_cs_t = a_cs.T
    last = a_cs[l - 1:l, :]
    exp_a = jnp.exp(a_cs)
    state_decay = jnp.exp(last - a_cs)
    chunk_decay = jnp.exp(last)

    causal = (lax.broadcasted_iota(jnp.int32, (l, l), 0) >= lax.broadcasted_iota(jnp.int32, (l, l), 1))
    lane_head = lax.broadcasted_iota(jnp.int32, (1, gw), 1) // SSM_HEAD_DIM
    tn_dims = (((0,), (0,)), ((), ()))
    nt_dims = (((1,), (1,)), ((), ()))

    for g in range(SSM_GROUPS):
        def per_head(mat):
            out = mat[:, hg * g + hg - 1:hg * g + hg]
            for j in range(hg - 2, -1, -1):
                out = jnp.where(lane_head == j, mat[:, hg * g + j:hg * g + j + 1], out)
            return out

        cg = cmat[:, g * ns:(g + 1) * ns]
        bg = bmat[:, g * ns:(g + 1) * ns]
        xg = xs[:, g * gw:(g + 1) * gw]
        xdt = xg * per_head(dt)
        xdt_b = xdt.astype(BF16)
        cb = lax.dot_general(cg, bg, nt_dims, preferred_element_type=F32)
        parts = []
        for j in range(hg):
            hh = hg * g + j
            seg = a_cs[:, hh:hh + 1] - a_cs_t[hh:hh + 1, :]
            decay = jnp.where(causal, jnp.exp(seg), 0.0)
            parts.append(jnp.dot((cb * decay).astype(BF16), xdt_b[:, j * SSM_HEAD_DIM:(j + 1) * SSM_HEAD_DIM],
                                 preferred_element_type=F32))
        y_diag = jnp.concatenate(parts, axis=1)
        prev = state_ref[g]
        y_off = jnp.dot(cg, prev.astype(BF16), preferred_element_type=F32) * per_head(exp_a)
        new = lax.dot_general(bg, (xdt * per_head(state_decay)).astype(BF16), tn_dims,
                              preferred_element_type=F32)
        state_ref[g] = prev * per_head(chunk_decay) + new
        y = y_diag + y_off + xg * dsk_ref[:, g * gw:(g + 1) * gw]
        zg = z_ref[:, g * gw:(g + 1) * gw]
        y = y * (zg * jax.nn.sigmoid(zg))
        y_ref[:, g * gw:(g + 1) * gw] = _rms(y, gn_ref[:, g * gw:(g + 1) * gw]).astype(y_ref.dtype)


def ssd_core(zx, dt_raw, conv_w, conv_b, dt_bias, a_log, d_skip, gate_norm):
    b, s, _ = zx.shape
    l = SSM_CHUNK
    di = SSM_D_INNER
    gn = SSM_GROUPS * SSM_D_STATE
    per = l // SUBLANES
    z_spec = pl.BlockSpec((None, l, di), lambda bi, c: (bi, c, 0))
    x_spec = pl.BlockSpec((None, l, di), lambda bi, c: (bi, c, 1))
    b_spec = pl.BlockSpec((None, l, gn), lambda bi, c: (bi, c, 2 * di // gn))
    c_spec = pl.BlockSpec((None, l, gn), lambda bi, c: (bi, c, 2 * di // gn + 1))

    def tail(width, col):
        return pl.BlockSpec((None, SUBLANES, width), lambda bi, c: (bi, jnp.maximum(c * per - 1, 0), col))

    def whole(shape, col=0):
        return pl.BlockSpec(shape, lambda bi, c: (0, col))

    return pl.pallas_call(
        _ssd_body,
        grid=(b, s // l),
        in_specs=[z_spec, x_spec, b_spec, c_spec,
                  tail(di, 1), tail(gn, 2 * di // gn), tail(gn, 2 * di // gn + 1),
                  pl.BlockSpec((None, l, LANES), lambda bi, c: (bi, c, 0)),
                  whole((SSM_CONV, di), 0), whole((SSM_CONV, gn), di // gn), whole((SSM_CONV, gn), di // gn + 1),
                  whole((1, di), 0), whole((1, gn), di // gn), whole((1, gn), di // gn + 1),
                  whole((1, LANES)), whole((1, LANES)), whole((1, di)), whole((1, di))],
        out_specs=pl.BlockSpec((None, l, di), lambda bi, c: (bi, c, 0)),
        out_shape=jax.ShapeDtypeStruct((b, s, di), BF16),
        scratch_shapes=[pltpu.VMEM((SSM_GROUPS, SSM_D_STATE, (SSM_HEADS // SSM_GROUPS) * SSM_HEAD_DIM), F32)],
        compiler_params=_cparams(("arbitrary", "arbitrary")),
        name="ssd_core",
    )(zx, zx, zx, zx, zx, zx, zx, dt_raw,
      conv_w, conv_w, conv_w, conv_b, conv_b, conv_b, dt_bias, a_log, d_skip, gate_norm)


def _dilated_layer(h, norm_g, w_qkv, w_o, dil_biases):
    b, s, d = h.shape
    width = 3 * DIL_HEADS * HEAD_DIM
    outs = []
    for g, (_, dilation) in enumerate(DIL_PATTERNS):
        wg = w_qkv[:, g * width:(g + 1) * width].astype(BF16)
        qkv = norm_matmul(h, norm_g, wg, dilation=dilation)
        o = dilated_attention(qkv, dil_biases[g], dilation)
        outs.append(o.reshape(b * s, -1))
    return dilated_combine(outs, w_o.astype(BF16), h.reshape(b * s, d)).reshape(b, s, d)


def _diff_layer(h, norm_g, w_qkv, lam_q1, lam_k1, lam_q2, lam_k2, subln, w_o, bias, lambda_init):
    b, s, d = h.shape
    qkv = norm_matmul(h, norm_g, w_qkv.astype(BF16)).reshape(b, s, -1)
    lam_vecs = jnp.stack([lam_q1, lam_k1, lam_q2, lam_k2])
    o = diff_attention(qkv, bias, lam_vecs, subln, lambda_init)
    return matmul_residual(o.reshape(b * s, -1), w_o.astype(BF16), h.reshape(b * s, d)).reshape(b, s, d)


def _ssd_layer(h, norm_g, w_in, conv_w, conv_b, dt_bias, a_log, d_skip, gate_norm, w_out):
    b, s, d = h.shape
    wide = SSM_D_INNER + conv_w.shape[1]
    pad = LANES - SSM_HEADS
    zx = norm_matmul(h, norm_g, w_in[:, :wide].astype(BF16), tn=wide // 2, out_dtype=F32).reshape(b, s, wide)
    w_dt = jnp.pad(w_in[:, wide:], ((0, 0), (0, pad))).astype(BF16)
    dt_raw = norm_matmul(h, norm_g, w_dt, out_dtype=F32).reshape(b, s, LANES)
    y = ssd_core(zx, dt_raw, conv_w, conv_b.reshape(1, -1),
                 jnp.pad(dt_bias, (0, pad)).reshape(1, LANES), jnp.pad(a_log, (0, pad)).reshape(1, LANES),
                 jnp.repeat(d_skip, SSM_HEAD_DIM).reshape(1, SSM_D_INNER), gate_norm.reshape(1, SSM_D_INNER))
    return matmul_residual(y.reshape(b * s, -1), w_out.astype(BF16), h.reshape(b * s, d)).reshape(b, s, d)


def _diff_lambda_init(layer):
    return 0.8 - 0.6 * math.exp(-0.3 * layer)


def kernel(x, rel_bias,
           l0_mix_norm, l0_dil_w_qkv, l0_dil_w_o, l0_mlp_norm, l0_mlp_w_up, l0_mlp_w_down,
           l1_mix_norm, l1_diff_w_qkv, l1_diff_lam_q1, l1_diff_lam_k1, l1_diff_lam_q2, l1_diff_lam_k2,
           l1_diff_subln, l1_diff_w_o, l1_mlp_norm, l1_mlp_w_up, l1_mlp_w_down,
           l2_mix_norm, l2_ssm_w_in, l2_ssm_conv_w, l2_ssm_conv_b, l2_ssm_dt_bias, l2_ssm_A_log, l2_ssm_D,
           l2_ssm_gate_norm, l2_ssm_w_out, l2_mlp_norm, l2_mlp_w_up, l2_mlp_w_down,
           l3_mix_norm, l3_dil_w_qkv, l3_dil_w_o, l3_mlp_norm, l3_mlp_w_up, l3_mlp_w_down,
           final_norm):
    b, s, d = x.shape
    dil_biases = [dilated_bias(rel_bias, dilation) for _, dilation in DIL_PATTERNS]
    dif_bias = diff_bias(rel_bias, s)

    def mlp(h, g, w_up, w_down, final_gain=None):
        return mlp_block(h.reshape(b * s, d), g, w_up.astype(BF16), w_down.astype(BF16),
                         final_gain).reshape(b, s, d)

    h = _dilated_layer(x, l0_mix_norm, l0_dil_w_qkv, l0_dil_w_o, dil_biases)
    h = mlp(h, l0_mlp_norm, l0_mlp_w_up, l0_mlp_w_down)
    h = _diff_layer(h, l1_mix_norm, l1_diff_w_qkv, l1_diff_lam_q1, l1_diff_lam_k1, l1_diff_lam_q2,
                    l1_diff_lam_k2, l1_diff_subln, l1_diff_w_o, dif_bias, _diff_lambda_init(1))
    h = mlp(h, l1_mlp_norm, l1_mlp_w_up, l1_mlp_w_down)
    h = _ssd_layer(h, l2_mix_norm, l2_ssm_w_in, l2_ssm_conv_w, l2_ssm_conv_b, l2_ssm_dt_bias, l2_ssm_A_log,
                   l2_ssm_D, l2_ssm_gate_norm, l2_ssm_w_out)
    h = mlp(h, l2_mlp_norm, l2_mlp_w_up, l2_mlp_w_down)
    h = _dilated_layer(h, l3_mix_norm, l3_dil_w_qkv, l3_dil_w_o, dil_biases)
    return mlp(h, l3_mlp_norm, l3_mlp_w_up, l3_mlp_w_down, final_norm)
```

```python
import functools
import math

import jax
import jax.numpy as jnp
from jax import lax
from jax.experimental import pallas as pl
from jax.experimental.pallas import tpu as pltpu

F32 = jnp.float32
BF16 = jnp.bfloat16

NORM_EPS = 1e-5
HEAD_DIM = 64
DIL_PATTERNS = ((128, 1), (512, 4), (2048, 16))
DIL_HEADS = 16
DIL_STEPS = 128
DIFF_HEADS = 8
N_BUCKETS = 32
MAX_DISTANCE = 2048
SSM_HEADS = 32
SSM_GROUPS = 8
SSM_D_STATE = 128
SSM_HEAD_DIM = 64
SSM_D_INNER = SSM_HEADS * SSM_HEAD_DIM
SSM_CONV = 4
SSM_CHUNK = 128
LANES = 128
SUBLANES = 8
MASK_VALUE = -1e30
VMEM_LIMIT = 48 * 1024 * 1024

DIFF_TILE = 512


def _bucket_thresholds():
    exact = N_BUCKETS // 2

    def bucket(d):
        if d < exact:
            return d
        big = exact + int(math.log(d / exact) / math.log(MAX_DISTANCE / exact) * (N_BUCKETS - exact))
        return min(big, N_BUCKETS - 1)

    thr = {}
    d = 0
    for b in range(1, N_BUCKETS):
        while bucket(d) < b:
            d += 1
        thr[b] = d
    return thr, bucket


_BUCKET_THR, _bucket_of = _bucket_thresholds()


def _cparams(sem):
    return pltpu.CompilerParams(dimension_semantics=sem, vmem_limit_bytes=VMEM_LIMIT)


def _rms(x, gain):
    ms = jnp.mean(x * x, axis=-1, keepdims=True)
    return x * lax.rsqrt(ms + NORM_EPS) * gain


def _norm_matmul_body(x_ref, g_ref, w_ref, o_ref):
    d = g_ref.shape[1]
    for j in range(o_ref.shape[0]):
        xn = _rms(x_ref[:, j * d:(j + 1) * d], g_ref[...]).astype(BF16)
        o_ref[j] = jnp.dot(xn, w_ref[...], preferred_element_type=F32).astype(o_ref.dtype)


def norm_matmul(x, gain, w, *, dilation=1, rows=512, tn=None, out_dtype=BF16):
    b, s, d = x.shape
    n = w.shape[1]
    r = dilation
    length = s // r
    tm = min(rows, length)
    nres = min(rows // tm, r)
    tn = n if tn is None else tn
    xv = x.reshape(b, length, r * d)
    grid = (n // tn, b, r // nres, length // tm)
    return pl.pallas_call(
        _norm_matmul_body,
        grid=grid,
        in_specs=[
            pl.BlockSpec((None, tm, nres * d), lambda j, bi, rho, i: (bi, i, rho)),
            pl.BlockSpec((1, d), lambda j, bi, rho, i: (0, 0)),
            pl.BlockSpec((d, tn), lambda j, bi, rho, i: (0, j)),
        ],
        out_specs=pl.BlockSpec((None, nres, tm, tn), lambda j, bi, rho, i: (bi, rho, i, j)),
        out_shape=jax.ShapeDtypeStruct((b, r, length, n), out_dtype),
        compiler_params=_cparams(("arbitrary",) * 4),
        name="norm_matmul",
    )(xv, gain.reshape(1, d), w)


def _matmul_res_body(a_ref, w_ref, r_ref, o_ref):
    o_ref[...] = r_ref[...] + jnp.dot(a_ref[...], w_ref[...], preferred_element_type=F32)


def matmul_residual(a, w, res, *, tm=512):
    m, k = a.shape
    n = w.shape[1]
    return pl.pallas_call(
        _matmul_res_body,
        grid=(m // tm,),
        in_specs=[
            pl.BlockSpec((tm, k), lambda i: (i, 0)),
            pl.BlockSpec((k, n), lambda i: (0, 0)),
            pl.BlockSpec((tm, n), lambda i: (i, 0)),
        ],
        out_specs=pl.BlockSpec((tm, n), lambda i: (i, 0)),
        out_shape=jax.ShapeDtypeStruct((m, n), F32),
        compiler_params=_cparams(("arbitrary",)),
        name="matmul_residual",
    )(a, w, res)


def _mlp_body(h_ref, g_ref, wu_ref, wd_ref, fg_ref, o_ref, xn_ref, acc_ref, *, final_norm):
    f = pl.program_id(1)

    @pl.when(f == 0)
    def _():
        xn_ref[...] = _rms(h_ref[...], g_ref[...]).astype(BF16)
        acc_ref[...] = jnp.zeros_like(acc_ref)

    u = jnp.maximum(jnp.dot(xn_ref[...], wu_ref[...], preferred_element_type=F32), 0.0)
    acc_ref[...] += jnp.dot((u * u).astype(BF16), wd_ref[...], preferred_element_type=F32)

    @pl.when(f == pl.num_programs(1) - 1)
    def _():
        out = h_ref[...] + acc_ref[...]
        if final_norm:
            out = _rms(out, fg_ref[...])
        o_ref[...] = out


def mlp_block(h, gain, w_up, w_down, final_gain=None, *, tm=1024, tf=1024):
    m, d = h.shape
    f = w_up.shape[1]
    final_norm = final_gain is not None
    fg = (final_gain if final_norm else gain).reshape(1, d)
    return pl.pallas_call(
        functools.partial(_mlp_body, final_norm=final_norm),
        grid=(m // tm, f // tf),
        in_specs=[
            pl.BlockSpec((tm, d), lambda i, j: (i, 0)),
            pl.BlockSpec((1, d), lambda i, j: (0, 0)),
            pl.BlockSpec((d, tf), lambda i, j: (0, j)),
            pl.BlockSpec((tf, d), lambda i, j: (j, 0)),
            pl.BlockSpec((1, d), lambda i, j: (0, 0)),
        ],
        out_specs=pl.BlockSpec((tm, d), lambda i, j: (i, 0)),
        out_shape=jax.ShapeDtypeStruct((m, d), F32),
        scratch_shapes=[pltpu.VMEM((tm, d), BF16), pltpu.VMEM((tm, d), F32)],
        compiler_params=_cparams(("arbitrary", "arbitrary")),
        name="mlp_block",
    )(h, gain.reshape(1, d), w_up, w_down, fg)


def _bias_from_dist(dist, col_of_bucket, d_lo, d_hi):
    val = jnp.full(dist.shape, col_of_bucket(_bucket_of(d_lo)), F32)
    for b in range(1, N_BUCKETS):
        t = _BUCKET_THR[b]
        if d_lo < t <= d_hi:
            val = jnp.where(dist >= t, col_of_bucket(b), val)
    return val


def _dil_bias_body(tab_ref, o_ref, *, dilation):
    c = pl.program_id(0)
    n = DIL_STEPS
    qi = lax.broadcasted_iota(jnp.int32, (n, 2 * n), 0)
    kj = lax.broadcasted_iota(jnp.int32, (n, 2 * n), 1)
    steps = n + qi - kj
    band = (steps >= 0) & (steps <= n)
    dist = jnp.clip(steps, 0, n) * dilation
    val = _bias_from_dist(dist, lambda b: tab_ref[b, c], 0, n * dilation)
    later = jnp.where(band, val, MASK_VALUE)
    o_ref[1] = later
    o_ref[0] = jnp.where(kj >= n, later, MASK_VALUE)


def dilated_bias(rel_bias, dilation):
    n = DIL_STEPS
    return pl.pallas_call(
        functools.partial(_dil_bias_body, dilation=dilation),
        grid=(DIL_HEADS,),
        in_specs=[pl.BlockSpec(memory_space=pltpu.SMEM)],
        out_specs=pl.BlockSpec((2, None, n, 2 * n), lambda c: (0, c, 0, 0)),
        out_shape=jax.ShapeDtypeStruct((2, DIL_HEADS, n, 2 * n), F32),
        compiler_params=_cparams(("arbitrary",)),
        name="dilated_bias",
    )(rel_bias)


def _diff_bias_body(tab_ref, o_ref, *, n_delta):
    m = pl.program_id(0)
    h = pl.program_id(1)
    c = m * DIFF_HEADS + h
    t = DIFF_TILE
    qi = lax.broadcasted_iota(jnp.int32, (t, t), 0)
    kj = lax.broadcasted_iota(jnp.int32, (t, t), 1)
    for delta in range(n_delta):
        dist = delta * t + qi - kj
        d_lo = max(delta * t - (t - 1), 0)
        d_hi = delta * t + t - 1
        val = _bias_from_dist(dist, lambda b: tab_ref[b, c], d_lo, d_hi)
        if delta == 0:
            val = jnp.where(dist >= 0, val, MASK_VALUE)
        o_ref[delta] = val


def _diff_n_delta(s):
    t = DIFF_TILE
    last_thr = _BUCKET_THR[N_BUCKETS - 1]
    n_delta = 1
    while n_delta * t - (t - 1) < last_thr:
        n_delta += 1
    return min(n_delta + 1, s // t)


def diff_bias(rel_bias, s):
    t = DIFF_TILE
    n_delta = _diff_n_delta(s)
    return pl.pallas_call(
        functools.partial(_diff_bias_body, n_delta=n_delta),
        grid=(2, DIFF_HEADS),
        in_specs=[pl.BlockSpec(memory_space=pltpu.SMEM)],
        out_specs=pl.BlockSpec((None, None, n_delta, t, t), lambda m, h: (h, m, 0, 0, 0)),
        out_shape=jax.ShapeDtypeStruct((DIFF_HEADS, 2, n_delta, t, t), F32),
        compiler_params=_cparams(("arbitrary", "arbitrary")),
        name="diff_bias",
    )(rel_bias)


def _dil_attn_body(q_ref, kc_ref, vc_ref, bias_ref, o_ref, lse_ref, kp_ref, vp_ref):
    n = DIL_STEPS
    hd = DIL_HEADS * HEAD_DIM
    nt = (((1,), (1,)), ((), ()))
    lane = lax.broadcasted_iota(jnp.int32, (1, LANES), 1)
    low = lane < HEAD_DIM
    ones_low = jnp.where(low, 1, 0).astype(BF16) + jnp.zeros((2 * n, LANES), BF16)
    ones_high = jnp.where(low, 0, 1).astype(BF16) + jnp.zeros((2 * n, LANES), BF16)

    @pl.when(pl.program_id(2) == 0)
    def _():
        kp_ref[...] = jnp.zeros_like(kp_ref)
        vp_ref[...] = jnp.zeros_like(vp_ref)

    lse_tile = jnp.zeros((n, LANES), F32)
    for pr in range(hd // LANES):
        cs = slice(pr * LANES, (pr + 1) * LANES)
        q2 = q_ref[:, cs] * jnp.asarray(HEAD_DIM ** -0.5, BF16)
        k2 = jnp.concatenate([kp_ref[:, cs], kc_ref[:, cs]], axis=0)
        v2 = jnp.concatenate([vp_ref[:, cs], vc_ref[:, cs]], axis=0)
        qs = jnp.concatenate([jnp.where(low, q2, 0), jnp.where(low, 0, q2)], axis=0)
        bias = bias_ref[2 * pr:2 * pr + 2].reshape(2 * n, 2 * n)
        s = lax.dot_general(qs, k2, nt, preferred_element_type=F32) + bias
        m = jnp.max(jnp.maximum(s[:, :n], s[:, n:]), axis=-1, keepdims=True)
        p = jnp.exp(s - m).astype(BF16)
        w = jnp.concatenate(
            [jnp.concatenate([jnp.where(low, v2, 0), ones_low], axis=1),
             jnp.concatenate([jnp.where(low, 0, v2), ones_high], axis=1)], axis=0)
        r = jnp.dot(jnp.concatenate([p[:n], p[n:]], axis=1), w, preferred_element_type=F32)
        den = r[:, LANES:]
        o_ref[:, cs] = (r[:, :LANES] / den).astype(o_ref.dtype)
        lse = jnp.where(low, m[:n], m[n:]) + jnp.log(den)
        lse_tile = jnp.where((lane == pr) | (lane == HEAD_DIM + pr), lse, lse_tile)
    lse_ref[...] = lse_tile
    kp_ref[...] = kc_ref[...]
    vp_ref[...] = vc_ref[...]


def dilated_attention(qkv, bias, dilation):
    b, r, length, _ = qkv.shape
    n = DIL_STEPS
    hd = DIL_HEADS * HEAD_DIM
    nb = length // n

    def cur(col):
        return pl.BlockSpec((None, None, n, hd), lambda bi, rho, i: (bi, rho, i, col))

    return pl.pallas_call(
        _dil_attn_body,
        grid=(b, r, nb),
        in_specs=[cur(0), cur(1), cur(2),
                  pl.BlockSpec((None, DIL_HEADS, n, 2 * n), lambda bi, rho, i: (jnp.minimum(i, 1), 0, 0, 0))],
        out_specs=[pl.BlockSpec((None, n, hd), lambda bi, rho, i: (bi, i, rho)),
                   pl.BlockSpec((None, n, LANES), lambda bi, rho, i: (bi, i, rho))],
        out_shape=[jax.ShapeDtypeStruct((b, length, r * hd), BF16),
                   jax.ShapeDtypeStruct((b, length, r * LANES), F32)],
        scratch_shapes=[pltpu.VMEM((n, hd), BF16), pltpu.VMEM((n, hd), BF16)],
        compiler_params=_cparams(("arbitrary",) * 3),
        name=f"dilated_attention_r{dilation}",
    )(qkv, qkv, qkv, bias)


def _dil_combine_body(o1_ref, o2_ref, o3_ref, l1_ref, l2_ref, l3_ref, e_ref, w_ref, h_ref, out_ref):
    l1, l2, l3 = l1_ref[...], l2_ref[...], l3_ref[...]
    mx = jnp.maximum(jnp.maximum(l1, l2), l3)
    e1, e2, e3 = jnp.exp(l1 - mx), jnp.exp(l2 - mx), jnp.exp(l3 - mx)
    inv = 1.0 / (e1 + e2 + e3)

    def spread(w):
        hi = w.astype(BF16)
        lo = (w - hi.astype(F32)).astype(BF16)
        return (jnp.dot(hi, e_ref[...], preferred_element_type=F32)
                + jnp.dot(lo, e_ref[...], preferred_element_type=F32))

    o3 = o3_ref[...].astype(F32)
    mix = o3 + spread(e1 * inv) * (o1_ref[...].astype(F32) - o3) + spread(e2 * inv) * (o2_ref[...].astype(F32) - o3)
    out_ref[...] = h_ref[...] + jnp.dot(mix.astype(BF16), w_ref[...], preferred_element_type=F32)


def _head_spread_matrix():
    hd = DIL_HEADS * HEAD_DIM
    lane = jnp.arange(LANES)[:, None]
    col = jnp.arange(hd)[None, :]
    head = col // HEAD_DIM
    src = jnp.where(head % 2 == 0, head // 2, HEAD_DIM + head // 2)
    return (lane == src).astype(BF16)


def dilated_combine(outs, lses, w_o, h, *, tm=512):
    m, d = h.shape
    hd = w_o.shape[0]
    ospec = pl.BlockSpec((tm, hd), lambda i: (i, 0))
    lspec = pl.BlockSpec((tm, LANES), lambda i: (i, 0))
    return pl.pallas_call(
        _dil_combine_body,
        grid=(m // tm,),
        in_specs=[ospec, ospec, ospec, lspec, lspec, lspec,
                  pl.BlockSpec((LANES, hd), lambda i: (0, 0)),
                  pl.BlockSpec(w_o.shape, lambda i: (0, 0)),
                  pl.BlockSpec((tm, d), lambda i: (i, 0))],
        out_specs=pl.BlockSpec((tm, d), lambda i: (i, 0)),
        out_shape=jax.ShapeDtypeStruct((m, d), F32),
        compiler_params=_cparams(("arbitrary",)),
        name="dilated_combine",
    )(*outs, *lses, _head_spread_matrix(), w_o, h)


def _diff_attn_body(qi_ref, ki_ref, q_ref, k_ref, v_ref, bias_ref, lam_ref, g_ref, o_ref,
                    m_ref, acc_ref, *, lambda_init):
    p = pl.program_id(1)
    qi = qi_ref[p]
    ki = ki_ref[p]
    nbatch, t, hw = q_ref.shape
    nt = (((1,), (1,)), ((), ()))
    low = lax.broadcasted_iota(jnp.int32, (1, hw), 1) < HEAD_DIM

    @pl.when(ki == 0)
    def _():
        m_ref[...] = jnp.full(m_ref.shape, MASK_VALUE, F32)
        acc_ref[...] = jnp.zeros_like(acc_ref)

    def batch_step(bi, carry):
        q = q_ref[bi] * jnp.asarray(HEAD_DIM ** -0.5, BF16)
        k = k_ref[bi]
        v1 = jnp.concatenate([v_ref[bi], jnp.ones((t, hw), BF16)], axis=1)
        for mp in range(2):
            qm = jnp.where(low, q, 0) if mp == 0 else jnp.where(low, 0, q)
            s = lax.dot_general(qm, k, nt, preferred_element_type=F32) + bias_ref[mp]
            m_old = m_ref[bi, mp]
            m_new = jnp.maximum(m_old, jnp.max(s, axis=-1, keepdims=True))
            alpha = jnp.exp(m_old - m_new)
            pr = jnp.exp(s - jnp.concatenate([m_new] * (t // hw), axis=1))
            acc_ref[bi, mp] = (jnp.concatenate([alpha, alpha], axis=1) * acc_ref[bi, mp]
                               + jnp.dot(pr.astype(BF16), v1, preferred_element_type=F32))
            m_ref[bi, mp] = m_new
        return carry

    lax.fori_loop(0, nbatch, batch_step, 0, unroll=True)

    @pl.when(ki == qi)
    def _():
        lam = (jnp.exp(jnp.sum(lam_ref[0:1, :] * lam_ref[1:2, :], axis=-1, keepdims=True))
               - jnp.exp(jnp.sum(lam_ref[2:3, :] * lam_ref[3:4, :], axis=-1, keepdims=True)) + lambda_init)
        gain = g_ref[...] * (1.0 - lambda_init)

        def finish(bi, carry):
            a0 = acc_ref[bi, 0]
            a1 = acc_ref[bi, 1]
            o = a0[:, :hw] / a0[:, hw:] - lam * (a1[:, :hw] / a1[:, hw:])
            o_ref[bi] = _rms(o, gain).astype(o_ref.dtype)
            return carry

        lax.fori_loop(0, nbatch, finish, 0)


def diff_attention(qkv, bias, lam_vecs, subln, lambda_init):
    b, s, _ = qkv.shape
    t = DIFF_TILE
    nq = s // t
    n_delta = bias.shape[2]
    pairs = [(qi, ki) for qi in range(nq) for ki in range(qi + 1)]
    qi_arr = jnp.asarray([pq for pq, _ in pairs], jnp.int32)
    ki_arr = jnp.asarray([pk for _, pk in pairs], jnp.int32)
    hw = 2 * HEAD_DIM
    nh = DIFF_HEADS
    grid_spec = pltpu.PrefetchScalarGridSpec(
        num_scalar_prefetch=2,
        grid=(nh, len(pairs)),
        in_specs=[
            pl.BlockSpec((b, t, hw), lambda h, p, qa, ka: (0, qa[p], h)),
            pl.BlockSpec((b, t, hw), lambda h, p, qa, ka: (0, ka[p], nh + h)),
            pl.BlockSpec((b, t, hw), lambda h, p, qa, ka: (0, ka[p], 2 * nh + h)),
            pl.BlockSpec((None, 2, None, t, t),
                         lambda h, p, qa, ka: (h, 0, jnp.minimum(qa[p] - ka[p], n_delta - 1), 0, 0)),
            pl.BlockSpec((4, HEAD_DIM), lambda h, p, qa, ka: (0, 0)),
            pl.BlockSpec((1, hw), lambda h, p, qa, ka: (0, 0)),
        ],
        out_specs=pl.BlockSpec((b, t, hw), lambda h, p, qa, ka: (0, qa[p], h)),
        scratch_shapes=[pltpu.VMEM((b, 2, t, hw), F32), pltpu.VMEM((b, 2, t, 2 * hw), F32)],
    )
    return pl.pallas_call(
        functools.partial(_diff_attn_body, lambda_init=lambda_init),
        grid_spec=grid_spec,
        out_shape=jax.ShapeDtypeStruct((b, s, nh * hw), BF16),
        compiler_params=_cparams(("arbitrary", "arbitrary")),
        name="diff_attention",
    )(qi_arr, ki_arr, qkv, qkv, qkv, bias, lam_vecs, subln.reshape(1, hw))


def _conv_silu(cur_ref, prev_ref, w_ref, b_ref, not_first):
    rows = cur_ref.shape[0]
    prev = prev_ref[...] * not_first
    xc = jnp.concatenate([prev, cur_ref[...]], axis=0)
    acc = b_ref[...]
    for k in range(SSM_CONV):
        off = SUBLANES - (SSM_CONV - 1) + k
        acc = acc + w_ref[k:k + 1, :] * xc[off:off + rows, :]
    return acc * jax.nn.sigmoid(acc)


def _cumsum_rows(a):
    rows = a.shape[0]
    ridx = lax.broadcasted_iota(jnp.int32, a.shape, 0)
    sh = 1
    while sh < rows:
        a = a + jnp.where(ridx >= sh, pltpu.roll(a, sh, 0), 0.0)
        sh *= 2
    return a


def _ssd_body(z_ref, x_ref, bm_ref, cm_ref, xp_ref, bp_ref, cp_ref, dt_ref,
              wx_ref, wb_ref, wc_ref, bx_ref, bb_ref, bc_ref, dtb_ref, alog_ref, dsk_ref, gn_ref,
              y_ref, state_ref):
    c = pl.program_id(1)
    l = SSM_CHUNK
    ns = SSM_D_STATE
    hg = SSM_HEADS // SSM_GROUPS
    gw = hg * SSM_HEAD_DIM

    @pl.when(c == 0)
    def _():
        state_ref[...] = jnp.zeros_like(state_ref)

    not_first = (c > 0).astype(F32)
    xs = _conv_silu(x_ref, xp_ref, wx_ref, bx_ref, not_first)
    bmat = _conv_silu(bm_ref, bp_ref, wb_ref, bb_ref, not_first).astype(BF16)
    cmat = _conv_silu(cm_ref, cp_ref, wc_ref, bc_ref, not_first).astype(BF16)

    dt = jax.nn.softplus(dt_ref[...] + dtb_ref[...])
    a_cs = _cumsum_rows(dt * (-jnp.exp(alog_ref[...])))
    a_cs_t = a_cs.T
    last = a_cs[l - 1:l, :]
    exp_a = jnp.exp(a_cs)
    state_decay = jnp.exp(last - a_cs)
    chunk_decay = jnp.exp(last)

    causal = (lax.broadcasted_iota(jnp.int32, (l, l), 0) >= lax.broadcasted_iota(jnp.int32, (l, l), 1))
    lane_head = lax.broadcasted_iota(jnp.int32, (1, gw), 1) // SSM_HEAD_DIM
    tn_dims = (((0,), (0,)), ((), ()))
    nt_dims = (((1,), (1,)), ((), ()))

    for g in range(SSM_GROUPS):
        def per_head(mat):
            out = mat[:, hg * g + hg - 1:hg * g + hg]
            for j in range(hg - 2, -1, -1):
                out = jnp.where(lane_head == j, mat[:, hg * g + j:hg * g + j + 1], out)
            return out

        cg = cmat[:, g * ns:(g + 1) * ns]
        bg = bmat[:, g * ns:(g + 1) * ns]
        xg = xs[:, g * gw:(g + 1) * gw]
        xdt = xg * per_head(dt)
        xdt_b = xdt.astype(BF16)
        cb = lax.dot_general(cg, bg, nt_dims, preferred_element_type=F32)
        parts = []
        for j in range(hg):
            hh = hg * g + j
            seg = a_cs[:, hh:hh + 1] - a_cs_t[hh:hh + 1, :]
            decay = jnp.where(causal, jnp.exp(seg), 0.0)
            parts.append(jnp.dot((cb * decay).astype(BF16), xdt_b[:, j * SSM_HEAD_DIM:(j + 1) * SSM_HEAD_DIM],
                                 preferred_element_type=F32))
        y_diag = jnp.concatenate(parts, axis=1)
        prev = state_ref[g]
        y_off = jnp.dot(cg, prev.astype(BF16), preferred_element_type=F32) * per_head(exp_a)
        new = lax.dot_general(bg, (xdt * per_head(state_decay)).astype(BF16), tn_dims,
                              preferred_element_type=F32)
        state_ref[g] = prev * per_head(chunk_decay) + new
        y = y_diag + y_off + xg * dsk_ref[:, g * gw:(g + 1) * gw]
        zg = z_ref[:, g * gw:(g + 1) * gw]
        y = y * (zg * jax.nn.sigmoid(zg))
        y_ref[:, g * gw:(g + 1) * gw] = _rms(y, gn_ref[:, g * gw:(g + 1) * gw]).astype(y_ref.dtype)


def ssd_core(zx, dt_raw, conv_w, conv_b, dt_bias, a_log, d_skip, gate_norm):
    b, s, _ = zx.shape
    l = SSM_CHUNK
    di = SSM_D_INNER
    gn = SSM_GROUPS * SSM_D_STATE
    per = l // SUBLANES
    z_spec = pl.BlockSpec((None, l, di), lambda bi, c: (bi, c, 0))
    x_spec = pl.BlockSpec((None, l, di), lambda bi, c: (bi, c, 1))
    b_spec = pl.BlockSpec((None, l, gn), lambda bi, c: (bi, c, 2 * di // gn))
    c_spec = pl.BlockSpec((None, l, gn), lambda bi, c: (bi, c, 2 * di // gn + 1))

    def tail(width, col):
        return pl.BlockSpec((None, SUBLANES, width), lambda bi, c: (bi, jnp.maximum(c * per - 1, 0), col))

    def whole(shape, col=0):
        return pl.BlockSpec(shape, lambda bi, c: (0, col))

    return pl.pallas_call(
        _ssd_body,
        grid=(b, s // l),
        in_specs=[z_spec, x_spec, b_spec, c_spec,
                  tail(di, 1), tail(gn, 2 * di // gn), tail(gn, 2 * di // gn + 1),
                  pl.BlockSpec((None, l, LANES), lambda bi, c: (bi, c, 0)),
                  whole((SSM_CONV, di), 0), whole((SSM_CONV, gn), di // gn), whole((SSM_CONV, gn), di // gn + 1),
                  whole((1, di), 0), whole((1, gn), di // gn), whole((1, gn), di // gn + 1),
                  whole((1, LANES)), whole((1, LANES)), whole((1, di)), whole((1, di))],
        out_specs=pl.BlockSpec((None, l, di), lambda bi, c: (bi, c, 0)),
        out_shape=jax.ShapeDtypeStruct((b, s, di), BF16),
        scratch_shapes=[pltpu.VMEM((SSM_GROUPS, SSM_D_STATE, (SSM_HEADS // SSM_GROUPS) * SSM_HEAD_DIM), F32)],
        compiler_params=_cparams(("arbitrary", "arbitrary")),
        name="ssd_core",
    )(zx, zx, zx, zx, zx, zx, zx, dt_raw,
      conv_w, conv_w, conv_w, conv_b, conv_b, conv_b, dt_bias, a_log, d_skip, gate_norm)


def _dilated_layer(h, norm_g, w_qkv, w_o, dil_biases):
    b, s, d = h.shape
    width = 3 * DIL_HEADS * HEAD_DIM
    outs, lses = [], []
    for g, (_, dilation) in enumerate(DIL_PATTERNS):
        wg = w_qkv[:, g * width:(g + 1) * width].astype(BF16)
        qkv = norm_matmul(h, norm_g, wg, dilation=dilation)
        o, lse = dilated_attention(qkv, dil_biases[g], dilation)
        outs.append(o.reshape(b * s, -1))
        lses.append(lse.reshape(b * s, LANES))
    return dilated_combine(outs, lses, w_o.astype(BF16), h.reshape(b * s, d)).reshape(b, s, d)


def _diff_layer(h, norm_g, w_qkv, lam_q1, lam_k1, lam_q2, lam_k2, subln, w_o, bias, lambda_init):
    b, s, d = h.shape
    qkv = norm_matmul(h, norm_g, w_qkv.astype(BF16)).reshape(b, s, -1)
    lam_vecs = jnp.stack([lam_q1, lam_k1, lam_q2, lam_k2])
    o = diff_attention(qkv, bias, lam_vecs, subln, lambda_init)
    return matmul_residual(o.reshape(b * s, -1), w_o.astype(BF16), h.reshape(b * s, d)).reshape(b, s, d)


def _ssd_layer(h, norm_g, w_in, conv_w, conv_b, dt_bias, a_log, d_skip, gate_norm, w_out):
    b, s, d = h.shape
    wide = SSM_D_INNER + conv_w.shape[1]
    pad = LANES - SSM_HEADS
    zx = norm_matmul(h, norm_g, w_in[:, :wide].astype(BF16), tn=wide // 2, out_dtype=F32).reshape(b, s, wide)
    w_dt = jnp.pad(w_in[:, wide:], ((0, 0), (0, pad))).astype(BF16)
    dt_raw = norm_matmul(h, norm_g, w_dt, out_dtype=F32).reshape(b, s, LANES)
    y = ssd_core(zx, dt_raw, conv_w, conv_b.reshape(1, -1),
                 jnp.pad(dt_bias, (0, pad)).reshape(1, LANES), jnp.pad(a_log, (0, pad)).reshape(1, LANES),
                 jnp.repeat(d_skip, SSM_HEAD_DIM).reshape(1, SSM_D_INNER), gate_norm.reshape(1, SSM_D_INNER))
    return matmul_residual(y.reshape(b * s, -1), w_out.astype(BF16), h.reshape(b * s, d)).reshape(b, s, d)


def _diff_lambda_init(layer):
    return 0.8 - 0.6 * math.exp(-0.3 * layer)


def kernel(x, rel_bias,
           l0_mix_norm, l0_dil_w_qkv, l0_dil_w_o, l0_mlp_norm, l0_mlp_w_up, l0_mlp_w_down,
           l1_mix_norm, l1_diff_w_qkv, l1_diff_lam_q1, l1_diff_lam_k1, l1_diff_lam_q2, l1_diff_lam_k2,
           l1_diff_subln, l1_diff_w_o, l1_mlp_norm, l1_mlp_w_up, l1_mlp_w_down,
           l2_mix_norm, l2_ssm_w_in, l2_ssm_conv_w, l2_ssm_conv_b, l2_ssm_dt_bias, l2_ssm_A_log, l2_ssm_D,
           l2_ssm_gate_norm, l2_ssm_w_out, l2_mlp_norm, l2_mlp_w_up, l2_mlp_w_down,
           l3_mix_norm, l3_dil_w_qkv, l3_dil_w_o, l3_mlp_norm, l3_mlp_w_up, l3_mlp_w_down,
           final_norm):
    b, s, d = x.shape
    dil_biases = [dilated_bias(rel_bias, dilation) for _, dilation in DIL_PATTERNS]
    dif_bias = diff_bias(rel_bias, s)

    def mlp(h, g, w_up, w_down, final_gain=None):
        return mlp_block(h.reshape(b * s, d), g, w_up.astype(BF16), w_down.astype(BF16),
                         final_gain).reshape(b, s, d)

    h = _dilated_layer(x, l0_mix_norm, l0_dil_w_qkv, l0_dil_w_o, dil_biases)
    h = mlp(h, l0_mlp_norm, l0_mlp_w_up, l0_mlp_w_down)
    h = _diff_layer(h, l1_mix_norm, l1_diff_w_qkv, l1_diff_lam_q1, l1_diff_lam_k1, l1_diff_lam_q2,
                    l1_diff_lam_k2, l1_diff_subln, l1_diff_w_o, dif_bias, _diff_lambda_init(1))
    h = mlp(h, l1_mlp_norm, l1_mlp_w_up, l1_mlp_w_down)
    h = _ssd_layer(h, l2_mix_norm, l2_ssm_w_in, l2_ssm_conv_w, l2_ssm_conv_b, l2_ssm_dt_bias, l2_ssm_A_log,
                   l2_ssm_D, l2_ssm_gate_norm, l2_ssm_w_out)
    h = mlp(h, l2_mlp_norm, l2_mlp_w_up, l2_mlp_w_down)
    h = _dilated_layer(h, l3_mix_norm, l3_dil_w_qkv, l3_dil_w_o, dil_biases)
    return mlp(h, l3_mlp_norm, l3_mlp_w_up, l3_mlp_w_down, final_norm)
```

```python
import functools
import math

import jax
import jax.numpy as jnp
from jax import lax
from jax.experimental import pallas as pl
from jax.experimental.pallas import tpu as pltpu

F32 = jnp.float32
BF16 = jnp.bfloat16

NORM_EPS = 1e-5
HEAD_DIM = 64
DIL_PATTERNS = ((128, 1), (512, 4), (2048, 16))
DIL_HEADS = 16
DIL_STEPS = 128
DIFF_HEADS = 8
N_BUCKETS = 32
MAX_DISTANCE = 2048
SSM_HEADS = 32
SSM_GROUPS = 8
SSM_D_STATE = 128
SSM_HEAD_DIM = 64
SSM_D_INNER = SSM_HEADS * SSM_HEAD_DIM
SSM_CONV = 4
SSM_CHUNK = 128
LANES = 128
SUBLANES = 8
MASK_VALUE = -1e30
VMEM_LIMIT = 48 * 1024 * 1024

DIFF_TILE = 512


def _bucket_thresholds():
    exact = N_BUCKETS // 2

    def bucket(d):
        if d < exact:
            return d
        big = exact + int(math.log(d / exact) / math.log(MAX_DISTANCE / exact) * (N_BUCKETS - exact))
        return min(big, N_BUCKETS - 1)

    thr = {}
    d = 0
    for b in range(1, N_BUCKETS):
        while bucket(d) < b:
            d += 1
        thr[b] = d
    return thr, bucket


_BUCKET_THR, _bucket_of = _bucket_thresholds()


def _cparams(sem):
    return pltpu.CompilerParams(dimension_semantics=sem, vmem_limit_bytes=VMEM_LIMIT)


def _rms(x, gain):
    ms = jnp.mean(x * x, axis=-1, keepdims=True)
    return x * lax.rsqrt(ms + NORM_EPS) * gain


def _norm_matmul_body(x_ref, g_ref, w_ref, o_ref, *slab_ref, r):
    rows, d = x_ref.shape
    if r == 1:
        x = x_ref[...]
    else:
        slab = slab_ref[0]
        per = rows // r
        for c in range(d // LANES):
            slab[c] = x_ref[:, c * LANES:(c + 1) * LANES]
        x = jnp.concatenate(
            [jnp.concatenate([slab[c, pl.ds(rho, per, stride=r), :] for c in range(d // LANES)], axis=1)
             for rho in range(r)], axis=0)
    xn = _rms(x, g_ref[...]).astype(BF16)
    res = jnp.dot(xn, w_ref[...], preferred_element_type=F32).astype(o_ref.dtype)
    o_ref[...] = res.reshape(o_ref.shape)


def norm_matmul(x, gain, w, *, dilation=1, rows=512, tn=None, out_dtype=BF16):
    b, s, d = x.shape
    n = w.shape[1]
    r = dilation
    per = rows // r
    tn = n if tn is None else tn
    scratch = [pltpu.VMEM((d // LANES, rows, LANES), F32)] if r > 1 else []
    return pl.pallas_call(
        functools.partial(_norm_matmul_body, r=r),
        grid=(n // tn, b, s // rows),
        in_specs=[
            pl.BlockSpec((None, rows, d), lambda j, bi, i: (bi, i, 0)),
            pl.BlockSpec((1, d), lambda j, bi, i: (0, 0)),
            pl.BlockSpec((d, tn), lambda j, bi, i: (0, j)),
        ],
        out_specs=pl.BlockSpec((None, r, per, tn), lambda j, bi, i: (bi, 0, i, j)),
        out_shape=jax.ShapeDtypeStruct((b, r, s // r, n), out_dtype),
        scratch_shapes=scratch,
        compiler_params=_cparams(("arbitrary",) * 3),
        name="norm_matmul",
    )(x, gain.reshape(1, d), w)


def _matmul_res_body(a_ref, w_ref, r_ref, o_ref):
    o_ref[...] = r_ref[...] + jnp.dot(a_ref[...], w_ref[...], preferred_element_type=F32)


def matmul_residual(a, w, res, *, tm=512):
    m, k = a.shape
    n = w.shape[1]
    return pl.pallas_call(
        _matmul_res_body,
        grid=(m // tm,),
        in_specs=[
            pl.BlockSpec((tm, k), lambda i: (i, 0)),
            pl.BlockSpec((k, n), lambda i: (0, 0)),
            pl.BlockSpec((tm, n), lambda i: (i, 0)),
        ],
        out_specs=pl.BlockSpec((tm, n), lambda i: (i, 0)),
        out_shape=jax.ShapeDtypeStruct((m, n), F32),
        compiler_params=_cparams(("arbitrary",)),
        name="matmul_residual",
    )(a, w, res)


def _mlp_body(h_ref, g_ref, wu_ref, wd_ref, fg_ref, o_ref, xn_ref, acc_ref, *, final_norm):
    f = pl.program_id(1)

    @pl.when(f == 0)
    def _():
        xn_ref[...] = _rms(h_ref[...], g_ref[...]).astype(BF16)
        acc_ref[...] = jnp.zeros_like(acc_ref)

    u = jnp.maximum(jnp.dot(xn_ref[...], wu_ref[...], preferred_element_type=F32), 0.0)
    acc_ref[...] += jnp.dot((u * u).astype(BF16), wd_ref[...], preferred_element_type=F32)

    @pl.when(f == pl.num_programs(1) - 1)
    def _():
        out = h_ref[...] + acc_ref[...]
        if final_norm:
            out = _rms(out, fg_ref[...])
        o_ref[...] = out


def mlp_block(h, gain, w_up, w_down, final_gain=None, *, tm=1024, tf=1024):
    m, d = h.shape
    f = w_up.shape[1]
    final_norm = final_gain is not None
    fg = (final_gain if final_norm else gain).reshape(1, d)
    return pl.pallas_call(
        functools.partial(_mlp_body, final_norm=final_norm),
        grid=(m // tm, f // tf),
        in_specs=[
            pl.BlockSpec((tm, d), lambda i, j: (i, 0)),
            pl.BlockSpec((1, d), lambda i, j: (0, 0)),
            pl.BlockSpec((d, tf), lambda i, j: (0, j)),
            pl.BlockSpec((tf, d), lambda i, j: (j, 0)),
            pl.BlockSpec((1, d), lambda i, j: (0, 0)),
        ],
        out_specs=pl.BlockSpec((tm, d), lambda i, j: (i, 0)),
        out_shape=jax.ShapeDtypeStruct((m, d), F32),
        scratch_shapes=[pltpu.VMEM((tm, d), BF16), pltpu.VMEM((tm, d), F32)],
        compiler_params=_cparams(("arbitrary", "arbitrary")),
        name="mlp_block",
    )(h, gain.reshape(1, d), w_up, w_down, fg)


def _bias_from_dist(dist, col_of_bucket, d_lo, d_hi):
    val = jnp.full(dist.shape, col_of_bucket(_bucket_of(d_lo)), F32)
    for b in range(1, N_BUCKETS):
        t = _BUCKET_THR[b]
        if d_lo < t <= d_hi:
            val = jnp.where(dist >= t, col_of_bucket(b), val)
    return val


def _dil_bias_body(tab_ref, o_ref, *, dilation):
    c = pl.program_id(0)
    n = DIL_STEPS
    qi = lax.broadcasted_iota(jnp.int32, (n, 2 * n), 0)
    kj = lax.broadcasted_iota(jnp.int32, (n, 2 * n), 1)
    steps = n + qi - kj
    band = (steps >= 0) & (steps <= n)
    dist = jnp.clip(steps, 0, n) * dilation
    val = _bias_from_dist(dist, lambda b: tab_ref[b, c], 0, n * dilation)
    later = jnp.where(band, val, MASK_VALUE)
    o_ref[1] = later
    o_ref[0] = jnp.where(kj >= n, later, MASK_VALUE)


def dilated_bias(rel_bias, dilation):
    n = DIL_STEPS
    return pl.pallas_call(
        functools.partial(_dil_bias_body, dilation=dilation),
        grid=(DIL_HEADS,),
        in_specs=[pl.BlockSpec(memory_space=pltpu.SMEM)],
        out_specs=pl.BlockSpec((2, None, n, 2 * n), lambda c: (0, c, 0, 0)),
        out_shape=jax.ShapeDtypeStruct((2, DIL_HEADS, n, 2 * n), F32),
        compiler_params=_cparams(("arbitrary",)),
        name="dilated_bias",
    )(rel_bias)


def _diff_bias_body(tab_ref, o_ref, *, n_delta):
    m = pl.program_id(0)
    h = pl.program_id(1)
    c = m * DIFF_HEADS + h
    t = DIFF_TILE
    qi = lax.broadcasted_iota(jnp.int32, (t, t), 0)
    kj = lax.broadcasted_iota(jnp.int32, (t, t), 1)
    for delta in range(n_delta):
        dist = delta * t + qi - kj
        d_lo = max(delta * t - (t - 1), 0)
        d_hi = delta * t + t - 1
        val = _bias_from_dist(dist, lambda b: tab_ref[b, c], d_lo, d_hi)
        if delta == 0:
            val = jnp.where(dist >= 0, val, MASK_VALUE)
        o_ref[delta] = val


def _diff_n_delta(s):
    t = DIFF_TILE
    last_thr = _BUCKET_THR[N_BUCKETS - 1]
    n_delta = 1
    while n_delta * t - (t - 1) < last_thr:
        n_delta += 1
    return min(n_delta + 1, s // t)


def diff_bias(rel_bias, s):
    t = DIFF_TILE
    n_delta = _diff_n_delta(s)
    return pl.pallas_call(
        functools.partial(_diff_bias_body, n_delta=n_delta),
        grid=(2, DIFF_HEADS),
        in_specs=[pl.BlockSpec(memory_space=pltpu.SMEM)],
        out_specs=pl.BlockSpec((None, None, n_delta, t, t), lambda m, h: (h, m, 0, 0, 0)),
        out_shape=jax.ShapeDtypeStruct((DIFF_HEADS, 2, n_delta, t, t), F32),
        compiler_params=_cparams(("arbitrary", "arbitrary")),
        name="diff_bias",
    )(rel_bias)


def _dil_attn_body(q_ref, kc_ref, vc_ref, bias_ref, o_ref, lse_ref, kp_ref, vp_ref):
    n = DIL_STEPS
    hd = DIL_HEADS * HEAD_DIM
    nt = (((1,), (1,)), ((), ()))
    lane = lax.broadcasted_iota(jnp.int32, (1, LANES), 1)
    low = lane < HEAD_DIM
    ones_low = jnp.where(low, 1, 0).astype(BF16) + jnp.zeros((2 * n, LANES), BF16)
    ones_high = jnp.where(low, 0, 1).astype(BF16) + jnp.zeros((2 * n, LANES), BF16)

    @pl.when(pl.program_id(2) == 0)
    def _():
        kp_ref[...] = jnp.zeros_like(kp_ref)
        vp_ref[...] = jnp.zeros_like(vp_ref)

    lse_tile = jnp.zeros((n, LANES), F32)
    for pr in range(hd // LANES):
        cs = slice(pr * LANES, (pr + 1) * LANES)
        q2 = q_ref[:, cs] * jnp.asarray(HEAD_DIM ** -0.5, BF16)
        k2 = jnp.concatenate([kp_ref[:, cs], kc_ref[:, cs]], axis=0)
        v2 = jnp.concatenate([vp_ref[:, cs], vc_ref[:, cs]], axis=0)
        qs = jnp.concatenate([jnp.where(low, q2, 0), jnp.where(low, 0, q2)], axis=0)
        bias = bias_ref[2 * pr:2 * pr + 2].reshape(2 * n, 2 * n)
        s = lax.dot_general(qs, k2, nt, preferred_element_type=F32) + bias
        m = jnp.max(jnp.maximum(s[:, :n], s[:, n:]), axis=-1, keepdims=True)
        p = jnp.exp(s - m).astype(BF16)
        w = jnp.concatenate(
            [jnp.concatenate([jnp.where(low, v2, 0), ones_low], axis=1),
             jnp.concatenate([jnp.where(low, 0, v2), ones_high], axis=1)], axis=0)
        r = jnp.dot(jnp.concatenate([p[:n], p[n:]], axis=1), w, preferred_element_type=F32)
        den = r[:, LANES:]
        o_ref[:, cs] = (r[:, :LANES] / den).astype(o_ref.dtype)
        lse = jnp.where(low, m[:n], m[n:]) + jnp.log(den)
        lse_tile = jnp.where((lane == pr) | (lane == HEAD_DIM + pr), lse, lse_tile)
    lse_ref[...] = lse_tile
    kp_ref[...] = kc_ref[...]
    vp_ref[...] = vc_ref[...]


def dilated_attention(qkv, bias, dilation):
    b, r, length, _ = qkv.shape
    n = DIL_STEPS
    hd = DIL_HEADS * HEAD_DIM
    nb = length // n

    def cur(col):
        return pl.BlockSpec((None, None, n, hd), lambda bi, rho, i: (bi, rho, i, col))

    return pl.pallas_call(
        _dil_attn_body,
        grid=(b, r, nb),
        in_specs=[cur(0), cur(1), cur(2),
                  pl.BlockSpec((None, DIL_HEADS, n, 2 * n), lambda bi, rho, i: (jnp.minimum(i, 1), 0, 0, 0))],
        out_specs=[pl.BlockSpec((None, None, n, hd), lambda bi, rho, i: (bi, rho, i, 0)),
                   pl.BlockSpec((None, None, n, LANES), lambda bi, rho, i: (bi, rho, i, 0))],
        out_shape=[jax.ShapeDtypeStruct((b, r, length, hd), BF16),
                   jax.ShapeDtypeStruct((b, r, length, LANES), F32)],
        scratch_shapes=[pltpu.VMEM((n, hd), BF16), pltpu.VMEM((n, hd), BF16)],
        compiler_params=_cparams(("arbitrary",) * 3),
        name=f"dilated_attention_r{dilation}",
    )(qkv, qkv, qkv, bias)


def _dil_combine_body(o1_ref, o2_ref, o3_ref, l1_ref, l2_ref, l3_ref, e_ref, w_ref, h_ref, out_ref,
                      oslab_ref, lslab_ref):
    tm = out_ref.shape[0]
    nslab = oslab_ref.shape[1]

    def token_order(o_ref, l_ref, k):
        r = o_ref.shape[0]
        if r == 1:
            return o_ref[0].astype(F32), l_ref[0]
        per = tm // r
        for rho in range(r):
            blk = o_ref[rho].astype(F32)
            for c in range(nslab):
                oslab_ref[k, c, pl.ds(rho, per, stride=r), :] = blk[:, c * LANES:(c + 1) * LANES]
            lslab_ref[k, pl.ds(rho, per, stride=r), :] = l_ref[rho]
        return jnp.concatenate([oslab_ref[k, c] for c in range(nslab)], axis=1), lslab_ref[k]

    o1, l1 = token_order(o1_ref, l1_ref, 0)
    o2, l2 = token_order(o2_ref, l2_ref, 0)
    o3, l3 = token_order(o3_ref, l3_ref, 1)
    mx = jnp.maximum(jnp.maximum(l1, l2), l3)
    e1, e2, e3 = jnp.exp(l1 - mx), jnp.exp(l2 - mx), jnp.exp(l3 - mx)
    inv = 1.0 / (e1 + e2 + e3)

    def spread(w):
        hi = w.astype(BF16)
        lo = (w - hi.astype(F32)).astype(BF16)
        return (jnp.dot(hi, e_ref[...], preferred_element_type=F32)
                + jnp.dot(lo, e_ref[...], preferred_element_type=F32))

    mix = o3 + spread(e1 * inv) * (o1 - o3) + spread(e2 * inv) * (o2 - o3)
    out_ref[...] = h_ref[...] + jnp.dot(mix.astype(BF16), w_ref[...], preferred_element_type=F32)


def _head_spread_matrix():
    hd = DIL_HEADS * HEAD_DIM
    lane = jnp.arange(LANES)[:, None]
    col = jnp.arange(hd)[None, :]
    head = col // HEAD_DIM
    src = jnp.where(head % 2 == 0, head // 2, HEAD_DIM + head // 2)
    return (lane == src).astype(BF16)


def dilated_combine(outs, lses, w_o, h, *, tm=512):
    b, s, d = h.shape
    hd = w_o.shape[0]

    def regrouped(arr):
        r = arr.shape[1]
        return pl.BlockSpec((None, r, tm // r, arr.shape[3]), lambda bi, i: (bi, 0, i, 0))

    return pl.pallas_call(
        _dil_combine_body,
        grid=(b, s // tm),
        in_specs=[regrouped(a) for a in (*outs, *lses)] + [
            pl.BlockSpec((LANES, hd), lambda bi, i: (0, 0)),
            pl.BlockSpec(w_o.shape, lambda bi, i: (0, 0)),
            pl.BlockSpec((None, tm, d), lambda bi, i: (bi, i, 0))],
        out_specs=pl.BlockSpec((None, tm, d), lambda bi, i: (bi, i, 0)),
        out_shape=jax.ShapeDtypeStruct((b, s, d), F32),
        scratch_shapes=[pltpu.VMEM((2, hd // LANES, tm, LANES), F32), pltpu.VMEM((2, tm, LANES), F32)],
        compiler_params=_cparams(("arbitrary", "arbitrary")),
        name="dilated_combine",
    )(*outs, *lses, _head_spread_matrix(), w_o, h)


def _diff_attn_body(qi_ref, ki_ref, q_ref, k_ref, v_ref, bias_ref, lam_ref, g_ref, o_ref,
                    m_ref, acc_ref, *, lambda_init):
    p = pl.program_id(1)
    qi = qi_ref[p]
    ki = ki_ref[p]
    nbatch, t, hw = q_ref.shape
    nt = (((1,), (1,)), ((), ()))
    low = lax.broadcasted_iota(jnp.int32, (1, hw), 1) < HEAD_DIM

    @pl.when(ki == 0)
    def _():
        m_ref[...] = jnp.full(m_ref.shape, MASK_VALUE, F32)
        acc_ref[...] = jnp.zeros_like(acc_ref)

    def batch_step(bi, carry):
        q = q_ref[bi] * jnp.asarray(HEAD_DIM ** -0.5, BF16)
        k = k_ref[bi]
        v1 = jnp.concatenate([v_ref[bi], jnp.ones((t, hw), BF16)], axis=1)
        for mp in range(2):
            qm = jnp.where(low, q, 0) if mp == 0 else jnp.where(low, 0, q)
            s = lax.dot_general(qm, k, nt, preferred_element_type=F32) + bias_ref[mp]
            m_old = m_ref[bi, mp]
            m_new = jnp.maximum(m_old, jnp.max(s, axis=-1, keepdims=True))
            alpha = jnp.exp(m_old - m_new)
            pr = jnp.exp(s - jnp.concatenate([m_new] * (t // hw), axis=1))
            acc_ref[bi, mp] = (jnp.concatenate([alpha, alpha], axis=1) * acc_ref[bi, mp]
                               + jnp.dot(pr.astype(BF16), v1, preferred_element_type=F32))
            m_ref[bi, mp] = m_new
        return carry

    lax.fori_loop(0, nbatch, batch_step, 0, unroll=True)

    @pl.when(ki == qi)
    def _():
        lam = (jnp.exp(jnp.sum(lam_ref[0:1, :] * lam_ref[1:2, :], axis=-1, keepdims=True))
               - jnp.exp(jnp.sum(lam_ref[2:3, :] * lam_ref[3:4, :], axis=-1, keepdims=True)) + lambda_init)
        gain = g_ref[...] * (1.0 - lambda_init)

        def finish(bi, carry):
            a0 = acc_ref[bi, 0]
            a1 = acc_ref[bi, 1]
            o = a0[:, :hw] / a0[:, hw:] - lam * (a1[:, :hw] / a1[:, hw:])
            o_ref[bi] = _rms(o, gain).astype(o_ref.dtype)
            return carry

        lax.fori_loop(0, nbatch, finish, 0)


def diff_attention(qkv, bias, lam_vecs, subln, lambda_init):
    b, s, _ = qkv.shape
    t = DIFF_TILE
    nq = s // t
    n_delta = bias.shape[2]
    pairs = [(qi, ki) for qi in range(nq) for ki in range(qi + 1)]
    qi_arr = jnp.asarray([pq for pq, _ in pairs], jnp.int32)
    ki_arr = jnp.asarray([pk for _, pk in pairs], jnp.int32)
    hw = 2 * HEAD_DIM
    nh = DIFF_HEADS
    grid_spec = pltpu.PrefetchScalarGridSpec(
        num_scalar_prefetch=2,
        grid=(nh, len(pairs)),
        in_specs=[
            pl.BlockSpec((b, t, hw), lambda h, p, qa, ka: (0, qa[p], h)),
            pl.BlockSpec((b, t, hw), lambda h, p, qa, ka: (0, ka[p], nh + h)),
            pl.BlockSpec((b, t, hw), lambda h, p, qa, ka: (0, ka[p], 2 * nh + h)),
            pl.BlockSpec((None, 2, None, t, t),
                         lambda h, p, qa, ka: (h, 0, jnp.minimum(qa[p] - ka[p], n_delta - 1), 0, 0)),
            pl.BlockSpec((4, HEAD_DIM), lambda h, p, qa, ka: (0, 0)),
            pl.BlockSpec((1, hw), lambda h, p, qa, ka: (0, 0)),
        ],
        out_specs=pl.BlockSpec((b, t, hw), lambda h, p, qa, ka: (0, qa[p], h)),
        scratch_shapes=[pltpu.VMEM((b, 2, t, hw), F32), pltpu.VMEM((b, 2, t, 2 * hw), F32)],
    )
    return pl.pallas_call(
        functools.partial(_diff_attn_body, lambda_init=lambda_init),
        grid_spec=grid_spec,
        out_shape=jax.ShapeDtypeStruct((b, s, nh * hw), BF16),
        compiler_params=_cparams(("arbitrary", "arbitrary")),
        name="diff_attention",
    )(qi_arr, ki_arr, qkv, qkv, qkv, bias, lam_vecs, subln.reshape(1, hw))


def _conv_silu(cur_ref, prev_ref, w_ref, b_ref, not_first):
    rows = cur_ref.shape[0]
    prev = prev_ref[...] * not_first
    xc = jnp.concatenate([prev, cur_ref[...]], axis=0)
    acc = b_ref[...]
    for k in range(SSM_CONV):
        off = SUBLANES - (SSM_CONV - 1) + k
        acc = acc + w_ref[k:k + 1, :] * xc[off:off + rows, :]
    return acc * jax.nn.sigmoid(acc)


def _cumsum_rows(a):
    rows = a.shape[0]
    ridx = lax.broadcasted_iota(jnp.int32, a.shape, 0)
    sh = 1
    while sh < rows:
        a = a + jnp.where(ridx >= sh, pltpu.roll(a, sh, 0), 0.0)
        sh *= 2
    return a


def _ssd_body(z_ref, x_ref, bm_ref, cm_ref, xp_ref, bp_ref, cp_ref, dt_ref,
              wx_ref, wb_ref, wc_ref, bx_ref, bb_ref, bc_ref, dtb_ref, alog_ref, dsk_ref, gn_ref,
              y_ref, state_ref):
    c = pl.program_id(1)
    l = SSM_CHUNK
    ns = SSM_D_STATE
    hg = SSM_HEADS // SSM_GROUPS
    gw = hg * SSM_HEAD_DIM

    @pl.when(c == 0)
    def _():
        state_ref[...] = jnp.zeros_like(state_ref)

    not_first = (c > 0).astype(F32)
    xs = _conv_silu(x_ref, xp_ref, wx_ref, bx_ref, not_first)
    bmat = _conv_silu(bm_ref, bp_ref, wb_ref, bb_ref, not_first).astype(BF16)
    cmat = _conv_silu(cm_ref, cp_ref, wc_ref, bc_ref, not_first).astype(BF16)

    dt = jax.nn.softplus(dt_ref[...] + dtb_ref[...])
    a_cs = _cumsum_rows(dt * (-jnp.exp(alog_ref[...])))
    a_cs_t = a_cs.T
    last = a_cs[l - 1:l, :]
    exp_a = jnp.exp(a_cs)
    state_decay = jnp.exp(last - a_cs)
    chunk_decay = jnp.exp(last)

    causal = (lax.broadcasted_iota(jnp.int32, (l, l), 0) >= lax.broadcasted_iota(jnp.int32, (l, l), 1))
    lane_head = lax.broadcasted_iota(jnp.int32, (1, gw), 1) // SSM_HEAD_DIM
    tn_dims = (((0,), (0,)), ((), ()))
    nt_dims = (((1,), (1,)), ((), ()))

    for g in range(SSM_GROUPS):
        def per_head(mat):
            out = mat[:, hg * g + hg - 1:hg * g + hg]
            for j in range(hg - 2, -1, -1):
                out = jnp.where(lane_head == j, mat[:, hg * g + j:hg * g + j + 1], out)
            return out

        cg = cmat[:, g * ns:(g + 1) * ns]
        bg = bmat[:, g * ns:(g + 1) * ns]
        xg = xs[:, g * gw:(g + 1) * gw]
        xdt = xg * per_head(dt)
        xdt_b = xdt.astype(BF16)
        cb = lax.dot_general(cg, bg, nt_dims, preferred_element_type=F32)
        parts = []
        for j in range(hg):
            hh = hg * g + j
            seg = a_cs[:, hh:hh + 1] - a_cs_t[hh:hh + 1, :]
            decay = jnp.where(causal, jnp.exp(seg), 0.0)
            parts.append(jnp.dot((cb * decay).astype(BF16), xdt_b[:, j * SSM_HEAD_DIM:(j + 1) * SSM_HEAD_DIM],
                                 preferred_element_type=F32))
        y_diag = jnp.concatenate(parts, axis=1)
        prev = state_ref[g]
        y_off = jnp.dot(cg, prev.astype(BF16), preferred_element_type=F32) * per_head(exp_a)
        new = lax.dot_general(bg, (xdt * per_head(state_decay)).astype(BF16), tn_dims,
                              preferred_element_type=F32)
        state_ref[g] = prev * per_head(chunk_decay) + new
        y = y_diag + y_off + xg * dsk_ref[:, g * gw:(g + 1) * gw]
        zg = z_ref[:, g * gw:(g + 1) * gw]
        y = y * (zg * jax.nn.sigmoid(zg))
        y_ref[:, g * gw:(g + 1) * gw] = _rms(y, gn_ref[:, g * gw:(g + 1) * gw]).astype(y_ref.dtype)


def ssd_core(zx, dt_raw, conv_w, conv_b, dt_bias, a_log, d_skip, gate_norm):
    b, s, _ = zx.shape
    l = SSM_CHUNK
    di = SSM_D_INNER
    gn = SSM_GROUPS * SSM_D_STATE
    per = l // SUBLANES
    z_spec = pl.BlockSpec((None, l, di), lambda bi, c: (bi, c, 0))
    x_spec = pl.BlockSpec((None, l, di), lambda bi, c: (bi, c, 1))
    b_spec = pl.BlockSpec((None, l, gn), lambda bi, c: (bi, c, 2 * di // gn))
    c_spec = pl.BlockSpec((None, l, gn), lambda bi, c: (bi, c, 2 * di // gn + 1))

    def tail(width, col):
        return pl.BlockSpec((None, SUBLANES, width), lambda bi, c: (bi, jnp.maximum(c * per - 1, 0), col))

    def whole(shape, col=0):
        return pl.BlockSpec(shape, lambda bi, c: (0, col))

    return pl.pallas_call(
        _ssd_body,
        grid=(b, s // l),
        in_specs=[z_spec, x_spec, b_spec, c_spec,
                  tail(di, 1), tail(gn, 2 * di // gn), tail(gn, 2 * di // gn + 1),
                  pl.BlockSpec((None, l, LANES), lambda bi, c: (bi, c, 0)),
                  whole((SSM_CONV, di), 0), whole((SSM_CONV, gn), di // gn), whole((SSM_CONV, gn), di // gn + 1),
                  whole((1, di), 0), whole((1, gn), di // gn), whole((1, gn), di // gn + 1),
                  whole((1, LANES)), whole((1, LANES)), whole((1, di)), whole((1, di))],
        out_specs=pl.BlockSpec((None, l, di), lambda bi, c: (bi, c, 0)),
        out_shape=jax.ShapeDtypeStruct((b, s, di), BF16),
        scratch_shapes=[pltpu.VMEM((SSM_GROUPS, SSM_D_STATE, (SSM_HEADS // SSM_GROUPS) * SSM_HEAD_DIM), F32)],
        compiler_params=_cparams(("arbitrary", "arbitrary")),
        name="ssd_core",
    )(zx, zx, zx, zx, zx, zx, zx, dt_raw,
      conv_w, conv_w, conv_w, conv_b, conv_b, conv_b, dt_bias, a_log, d_skip, gate_norm)


def _dilated_layer(h, norm_g, w_qkv, w_o, dil_biases):
    b, s, d = h.shape
    width = 3 * DIL_HEADS * HEAD_DIM
    outs, lses = [], []
    for g, (_, dilation) in enumerate(DIL_PATTERNS):
        wg = w_qkv[:, g * width:(g + 1) * width].astype(BF16)
        qkv = norm_matmul(h, norm_g, wg, dilation=dilation)
        o, lse = dilated_attention(qkv, dil_biases[g], dilation)
        outs.append(o)
        lses.append(lse)
    return dilated_combine(outs, lses, w_o.astype(BF16), h)


def _diff_layer(h, norm_g, w_qkv, lam_q1, lam_k1, lam_q2, lam_k2, subln, w_o, bias, lambda_init):
    b, s, d = h.shape
    qkv = norm_matmul(h, norm_g, w_qkv.astype(BF16)).reshape(b, s, -1)
    lam_vecs = jnp.stack([lam_q1, lam_k1, lam_q2, lam_k2])
    o = diff_attention(qkv, bias, lam_vecs, subln, lambda_init)
    return matmul_residual(o.reshape(b * s, -1), w_o.astype(BF16), h.reshape(b * s, d)).reshape(b, s, d)


def _ssd_layer(h, norm_g, w_in, conv_w, conv_b, dt_bias, a_log, d_skip, gate_norm, w_out):
    b, s, d = h.shape
    wide = SSM_D_INNER + conv_w.shape[1]
    pad = LANES - SSM_HEADS
    zx = norm_matmul(h, norm_g, w_in[:, :wide].astype(BF16), tn=wide // 2, out_dtype=F32).reshape(b, s, wide)
    w_dt = jnp.pad(w_in[:, wide:], ((0, 0), (0, pad))).astype(BF16)
    dt_raw = norm_matmul(h, norm_g, w_dt, out_dtype=F32).reshape(b, s, LANES)
    y = ssd_core(zx, dt_raw, conv_w, conv_b.reshape(1, -1),
                 jnp.pad(dt_bias, (0, pad)).reshape(1, LANES), jnp.pad(a_log, (0, pad)).reshape(1, LANES),
                 jnp.repeat(d_skip, SSM_HEAD_DIM).reshape(1, SSM_D_INNER), gate_norm.reshape(1, SSM_D_INNER))
    return matmul_residual(y.reshape(b * s, -1), w_out.astype(BF16), h.reshape(b * s, d)).reshape(b, s, d)


def _diff_lambda_init(layer):
    return 0.8 - 0.6 * math.exp(-0.3 * layer)


def kernel(x, rel_bias,
           l0_mix_norm, l0_dil_w_qkv, l0_dil_w_o, l0_mlp_norm, l0_mlp_w_up, l0_mlp_w_down,
           l1_mix_norm, l1_diff_w_qkv, l1_diff_lam_q1, l1_diff_lam_k1, l1_diff_lam_q2, l1_diff_lam_k2,
           l1_diff_subln, l1_diff_w_o, l1_mlp_norm, l1_mlp_w_up, l1_mlp_w_down,
           l2_mix_norm, l2_ssm_w_in, l2_ssm_conv_w, l2_ssm_conv_b, l2_ssm_dt_bias, l2_ssm_A_log, l2_ssm_D,
           l2_ssm_gate_norm, l2_ssm_w_out, l2_mlp_norm, l2_mlp_w_up, l2_mlp_w_down,
           l3_mix_norm, l3_dil_w_qkv, l3_dil_w_o, l3_mlp_norm, l3_mlp_w_up, l3_mlp_w_down,
           final_norm):
    b, s, d = x.shape
    dil_biases = [dilated_bias(rel_bias, dilation) for _, dilation in DIL_PATTERNS]
    dif_bias = diff_bias(rel_bias, s)

    def mlp(h, g, w_up, w_down, final_gain=None):
        return mlp_block(h.reshape(b * s, d), g, w_up.astype(BF16), w_down.astype(BF16),
                         final_gain).reshape(b, s, d)

    h = _dilated_layer(x, l0_mix_norm, l0_dil_w_qkv, l0_dil_w_o, dil_biases)
    h = mlp(h, l0_mlp_norm, l0_mlp_w_up, l0_mlp_w_down)
    h = _diff_layer(h, l1_mix_norm, l1_diff_w_qkv, l1_diff_lam_q1, l1_diff_lam_k1, l1_diff_lam_q2,
                    l1_diff_lam_k2, l1_diff_subln, l1_diff_w_o, dif_bias, _diff_lambda_init(1))
    h = mlp(h, l1_mlp_norm, l1_mlp_w_up, l1_mlp_w_down)
    h = _ssd_layer(h, l2_mix_norm, l2_ssm_w_in, l2_ssm_conv_w, l2_ssm_conv_b, l2_ssm_dt_bias, l2_ssm_A_log,
                   l2_ssm_D, l2_ssm_gate_norm, l2_ssm_w_out)
    h = mlp(h, l2_mlp_norm, l2_mlp_w_up, l2_mlp_w_down)
    h = _dilated_layer(h, l3_mix_norm, l3_dil_w_qkv, l3_dil_w_o, dil_biases)
    return mlp(h, l3_mlp_norm, l3_mlp_w_up, l3_mlp_w_down, final_norm)
```

```python
import functools
import math

import jax
import jax.numpy as jnp
from jax import lax
from jax.experimental import pallas as pl
from jax.experimental.pallas import tpu as pltpu

F32 = jnp.float32
BF16 = jnp.bfloat16

NORM_EPS = 1e-5
HEAD_DIM = 64
DIL_PATTERNS = ((128, 1), (512, 4), (2048, 16))
DIL_HEADS = 16
DIL_STEPS = 128
DIFF_HEADS = 8
N_BUCKETS = 32
MAX_DISTANCE = 2048
SSM_HEADS = 32
SSM_GROUPS = 8
SSM_D_STATE = 128
SSM_HEAD_DIM = 64
SSM_D_INNER = SSM_HEADS * SSM_HEAD_DIM
SSM_CONV = 4
SSM_CHUNK = 128
LANES = 128
SUBLANES = 8
MASK_VALUE = -1e30
VMEM_LIMIT = 48 * 1024 * 1024

DIFF_TILE = 512


def _bucket_thresholds():
    exact = N_BUCKETS // 2

    def bucket(d):
        if d < exact:
            return d
        big = exact + int(math.log(d / exact) / math.log(MAX_DISTANCE / exact) * (N_BUCKETS - exact))
        return min(big, N_BUCKETS - 1)

    thr = {}
    d = 0
    for b in range(1, N_BUCKETS):
        while bucket(d) < b:
            d += 1
        thr[b] = d
    return thr, bucket


_BUCKET_THR, _bucket_of = _bucket_thresholds()


def _cparams(sem):
    return pltpu.CompilerParams(dimension_semantics=sem, vmem_limit_bytes=VMEM_LIMIT)


def _rms(x, gain):
    ms = jnp.mean(x * x, axis=-1, keepdims=True)
    return x * lax.rsqrt(ms + NORM_EPS) * gain


def _norm_matmul_body(x_ref, g_ref, w_ref, o_ref, *slab_ref, r):
    rows, d = x_ref.shape
    if r == 1:
        x = x_ref[...]
    else:
        slab = slab_ref[0]
        per = rows // r
        for c in range(d // LANES):
            slab[c] = x_ref[:, c * LANES:(c + 1) * LANES]
        x = jnp.concatenate(
            [jnp.concatenate([slab[c, pl.ds(rho, per, stride=r), :] for c in range(d // LANES)], axis=1)
             for rho in range(r)], axis=0)
    xn = _rms(x, g_ref[...]).astype(BF16)
    res = jnp.dot(xn, w_ref[...], preferred_element_type=F32).astype(o_ref.dtype)
    o_ref[...] = res.reshape(o_ref.shape)


def norm_matmul(x, gain, w, *, dilation=1, rows=512, tn=None, out_dtype=BF16):
    b, s, d = x.shape
    n = w.shape[1]
    r = dilation
    per = rows // r
    tn = n if tn is None else tn
    scratch = [pltpu.VMEM((d // LANES, rows, LANES), F32)] if r > 1 else []
    return pl.pallas_call(
        functools.partial(_norm_matmul_body, r=r),
        grid=(n // tn, b, s // rows),
        in_specs=[
            pl.BlockSpec((None, rows, d), lambda j, bi, i: (bi, i, 0)),
            pl.BlockSpec((1, d), lambda j, bi, i: (0, 0)),
            pl.BlockSpec((d, tn), lambda j, bi, i: (0, j)),
        ],
        out_specs=pl.BlockSpec((None, r, per, tn), lambda j, bi, i: (bi, 0, i, j)),
        out_shape=jax.ShapeDtypeStruct((b, r, s // r, n), out_dtype),
        scratch_shapes=scratch,
        compiler_params=_cparams(("arbitrary",) * 3),
        name="norm_matmul",
    )(x, gain.reshape(1, d), w)


def _matmul_res_body(a_ref, w_ref, r_ref, o_ref):
    o_ref[...] = r_ref[...] + jnp.dot(a_ref[...], w_ref[...], preferred_element_type=F32)


def matmul_residual(a, w, res, *, tm=512):
    m, k = a.shape
    n = w.shape[1]
    return pl.pallas_call(
        _matmul_res_body,
        grid=(m // tm,),
        in_specs=[
            pl.BlockSpec((tm, k), lambda i: (i, 0)),
            pl.BlockSpec((k, n), lambda i: (0, 0)),
            pl.BlockSpec((tm, n), lambda i: (i, 0)),
        ],
        out_specs=pl.BlockSpec((tm, n), lambda i: (i, 0)),
        out_shape=jax.ShapeDtypeStruct((m, n), F32),
        compiler_params=_cparams(("arbitrary",)),
        name="matmul_residual",
    )(a, w, res)


def _mlp_body(h_ref, g_ref, wu_ref, wd_ref, fg_ref, o_ref, xn_ref, acc_ref, *, final_norm):
    f = pl.program_id(1)

    @pl.when(f == 0)
    def _():
        xn_ref[...] = _rms(h_ref[...], g_ref[...]).astype(BF16)
        acc_ref[...] = jnp.zeros_like(acc_ref)

    u = jnp.maximum(jnp.dot(xn_ref[...], wu_ref[...], preferred_element_type=F32), 0.0)
    acc_ref[...] += jnp.dot((u * u).astype(BF16), wd_ref[...], preferred_element_type=F32)

    @pl.when(f == pl.num_programs(1) - 1)
    def _():
        out = h_ref[...] + acc_ref[...]
        if final_norm:
            out = _rms(out, fg_ref[...])
        o_ref[...] = out


def mlp_block(h, gain, w_up, w_down, final_gain=None, *, tm=1024, tf=1024):
    m, d = h.shape
    f = w_up.shape[1]
    final_norm = final_gain is not None
    fg = (final_gain if final_norm else gain).reshape(1, d)
    return pl.pallas_call(
        functools.partial(_mlp_body, final_norm=final_norm),
        grid=(m // tm, f // tf),
        in_specs=[
            pl.BlockSpec((tm, d), lambda i, j: (i, 0)),
            pl.BlockSpec((1, d), lambda i, j: (0, 0)),
            pl.BlockSpec((d, tf), lambda i, j: (0, j)),
            pl.BlockSpec((tf, d), lambda i, j: (j, 0)),
            pl.BlockSpec((1, d), lambda i, j: (0, 0)),
        ],
        out_specs=pl.BlockSpec((tm, d), lambda i, j: (i, 0)),
        out_shape=jax.ShapeDtypeStruct((m, d), F32),
        scratch_shapes=[pltpu.VMEM((tm, d), BF16), pltpu.VMEM((tm, d), F32)],
        compiler_params=_cparams(("arbitrary", "arbitrary")),
        name="mlp_block",
    )(h, gain.reshape(1, d), w_up, w_down, fg)


def _bias_from_dist(dist, col_of_bucket, d_lo, d_hi):
    val = jnp.full(dist.shape, col_of_bucket(_bucket_of(d_lo)), F32)
    for b in range(1, N_BUCKETS):
        t = _BUCKET_THR[b]
        if d_lo < t <= d_hi:
            val = jnp.where(dist >= t, col_of_bucket(b), val)
    return val


def _dil_bias_body(tab_ref, o_ref, *, dilation):
    c = pl.program_id(0)
    n = DIL_STEPS
    qi = lax.broadcasted_iota(jnp.int32, (n, 2 * n), 0)
    kj = lax.broadcasted_iota(jnp.int32, (n, 2 * n), 1)
    steps = n + qi - kj
    band = (steps >= 0) & (steps <= n)
    dist = jnp.clip(steps, 0, n) * dilation
    val = _bias_from_dist(dist, lambda b: tab_ref[b, c], 0, n * dilation)
    later = jnp.where(band, val, MASK_VALUE)
    o_ref[1] = later
    o_ref[0] = jnp.where(kj >= n, later, MASK_VALUE)


def dilated_bias(rel_bias, dilation):
    n = DIL_STEPS
    return pl.pallas_call(
        functools.partial(_dil_bias_body, dilation=dilation),
        grid=(DIL_HEADS,),
        in_specs=[pl.BlockSpec(memory_space=pltpu.SMEM)],
        out_specs=pl.BlockSpec((2, None, n, 2 * n), lambda c: (0, c, 0, 0)),
        out_shape=jax.ShapeDtypeStruct((2, DIL_HEADS, n, 2 * n), F32),
        compiler_params=_cparams(("arbitrary",)),
        name="dilated_bias",
    )(rel_bias)


def _diff_bias_body(tab_ref, o_ref, *, n_delta):
    m = pl.program_id(0)
    h = pl.program_id(1)
    c = m * DIFF_HEADS + h
    t = DIFF_TILE
    qi = lax.broadcasted_iota(jnp.int32, (t, t), 0)
    kj = lax.broadcasted_iota(jnp.int32, (t, t), 1)
    for delta in range(n_delta):
        dist = delta * t + qi - kj
        d_lo = max(delta * t - (t - 1), 0)
        d_hi = delta * t + t - 1
        val = _bias_from_dist(dist, lambda b: tab_ref[b, c], d_lo, d_hi)
        if delta == 0:
            val = jnp.where(dist >= 0, val, MASK_VALUE)
        o_ref[delta] = val


def _diff_n_delta(s):
    t = DIFF_TILE
    last_thr = _BUCKET_THR[N_BUCKETS - 1]
    n_delta = 1
    while n_delta * t - (t - 1) < last_thr:
        n_delta += 1
    return min(n_delta + 1, s // t)


def diff_bias(rel_bias, s):
    t = DIFF_TILE
    n_delta = _diff_n_delta(s)
    return pl.pallas_call(
        functools.partial(_diff_bias_body, n_delta=n_delta),
        grid=(2, DIFF_HEADS),
        in_specs=[pl.BlockSpec(memory_space=pltpu.SMEM)],
        out_specs=pl.BlockSpec((None, None, n_delta, t, t), lambda m, h: (h, m, 0, 0, 0)),
        out_shape=jax.ShapeDtypeStruct((DIFF_HEADS, 2, n_delta, t, t), F32),
        compiler_params=_cparams(("arbitrary", "arbitrary")),
        name="diff_bias",
    )(rel_bias)


def _dil_attn_body(q_ref, kc_ref, vc_ref, bias_ref, o_ref, lse_ref, kp_ref, vp_ref):
    n = DIL_STEPS
    hd = DIL_HEADS * HEAD_DIM
    nt = (((1,), (1,)), ((), ()))
    lane = lax.broadcasted_iota(jnp.int32, (1, LANES), 1)
    low = lane < HEAD_DIM
    ones_low = jnp.where(low, 1, 0).astype(BF16) + jnp.zeros((2 * n, LANES), BF16)
    ones_high = jnp.where(low, 0, 1).astype(BF16) + jnp.zeros((2 * n, LANES), BF16)

    @pl.when(pl.program_id(2) == 0)
    def _():
        kp_ref[...] = jnp.zeros_like(kp_ref)
        vp_ref[...] = jnp.zeros_like(vp_ref)

    lse_tile = jnp.zeros((n, LANES), F32)
    for pr in range(hd // LANES):
        cs = slice(pr * LANES, (pr + 1) * LANES)
        q2 = q_ref[:, cs] * jnp.asarray(HEAD_DIM ** -0.5, BF16)
        k2 = jnp.concatenate([kp_ref[:, cs], kc_ref[:, cs]], axis=0)
        v2 = jnp.concatenate([vp_ref[:, cs], vc_ref[:, cs]], axis=0)
        qs = jnp.concatenate([jnp.where(low, q2, 0), jnp.where(low, 0, q2)], axis=0)
        bias = bias_ref[2 * pr:2 * pr + 2].reshape(2 * n, 2 * n)
        s = lax.dot_general(qs, k2, nt, preferred_element_type=F32) + bias
        m = jnp.max(jnp.maximum(s[:, :n], s[:, n:]), axis=-1, keepdims=True)
        p = jnp.exp(s - m).astype(BF16)
        w = jnp.concatenate(
            [jnp.concatenate([jnp.where(low, v2, 0), ones_low], axis=1),
             jnp.concatenate([jnp.where(low, 0, v2), ones_high], axis=1)], axis=0)
        r = jnp.dot(jnp.concatenate([p[:n], p[n:]], axis=1), w, preferred_element_type=F32)
        den = r[:, LANES:]
        o_ref[:, cs] = (r[:, :LANES] / den).astype(o_ref.dtype)
        lse = jnp.where(low, m[:n], m[n:]) + jnp.log(den)
        lse_tile = jnp.where((lane == pr) | (lane == HEAD_DIM + pr), lse, lse_tile)
    lse_ref[...] = lse_tile
    kp_ref[...] = kc_ref[...]
    vp_ref[...] = vc_ref[...]


def dilated_attention(qkv, bias, dilation):
    b, r, length, _ = qkv.shape
    n = DIL_STEPS
    hd = DIL_HEADS * HEAD_DIM
    nb = length // n

    def cur(col):
        return pl.BlockSpec((None, None, n, hd), lambda bi, rho, i: (bi, rho, i, col))

    return pl.pallas_call(
        _dil_attn_body,
        grid=(b, r, nb),
        in_specs=[cur(0), cur(1), cur(2),
                  pl.BlockSpec((None, DIL_HEADS, n, 2 * n), lambda bi, rho, i: (jnp.minimum(i, 1), 0, 0, 0))],
        out_specs=[pl.BlockSpec((None, None, n, hd), lambda bi, rho, i: (bi, rho, i, 0)),
                   pl.BlockSpec((None, None, n, LANES), lambda bi, rho, i: (bi, rho, i, 0))],
        out_shape=[jax.ShapeDtypeStruct((b, r, length, hd), BF16),
                   jax.ShapeDtypeStruct((b, r, length, LANES), F32)],
        scratch_shapes=[pltpu.VMEM((n, hd), BF16), pltpu.VMEM((n, hd), BF16)],
        compiler_params=_cparams(("arbitrary",) * 3),
        name=f"dilated_attention_r{dilation}",
    )(qkv, qkv, qkv, bias)


def _dil_combine_body(o1_ref, o2_ref, o3_ref, l1_ref, l2_ref, l3_ref, e_ref, w_ref, h_ref, out_ref,
                      oslab_ref, lslab_ref):
    tm = out_ref.shape[0]
    nslab = oslab_ref.shape[1]

    def token_order(o_ref, l_ref, k):
        r = o_ref.shape[0]
        if r == 1:
            return o_ref[0].astype(F32), l_ref[0]
        per = tm // r
        for rho in range(r):
            blk = o_ref[rho].astype(F32)
            for c in range(nslab):
                oslab_ref[k, c, pl.ds(rho, per, stride=r), :] = blk[:, c * LANES:(c + 1) * LANES]
            lslab_ref[k, pl.ds(rho, per, stride=r), :] = l_ref[rho]
        return jnp.concatenate([oslab_ref[k, c] for c in range(nslab)], axis=1), lslab_ref[k]

    o1, l1 = token_order(o1_ref, l1_ref, 0)
    o2, l2 = token_order(o2_ref, l2_ref, 0)
    o3, l3 = token_order(o3_ref, l3_ref, 1)
    mx = jnp.maximum(jnp.maximum(l1, l2), l3)
    e1, e2, e3 = jnp.exp(l1 - mx), jnp.exp(l2 - mx), jnp.exp(l3 - mx)
    inv = 1.0 / (e1 + e2 + e3)

    def spread(w):
        hi = w.astype(BF16)
        lo = (w - hi.astype(F32)).astype(BF16)
        return (jnp.dot(hi, e_ref[...], preferred_element_type=F32)
                + jnp.dot(lo, e_ref[...], preferred_element_type=F32))

    mix = o3 + spread(e1 * inv) * (o1 - o3) + spread(e2 * inv) * (o2 - o3)
    out_ref[...] = h_ref[...] + jnp.dot(mix.astype(BF16), w_ref[...], preferred_element_type=F32)


def _head_spread_matrix():
    hd = DIL_HEADS * HEAD_DIM
    lane = jnp.arange(LANES)[:, None]
    col = jnp.arange(hd)[None, :]
    head = col // HEAD_DIM
    src = jnp.where(head % 2 == 0, head // 2, HEAD_DIM + head // 2)
    return (lane == src).astype(BF16)


def dilated_combine(outs, lses, w_o, h, *, tm=512):
    b, s, d = h.shape
    hd = w_o.shape[0]

    def regrouped(arr):
        r = arr.shape[1]
        return pl.BlockSpec((None, r, tm // r, arr.shape[3]), lambda bi, i: (bi, 0, i, 0))

    return pl.pallas_call(
        _dil_combine_body,
        grid=(b, s // tm),
        in_specs=[regrouped(a) for a in (*outs, *lses)] + [
            pl.BlockSpec((LANES, hd), lambda bi, i: (0, 0)),
            pl.BlockSpec(w_o.shape, lambda bi, i: (0, 0)),
            pl.BlockSpec((None, tm, d), lambda bi, i: (bi, i, 0))],
        out_specs=pl.BlockSpec((None, tm, d), lambda bi, i: (bi, i, 0)),
        out_shape=jax.ShapeDtypeStruct((b, s, d), F32),
        scratch_shapes=[pltpu.VMEM((2, hd // LANES, tm, LANES), F32), pltpu.VMEM((2, tm, LANES), F32)],
        compiler_params=_cparams(("arbitrary", "arbitrary")),
        name="dilated_combine",
    )(*outs, *lses, _head_spread_matrix(), w_o, h)


def _diff_attn_body(qi_ref, ki_ref, q_ref, k_ref, v_ref, bias_ref, lam_ref, g_ref, o_ref,
                    m_ref, acc_ref, *, lambda_init):
    p = pl.program_id(1)
    qi = qi_ref[p]
    ki = ki_ref[p]
    nbatch, t, hw = q_ref.shape
    nt = (((1,), (1,)), ((), ()))
    low = lax.broadcasted_iota(jnp.int32, (1, hw), 1) < HEAD_DIM

    @pl.when(ki == 0)
    def _():
        m_ref[...] = jnp.full(m_ref.shape, MASK_VALUE, F32)
        acc_ref[...] = jnp.zeros_like(acc_ref)

    def batch_step(bi, carry):
        q = q_ref[bi] * jnp.asarray(HEAD_DIM ** -0.5, BF16)
        k = k_ref[bi]
        v1 = jnp.concatenate([v_ref[bi], jnp.ones((t, hw), BF16)], axis=1)
        for mp in range(2):
            qm = jnp.where(low, q, 0) if mp == 0 else jnp.where(low, 0, q)
            s = lax.dot_general(qm, k, nt, preferred_element_type=F32) + bias_ref[mp]
            m_old = m_ref[bi, mp]
            m_new = jnp.maximum(m_old, jnp.max(s, axis=-1, keepdims=True))
            alpha = jnp.exp(m_old - m_new)
            pr = jnp.exp(s - jnp.concatenate([m_new] * (t // hw), axis=1))
            acc_ref[bi, mp] = (jnp.concatenate([alpha, alpha], axis=1) * acc_ref[bi, mp]
                               + jnp.dot(pr.astype(BF16), v1, preferred_element_type=F32))
            m_ref[bi, mp] = m_new
        return carry

    lax.fori_loop(0, nbatch, batch_step, 0, unroll=True)

    @pl.when(ki == qi)
    def _():
        lam = (jnp.exp(jnp.sum(lam_ref[0:1, :] * lam_ref[1:2, :], axis=-1, keepdims=True))
               - jnp.exp(jnp.sum(lam_ref[2:3, :] * lam_ref[3:4, :], axis=-1, keepdims=True)) + lambda_init)
        gain = g_ref[...] * (1.0 - lambda_init)

        def finish(bi, carry):
            a0 = acc_ref[bi, 0]
            a1 = acc_ref[bi, 1]
            o = a0[:, :hw] / a0[:, hw:] - lam * (a1[:, :hw] / a1[:, hw:])
            o_ref[bi] = _rms(o, gain).astype(o_ref.dtype)
            return carry

        lax.fori_loop(0, nbatch, finish, 0)


def diff_attention(qkv, bias, lam_vecs, subln, lambda_init):
    b, s, _ = qkv.shape
    t = DIFF_TILE
    nq = s // t
    n_delta = bias.shape[2]
    pairs = [(qi, ki) for qi in range(nq) for ki in range(qi + 1)]
    qi_arr = jnp.asarray([pq for pq, _ in pairs], jnp.int32)
    ki_arr = jnp.asarray([pk for _, pk in pairs], jnp.int32)
    hw = 2 * HEAD_DIM
    nh = DIFF_HEADS
    grid_spec = pltpu.PrefetchScalarGridSpec(
        num_scalar_prefetch=2,
        grid=(nh, len(pairs)),
        in_specs=[
            pl.BlockSpec((b, t, hw), lambda h, p, qa, ka: (0, qa[p], h)),
            pl.BlockSpec((b, t, hw), lambda h, p, qa, ka: (0, ka[p], nh + h)),
            pl.BlockSpec((b, t, hw), lambda h, p, qa, ka: (0, ka[p], 2 * nh + h)),
            pl.BlockSpec((None, 2, None, t, t),
                         lambda h, p, qa, ka: (h, 0, jnp.minimum(qa[p] - ka[p], n_delta - 1), 0, 0)),
            pl.BlockSpec((4, HEAD_DIM), lambda h, p, qa, ka: (0, 0)),
            pl.BlockSpec((1, hw), lambda h, p, qa, ka: (0, 0)),
        ],
        out_specs=pl.BlockSpec((b, t, hw), lambda h, p, qa, ka: (0, qa[p], h)),
        scratch_shapes=[pltpu.VMEM((b, 2, t, hw), F32), pltpu.VMEM((b, 2, t, 2 * hw), F32)],
    )
    return pl.pallas_call(
        functools.partial(_diff_attn_body, lambda_init=lambda_init),
        grid_spec=grid_spec,
        out_shape=jax.ShapeDtypeStruct((b, s, nh * hw), BF16),
        compiler_params=_cparams(("arbitrary", "arbitrary")),
        name="diff_attention",
    )(qi_arr, ki_arr, qkv, qkv, qkv, bias, lam_vecs, subln.reshape(1, hw))


def _silu(x):
    h = 0.5 * x
    return h + h * jnp.tanh(h)


def _conv_silu(win_ref, w_ref, b_ref, cols):
    rows = win_ref.shape[0] - SUBLANES
    acc = b_ref[:, cols]
    for k in range(SSM_CONV):
        off = SUBLANES - (SSM_CONV - 1) + k
        acc = acc + w_ref[k:k + 1, cols] * win_ref[pl.ds(off, rows), cols]
    return _silu(acc)


def _spread(w, e_ref):
    hi = w.astype(BF16)
    lo = (w - hi.astype(F32)).astype(BF16)
    return (jnp.dot(hi, e_ref[...], preferred_element_type=F32)
            + jnp.dot(lo, e_ref[...], preferred_element_type=F32))


def _cumsum_rows(a):
    rows = a.shape[0]
    ridx = lax.broadcasted_iota(jnp.int32, a.shape, 0)
    sh = 1
    while sh < rows:
        a = a + jnp.where(ridx >= sh, pltpu.roll(a, sh, 0), 0.0)
        sh *= 2
    return a


def _ssd_body(z_ref, x_ref, bm_ref, cm_ref, dt_ref, cw_ref, cb_ref, dtb_ref, alog_ref, dsk_ref, gn_ref, e_ref,
              y_ref, state_ref, win_ref):
    c = pl.program_id(1)
    l = SSM_CHUNK
    ns = SSM_D_STATE
    di = SSM_D_INNER
    gn = SSM_GROUPS * ns
    hg = SSM_HEADS // SSM_GROUPS
    gw = hg * SSM_HEAD_DIM

    @pl.when(c == 0)
    def _():
        state_ref[...] = jnp.zeros_like(state_ref)
        win_ref[0:SUBLANES, :] = jnp.zeros((SUBLANES, win_ref.shape[1]), F32)

    win_ref[SUBLANES:, 0:di] = x_ref[...]
    win_ref[SUBLANES:, di:di + gn] = bm_ref[...]
    win_ref[SUBLANES:, di + gn:] = cm_ref[...]
    xs = [_conv_silu(win_ref, cw_ref, cb_ref, slice(g * gw, (g + 1) * gw)) for g in range(SSM_GROUPS)]
    bmat = _conv_silu(win_ref, cw_ref, cb_ref, slice(di, di + gn)).astype(BF16)
    cmat = _conv_silu(win_ref, cw_ref, cb_ref, slice(di + gn, di + 2 * gn)).astype(BF16)
    win_ref[0:SUBLANES, :] = win_ref[l:l + SUBLANES, :]

    dt = jax.nn.softplus(dt_ref[...] + dtb_ref[...])
    a_cs = _cumsum_rows(dt * (-jnp.exp(alog_ref[...])))
    a_cs_t = a_cs.T
    last = a_cs[l - 1:l, :]
    dt_e = _spread(dt, e_ref)
    exp_a_e = _spread(jnp.exp(a_cs), e_ref)
    state_decay_e = _spread(jnp.exp(last - a_cs), e_ref)

    causal = (lax.broadcasted_iota(jnp.int32, (l, l), 0) >= lax.broadcasted_iota(jnp.int32, (l, l), 1))
    lane_head = lax.broadcasted_iota(jnp.int32, (1, gw), 1) // SSM_HEAD_DIM
    tn_dims = (((0,), (0,)), ((), ()))
    nt_dims = (((1,), (1,)), ((), ()))

    for g in range(SSM_GROUPS):
        gs = slice(g * gw, (g + 1) * gw)
        cg = cmat[:, g * ns:(g + 1) * ns]
        bg = bmat[:, g * ns:(g + 1) * ns]
        xg = xs[g]
        xdt = xg * dt_e[:, gs]
        xdt_b = xdt.astype(BF16)
        cb = lax.dot_general(cg, bg, nt_dims, preferred_element_type=F32)
        y_diag = None
        for j in range(hg):
            hh = hg * g + j
            seg = a_cs[:, hh:hh + 1] - a_cs_t[hh:hh + 1, :]
            decay = jnp.where(causal, jnp.exp(seg), 0.0)
            part = jnp.dot((cb * decay).astype(BF16), jnp.where(lane_head == j, xdt_b, 0),
                           preferred_element_type=F32)
            y_diag = part if y_diag is None else y_diag + part
        prev = state_ref[g]
        y_off = jnp.dot(cg, prev.astype(BF16), preferred_element_type=F32) * exp_a_e[:, gs]
        new = lax.dot_general(bg, (xdt * state_decay_e[:, gs]).astype(BF16), tn_dims,
                              preferred_element_type=F32)
        state_ref[g] = prev * exp_a_e[l - 1:l, gs] + new
        y = y_diag + y_off + xg * dsk_ref[:, gs]
        zg = z_ref[:, gs]
        y = y * _silu(zg)
        y_ref[:, gs] = _rms(y, gn_ref[:, gs]).astype(y_ref.dtype)


def ssd_core(zx, dt_raw, conv_w, conv_b, dt_bias, a_log, d_skip, gate_norm):
    b, s, _ = zx.shape
    l = SSM_CHUNK
    di = SSM_D_INNER
    gn = SSM_GROUPS * SSM_D_STATE
    conv_dim = di + 2 * gn
    z_spec = pl.BlockSpec((None, l, di), lambda bi, c: (bi, c, 0))
    x_spec = pl.BlockSpec((None, l, di), lambda bi, c: (bi, c, 1))
    b_spec = pl.BlockSpec((None, l, gn), lambda bi, c: (bi, c, 2 * di // gn))
    c_spec = pl.BlockSpec((None, l, gn), lambda bi, c: (bi, c, 2 * di // gn + 1))

    def whole(shape):
        return pl.BlockSpec(shape, lambda bi, c: (0, 0))

    head_lanes = (jnp.arange(LANES)[:, None] == jnp.arange(di)[None, :] // SSM_HEAD_DIM).astype(BF16)
    return pl.pallas_call(
        _ssd_body,
        grid=(b, s // l),
        in_specs=[z_spec, x_spec, b_spec, c_spec,
                  pl.BlockSpec((None, l, LANES), lambda bi, c: (bi, c, 0)),
                  whole((SSM_CONV, conv_dim)), whole((1, conv_dim)),
                  whole((1, LANES)), whole((1, LANES)), whole((1, di)), whole((1, di)), whole((LANES, di))],
        out_specs=pl.BlockSpec((None, l, di), lambda bi, c: (bi, c, 0)),
        out_shape=jax.ShapeDtypeStruct((b, s, di), BF16),
        scratch_shapes=[pltpu.VMEM((SSM_GROUPS, SSM_D_STATE, (SSM_HEADS // SSM_GROUPS) * SSM_HEAD_DIM), F32),
                        pltpu.VMEM((SUBLANES + l, conv_dim), F32)],
        compiler_params=_cparams(("arbitrary", "arbitrary")),
        name="ssd_core",
    )(zx, zx, zx, zx, dt_raw, conv_w, conv_b, dt_bias, a_log, d_skip, gate_norm, head_lanes)


def _dilated_layer(h, norm_g, w_qkv, w_o, dil_biases):
    b, s, d = h.shape
    width = 3 * DIL_HEADS * HEAD_DIM
    outs, lses = [], []
    for g, (_, dilation) in enumerate(DIL_PATTERNS):
        wg = w_qkv[:, g * width:(g + 1) * width].astype(BF16)
        qkv = norm_matmul(h, norm_g, wg, dilation=dilation)
        o, lse = dilated_attention(qkv, dil_biases[g], dilation)
        outs.append(o)
        lses.append(lse)
    return dilated_combine(outs, lses, w_o.astype(BF16), h)


def _diff_layer(h, norm_g, w_qkv, lam_q1, lam_k1, lam_q2, lam_k2, subln, w_o, bias, lambda_init):
    b, s, d = h.shape
    qkv = norm_matmul(h, norm_g, w_qkv.astype(BF16)).reshape(b, s, -1)
    lam_vecs = jnp.stack([lam_q1, lam_k1, lam_q2, lam_k2])
    o = diff_attention(qkv, bias, lam_vecs, subln, lambda_init)
    return matmul_residual(o.reshape(b * s, -1), w_o.astype(BF16), h.reshape(b * s, d)).reshape(b, s, d)


def _ssd_layer(h, norm_g, w_in, conv_w, conv_b, dt_bias, a_log, d_skip, gate_norm, w_out):
    b, s, d = h.shape
    wide = SSM_D_INNER + conv_w.shape[1]
    pad = LANES - SSM_HEADS
    zx = norm_matmul(h, norm_g, w_in[:, :wide].astype(BF16), tn=wide // 2, out_dtype=F32).reshape(b, s, wide)
    w_dt = jnp.pad(w_in[:, wide:], ((0, 0), (0, pad))).astype(BF16)
    dt_raw = norm_matmul(h, norm_g, w_dt, out_dtype=F32).reshape(b, s, LANES)
    y = ssd_core(zx, dt_raw, conv_w, conv_b.reshape(1, -1),
                 jnp.pad(dt_bias, (0, pad)).reshape(1, LANES), jnp.pad(a_log, (0, pad)).reshape(1, LANES),
                 jnp.repeat(d_skip, SSM_HEAD_DIM).reshape(1, SSM_D_INNER), gate_norm.reshape(1, SSM_D_INNER))
    return matmul_residual(y.reshape(b * s, -1), w_out.astype(BF16), h.reshape(b * s, d)).reshape(b, s, d)


def _diff_lambda_init(layer):
    return 0.8 - 0.6 * math.exp(-0.3 * layer)


def kernel(x, rel_bias,
           l0_mix_norm, l0_dil_w_qkv, l0_dil_w_o, l0_mlp_norm, l0_mlp_w_up, l0_mlp_w_down,
           l1_mix_norm, l1_diff_w_qkv, l1_diff_lam_q1, l1_diff_lam_k1, l1_diff_lam_q2, l1_diff_lam_k2,
           l1_diff_subln, l1_diff_w_o, l1_mlp_norm, l1_mlp_w_up, l1_mlp_w_down,
           l2_mix_norm, l2_ssm_w_in, l2_ssm_conv_w, l2_ssm_conv_b, l2_ssm_dt_bias, l2_ssm_A_log, l2_ssm_D,
           l2_ssm_gate_norm, l2_ssm_w_out, l2_mlp_norm, l2_mlp_w_up, l2_mlp_w_down,
           l3_mix_norm, l3_dil_w_qkv, l3_dil_w_o, l3_mlp_norm, l3_mlp_w_up, l3_mlp_w_down,
           final_norm):
    b, s, d = x.shape
    dil_biases = [dilated_bias(rel_bias, dilation) for _, dilation in DIL_PATTERNS]
    dif_bias = diff_bias(rel_bias, s)

    def mlp(h, g, w_up, w_down, final_gain=None):
        return mlp_block(h.reshape(b * s, d), g, w_up.astype(BF16), w_down.astype(BF16),
                         final_gain).reshape(b, s, d)

    h = _dilated_layer(x, l0_mix_norm, l0_dil_w_qkv, l0_dil_w_o, dil_biases)
    h = mlp(h, l0_mlp_norm, l0_mlp_w_up, l0_mlp_w_down)
    h = _diff_layer(h, l1_mix_norm, l1_diff_w_qkv, l1_diff_lam_q1, l1_diff_lam_k1, l1_diff_lam_q2,
                    l1_diff_lam_k2, l1_diff_subln, l1_diff_w_o, dif_bias, _diff_lambda_init(1))
    h = mlp(h, l1_mlp_norm, l1_mlp_w_up, l1_mlp_w_down)
    h = _ssd_layer(h, l2_mix_norm, l2_ssm_w_in, l2_ssm_conv_w, l2_ssm_conv_b, l2_ssm_dt_bias, l2_ssm_A_log,
                   l2_ssm_D, l2_ssm_gate_norm, l2_ssm_w_out)
    h = mlp(h, l2_mlp_norm, l2_mlp_w_up, l2_mlp_w_down)
    h = _dilated_layer(h, l3_mix_norm, l3_dil_w_qkv, l3_dil_w_o, dil_biases)
    return mlp(h, l3_mlp_norm, l3_mlp_w_up, l3_mlp_w_down, final_norm)
```

```python
import functools
import math

import jax
import jax.numpy as jnp
from jax import lax
from jax.experimental import pallas as pl
from jax.experimental.pallas import tpu as pltpu

F32 = jnp.float32
BF16 = jnp.bfloat16

NORM_EPS = 1e-5
HEAD_DIM = 64
DIL_PATTERNS = ((128, 1), (512, 4), (2048, 16))
DIL_HEADS = 16
DIL_STEPS = 128
DIFF_HEADS = 8
N_BUCKETS = 32
MAX_DISTANCE = 2048
SSM_HEADS = 32
SSM_GROUPS = 8
SSM_D_STATE = 128
SSM_HEAD_DIM = 64
SSM_D_INNER = SSM_HEADS * SSM_HEAD_DIM
SSM_CONV = 4
SSM_CHUNK = 128
LANES = 128
SUBLANES = 8
MASK_VALUE = -1e30
VMEM_LIMIT = 56 * 1024 * 1024

DIFF_TILE = 512


def _bucket_thresholds():
    exact = N_BUCKETS // 2

    def bucket(d):
        if d < exact:
            return d
        big = exact + int(math.log(d / exact) / math.log(MAX_DISTANCE / exact) * (N_BUCKETS - exact))
        return min(big, N_BUCKETS - 1)

    thr = {}
    d = 0
    for b in range(1, N_BUCKETS):
        while bucket(d) < b:
            d += 1
        thr[b] = d
    return thr, bucket


_BUCKET_THR, _bucket_of = _bucket_thresholds()


def _cparams(sem):
    return pltpu.CompilerParams(dimension_semantics=sem, vmem_limit_bytes=VMEM_LIMIT)


def _rms(x, gain):
    ms = jnp.mean(x * x, axis=-1, keepdims=True)
    return x * lax.rsqrt(ms + NORM_EPS) * gain


def _norm_matmul_body(x_ref, g_ref, w_ref, o_ref, *slab_ref, r):
    rows, d = x_ref.shape
    if r == 1:
        x = x_ref[...]
    else:
        slab = slab_ref[0]
        per = rows // r
        for c in range(d // LANES):
            slab[c] = x_ref[:, c * LANES:(c + 1) * LANES]
        x = jnp.concatenate(
            [jnp.concatenate([slab[c, pl.ds(rho, per, stride=r), :] for c in range(d // LANES)], axis=1)
             for rho in range(r)], axis=0)
    xn = _rms(x, g_ref[...]).astype(BF16)
    res = jnp.dot(xn, w_ref[...].astype(BF16), preferred_element_type=F32).astype(o_ref.dtype)
    o_ref[...] = res.reshape(o_ref.shape)


def norm_matmul(x, gain, w, *, n, tn, col_block=0, dilation=1, rows=512, out_dtype=BF16):
    b, s, d = x.shape
    r = dilation
    per = rows // r
    scratch = [pltpu.VMEM((d // LANES, rows, LANES), F32)] if r > 1 else []
    return pl.pallas_call(
        functools.partial(_norm_matmul_body, r=r),
        grid=(n // tn, b, s // rows),
        in_specs=[
            pl.BlockSpec((None, rows, d), lambda j, bi, i: (bi, i, 0)),
            pl.BlockSpec((1, d), lambda j, bi, i: (0, 0)),
            pl.BlockSpec((d, tn), lambda j, bi, i: (0, col_block + j), pipeline_mode=pl.Buffered(1)),
        ],
        out_specs=pl.BlockSpec((None, r, per, tn), lambda j, bi, i: (bi, 0, i, j)),
        out_shape=jax.ShapeDtypeStruct((b, r, s // r, n), out_dtype),
        scratch_shapes=scratch,
        compiler_params=_cparams(("arbitrary",) * 3),
        name="norm_matmul",
    )(x, gain.reshape(1, d), w)


def _matmul_res_body(a_ref, w_ref, r_ref, o_ref):
    o_ref[...] = r_ref[...] + jnp.dot(a_ref[...], w_ref[...].astype(BF16), preferred_element_type=F32)


def matmul_residual(a, w, res, *, tm=512):
    m, k = a.shape
    n = w.shape[1]
    return pl.pallas_call(
        _matmul_res_body,
        grid=(m // tm,),
        in_specs=[
            pl.BlockSpec((tm, k), lambda i: (i, 0)),
            pl.BlockSpec((k, n), lambda i: (0, 0)),
            pl.BlockSpec((tm, n), lambda i: (i, 0)),
        ],
        out_specs=pl.BlockSpec((tm, n), lambda i: (i, 0)),
        out_shape=jax.ShapeDtypeStruct((m, n), F32),
        compiler_params=_cparams(("arbitrary",)),
        name="matmul_residual",
    )(a, w, res)


def _mlp_body(h_ref, g_ref, wu_ref, wd_ref, fg_ref, o_ref, *, final_norm):
    f = pl.program_id(1)
    last = pl.num_programs(1) - 1

    def contribution():
        xn = _rms(h_ref[...], g_ref[...]).astype(BF16)
        u = jnp.maximum(jnp.dot(xn, wu_ref[...].astype(BF16), preferred_element_type=F32), 0.0)
        return jnp.dot((u * u).astype(BF16), wd_ref[...].astype(BF16), preferred_element_type=F32)

    @pl.when(f == 0)
    def _():
        o_ref[...] = h_ref[...] + contribution()

    @pl.when((f > 0) & (f < last))
    def _():
        o_ref[...] += contribution()

    @pl.when(f == last)
    def _():
        out = o_ref[...] + contribution()
        if final_norm:
            out = _rms(out, fg_ref[...])
        o_ref[...] = out


def mlp_block(h, gain, w_up, w_down, final_gain=None, *, tm=1024, tf=1024):
    m, d = h.shape
    f = w_up.shape[1]
    assert f // tf >= 2
    final_norm = final_gain is not None
    fg = (final_gain if final_norm else gain).reshape(1, d)
    return pl.pallas_call(
        functools.partial(_mlp_body, final_norm=final_norm),
        grid=(m // tm, f // tf),
        in_specs=[
            pl.BlockSpec((tm, d), lambda i, j: (i, 0)),
            pl.BlockSpec((1, d), lambda i, j: (0, 0)),
            pl.BlockSpec((d, tf), lambda i, j: (0, j)),
            pl.BlockSpec((tf, d), lambda i, j: (j, 0)),
            pl.BlockSpec((1, d), lambda i, j: (0, 0)),
        ],
        out_specs=pl.BlockSpec((tm, d), lambda i, j: (i, 0)),
        out_shape=jax.ShapeDtypeStruct((m, d), F32),
        compiler_params=_cparams(("arbitrary", "arbitrary")),
        name="mlp_block",
    )(h, gain.reshape(1, d), w_up, w_down, fg)


def _bias_from_dist(dist, col_of_bucket, d_lo, d_hi):
    val = jnp.full(dist.shape, col_of_bucket(_bucket_of(d_lo)), F32)
    for b in range(1, N_BUCKETS):
        t = _BUCKET_THR[b]
        if d_lo < t <= d_hi:
            val = jnp.where(dist >= t, col_of_bucket(b), val)
    return val


def _dil_bias_body(tab_ref, o_ref, *, dilation):
    c = pl.program_id(0)
    n = DIL_STEPS
    qi = lax.broadcasted_iota(jnp.int32, (n, 2 * n), 0)
    kj = lax.broadcasted_iota(jnp.int32, (n, 2 * n), 1)
    steps = n + qi - kj
    band = (steps >= 0) & (steps <= n)
    dist = jnp.clip(steps, 0, n) * dilation
    val = _bias_from_dist(dist, lambda b: tab_ref[b, c], 0, n * dilation)
    later = jnp.where(band, val, MASK_VALUE)
    o_ref[1] = later
    o_ref[0] = jnp.where(kj >= n, later, MASK_VALUE)


def dilated_bias(rel_bias, dilation):
    n = DIL_STEPS
    return pl.pallas_call(
        functools.partial(_dil_bias_body, dilation=dilation),
        grid=(DIL_HEADS,),
        in_specs=[pl.BlockSpec(memory_space=pltpu.SMEM)],
        out_specs=pl.BlockSpec((2, None, n, 2 * n), lambda c: (0, c, 0, 0)),
        out_shape=jax.ShapeDtypeStruct((2, DIL_HEADS, n, 2 * n), F32),
        compiler_params=_cparams(("arbitrary",)),
        name="dilated_bias",
    )(rel_bias)


def _diff_bias_body(tab_ref, o_ref, *, n_delta):
    m = pl.program_id(0)
    h = pl.program_id(1)
    c = m * DIFF_HEADS + h
    t = DIFF_TILE
    qi = lax.broadcasted_iota(jnp.int32, (t, t), 0)
    kj = lax.broadcasted_iota(jnp.int32, (t, t), 1)
    for delta in range(n_delta):
        dist = delta * t + qi - kj
        d_lo = max(delta * t - (t - 1), 0)
        d_hi = delta * t + t - 1
        val = _bias_from_dist(dist, lambda b: tab_ref[b, c], d_lo, d_hi)
        if delta == 0:
            val = jnp.where(dist >= 0, val, MASK_VALUE)
        o_ref[delta] = val


def _diff_first_constant_delta():
    t = DIFF_TILE
    last_thr = _BUCKET_THR[N_BUCKETS - 1]
    delta = 1
    while delta * t - (t - 1) < last_thr:
        delta += 1
    return delta


def _diff_n_delta(s):
    return min(_diff_first_constant_delta() + 1, s // DIFF_TILE)


def diff_bias(rel_bias, s):
    t = DIFF_TILE
    n_delta = _diff_n_delta(s)
    return pl.pallas_call(
        functools.partial(_diff_bias_body, n_delta=n_delta),
        grid=(2, DIFF_HEADS),
        in_specs=[pl.BlockSpec(memory_space=pltpu.SMEM)],
        out_specs=pl.BlockSpec((None, None, n_delta, t, t), lambda m, h: (h, m, 0, 0, 0)),
        out_shape=jax.ShapeDtypeStruct((DIFF_HEADS, 2, n_delta, t, t), F32),
        compiler_params=_cparams(("arbitrary", "arbitrary")),
        name="diff_bias",
    )(rel_bias)


def _dil_attn_body(q_ref, kc_ref, vc_ref, bias_ref, o_ref, lse_ref, kp_ref, vp_ref):
    n = DIL_STEPS
    hd = DIL_HEADS * HEAD_DIM
    nt = (((1,), (1,)), ((), ()))
    lane = lax.broadcasted_iota(jnp.int32, (1, LANES), 1)
    low = lane < HEAD_DIM
    ones_low = jnp.where(low, 1, 0).astype(BF16) + jnp.zeros((2 * n, LANES), BF16)
    ones_high = jnp.where(low, 0, 1).astype(BF16) + jnp.zeros((2 * n, LANES), BF16)

    @pl.when(pl.program_id(2) == 0)
    def _():
        kp_ref[...] = jnp.zeros_like(kp_ref)
        vp_ref[...] = jnp.zeros_like(vp_ref)

    lse_tile = jnp.zeros((n, LANES), F32)
    for pr in range(hd // LANES):
        cs = slice(pr * LANES, (pr + 1) * LANES)
        q2 = q_ref[:, cs] * jnp.asarray(HEAD_DIM ** -0.5, BF16)
        k2 = jnp.concatenate([kp_ref[:, cs], kc_ref[:, cs]], axis=0)
        v2 = jnp.concatenate([vp_ref[:, cs], vc_ref[:, cs]], axis=0)
        qs = jnp.concatenate([jnp.where(low, q2, 0), jnp.where(low, 0, q2)], axis=0)
        bias = bias_ref[2 * pr:2 * pr + 2].reshape(2 * n, 2 * n)
        s = lax.dot_general(qs, k2, nt, preferred_element_type=F32) + bias
        m = jnp.max(jnp.maximum(s[:, :n], s[:, n:]), axis=-1, keepdims=True)
        p = jnp.exp(s - m).astype(BF16)
        w = jnp.concatenate(
            [jnp.concatenate([jnp.where(low, v2, 0), ones_low], axis=1),
             jnp.concatenate([jnp.where(low, 0, v2), ones_high], axis=1)], axis=0)
        r = jnp.dot(jnp.concatenate([p[:n], p[n:]], axis=1), w, preferred_element_type=F32)
        den = r[:, LANES:]
        o_ref[:, cs] = (r[:, :LANES] / den).astype(o_ref.dtype)
        lse = jnp.where(low, m[:n], m[n:]) + jnp.log(den)
        lse_tile = jnp.where((lane == pr) | (lane == HEAD_DIM + pr), lse, lse_tile)
    lse_ref[...] = lse_tile
    kp_ref[...] = kc_ref[...]
    vp_ref[...] = vc_ref[...]


def dilated_attention(qkv, bias, dilation):
    b, r, length, _ = qkv.shape
    n = DIL_STEPS
    hd = DIL_HEADS * HEAD_DIM
    nb = length // n

    def cur(col):
        return pl.BlockSpec((None, None, n, hd), lambda bi, rho, i: (bi, rho, i, col))

    return pl.pallas_call(
        _dil_attn_body,
        grid=(b, r, nb),
        in_specs=[cur(0), cur(1), cur(2),
                  pl.BlockSpec((None, DIL_HEADS, n, 2 * n), lambda bi, rho, i: (jnp.minimum(i, 1), 0, 0, 0))],
        out_specs=[pl.BlockSpec((None, None, n, hd), lambda bi, rho, i: (bi, rho, i, 0)),
                   pl.BlockSpec((None, None, n, LANES), lambda bi, rho, i: (bi, rho, i, 0))],
        out_shape=[jax.ShapeDtypeStruct((b, r, length, hd), BF16),
                   jax.ShapeDtypeStruct((b, r, length, LANES), F32)],
        scratch_shapes=[pltpu.VMEM((n, hd), BF16), pltpu.VMEM((n, hd), BF16)],
        compiler_params=_cparams(("arbitrary",) * 3),
        name=f"dilated_attention_r{dilation}",
    )(qkv, qkv, qkv, bias)


def _dil_combine_body(o1_ref, o2_ref, o3_ref, l1_ref, l2_ref, l3_ref, e_ref, w_ref, h_ref, out_ref,
                      oslab_ref, lslab_ref):
    tm = out_ref.shape[0]
    nslab = oslab_ref.shape[1]

    def token_order(o_ref, l_ref, k):
        r = o_ref.shape[0]
        if r == 1:
            return o_ref[0].astype(F32), l_ref[0]
        per = tm // r
        for rho in range(r):
            blk = o_ref[rho].astype(F32)
            for c in range(nslab):
                oslab_ref[k, c, pl.ds(rho, per, stride=r), :] = blk[:, c * LANES:(c + 1) * LANES]
            lslab_ref[k, pl.ds(rho, per, stride=r), :] = l_ref[rho]
        return jnp.concatenate([oslab_ref[k, c] for c in range(nslab)], axis=1), lslab_ref[k]

    o1, l1 = token_order(o1_ref, l1_ref, 0)
    o2, l2 = token_order(o2_ref, l2_ref, 0)
    o3, l3 = token_order(o3_ref, l3_ref, 1)
    mx = jnp.maximum(jnp.maximum(l1, l2), l3)
    e1, e2, e3 = jnp.exp(l1 - mx), jnp.exp(l2 - mx), jnp.exp(l3 - mx)
    inv = 1.0 / (e1 + e2 + e3)

    def spread(w):
        hi = w.astype(BF16)
        lo = (w - hi.astype(F32)).astype(BF16)
        return (jnp.dot(hi, e_ref[...], preferred_element_type=F32)
                + jnp.dot(lo, e_ref[...], preferred_element_type=F32))

    mix = o3 + spread(e1 * inv) * (o1 - o3) + spread(e2 * inv) * (o2 - o3)
    out_ref[...] = h_ref[...] + jnp.dot(mix.astype(BF16), w_ref[...].astype(BF16), preferred_element_type=F32)


def _head_spread_matrix():
    hd = DIL_HEADS * HEAD_DIM
    lane = jnp.arange(LANES)[:, None]
    col = jnp.arange(hd)[None, :]
    head = col // HEAD_DIM
    src = jnp.where(head % 2 == 0, head // 2, HEAD_DIM + head // 2)
    return (lane == src).astype(BF16)


def dilated_combine(outs, lses, w_o, h, *, tm=512):
    b, s, d = h.shape
    hd = w_o.shape[0]

    def regrouped(arr):
        r = arr.shape[1]
        return pl.BlockSpec((None, r, tm // r, arr.shape[3]), lambda bi, i: (bi, 0, i, 0))

    return pl.pallas_call(
        _dil_combine_body,
        grid=(b, s // tm),
        in_specs=[regrouped(a) for a in (*outs, *lses)] + [
            pl.BlockSpec((LANES, hd), lambda bi, i: (0, 0)),
            pl.BlockSpec(w_o.shape, lambda bi, i: (0, 0)),
            pl.BlockSpec((None, tm, d), lambda bi, i: (bi, i, 0))],
        out_specs=pl.BlockSpec((None, tm, d), lambda bi, i: (bi, i, 0)),
        out_shape=jax.ShapeDtypeStruct((b, s, d), F32),
        scratch_shapes=[pltpu.VMEM((2, hd // LANES, tm, LANES), F32), pltpu.VMEM((2, tm, LANES), F32)],
        compiler_params=_cparams(("arbitrary", "arbitrary")),
        name="dilated_combine",
    )(*outs, *lses, _head_spread_matrix(), w_o, h)


def _diff_attn_body(qi_ref, ki_ref, q_ref, k_ref, v_ref, bias_ref, lam_ref, g_ref, o_ref,
                    m_ref, acc_ref, *, lambda_init):
    p = pl.program_id(1)
    qi = qi_ref[p]
    ki = ki_ref[p]
    nbatch, t, hw = q_ref.shape
    nt = (((1,), (1,)), ((), ()))
    low = lax.broadcasted_iota(jnp.int32, (1, hw), 1) < HEAD_DIM

    @pl.when(ki == 0)
    def _():
        m_ref[...] = jnp.full(m_ref.shape, MASK_VALUE, F32)
        acc_ref[...] = jnp.zeros_like(acc_ref)

    for bi in range(nbatch):
        q = q_ref[bi] * jnp.asarray(HEAD_DIM ** -0.5, BF16)
        k = k_ref[bi]
        v1 = jnp.concatenate([v_ref[bi], jnp.ones((t, hw), BF16)], axis=1)
        for mp in range(2):
            qm = jnp.where(low, q, 0) if mp == 0 else jnp.where(low, 0, q)
            s = lax.dot_general(qm, k, nt, preferred_element_type=F32) + bias_ref[mp]
            m_old = m_ref[bi, mp]
            m_new = jnp.maximum(m_old, jnp.max(s, axis=-1, keepdims=True))
            alpha = jnp.exp(m_old - m_new)
            pr = jnp.exp(s - jnp.concatenate([m_new] * (t // hw), axis=1))
            acc_ref[bi, mp] = (jnp.concatenate([alpha, alpha], axis=1) * acc_ref[bi, mp]
                               + jnp.dot(pr.astype(BF16), v1, preferred_element_type=F32))
            m_ref[bi, mp] = m_new

    @pl.when(ki == qi)
    def _():
        lam = (jnp.exp(jnp.sum(lam_ref[0:1, :] * lam_ref[1:2, :], axis=-1, keepdims=True))
               - jnp.exp(jnp.sum(lam_ref[2:3, :] * lam_ref[3:4, :], axis=-1, keepdims=True)) + lambda_init)
        gain = g_ref[...] * (1.0 - lambda_init)

        def finish(bi, carry):
            a0 = acc_ref[bi, 0]
            a1 = acc_ref[bi, 1]
            o = a0[:, :hw] / a0[:, hw:] - lam * (a1[:, :hw] / a1[:, hw:])
            o_ref[bi] = _rms(o, gain).astype(o_ref.dtype)
            return carry

        lax.fori_loop(0, nbatch, finish, 0)


def diff_attention(qkv, bias, lam_vecs, subln, lambda_init):
    b, s, _ = qkv.shape
    t = DIFF_TILE
    nq = s // t
    n_delta = bias.shape[2]
    pairs = [(qi, ki) for qi in range(nq) for ki in range(qi + 1)]
    qi_arr = jnp.asarray([pq for pq, _ in pairs], jnp.int32)
    ki_arr = jnp.asarray([pk for _, pk in pairs], jnp.int32)
    hw = 2 * HEAD_DIM
    nh = DIFF_HEADS
    grid_spec = pltpu.PrefetchScalarGridSpec(
        num_scalar_prefetch=2,
        grid=(nh, len(pairs)),
        in_specs=[
            pl.BlockSpec((b, t, hw), lambda h, p, qa, ka: (0, qa[p], h)),
            pl.BlockSpec((b, t, hw), lambda h, p, qa, ka: (0, ka[p], nh + h)),
            pl.BlockSpec((b, t, hw), lambda h, p, qa, ka: (0, ka[p], 2 * nh + h)),
            pl.BlockSpec((None, 2, None, t, t),
                         lambda h, p, qa, ka: (h, 0, jnp.minimum(qa[p] - ka[p], n_delta - 1), 0, 0)),
            pl.BlockSpec((4, HEAD_DIM), lambda h, p, qa, ka: (0, 0)),
            pl.BlockSpec((1, hw), lambda h, p, qa, ka: (0, 0)),
        ],
        out_specs=pl.BlockSpec((b, t, hw), lambda h, p, qa, ka: (0, qa[p], h)),
        scratch_shapes=[pltpu.VMEM((b, 2, t, hw), F32), pltpu.VMEM((b, 2, t, 2 * hw), F32)],
    )
    return pl.pallas_call(
        functools.partial(_diff_attn_body, lambda_init=lambda_init),
        grid_spec=grid_spec,
        out_shape=jax.ShapeDtypeStruct((b, s, nh * hw), BF16),
        compiler_params=_cparams(("arbitrary", "arbitrary")),
        name="diff_attention",
    )(qi_arr, ki_arr, qkv, qkv, qkv, bias, lam_vecs, subln.reshape(1, hw))


def _silu(x):
    h = 0.5 * x
    return h + h * jnp.tanh(h)


def _conv_silu(win_ref, w_ref, b_ref, cols):
    rows = win_ref.shape[0] - SUBLANES
    acc = b_ref[:, cols]
    for k in range(SSM_CONV):
        off = SUBLANES - (SSM_CONV - 1) + k
        acc = acc + w_ref[k:k + 1, cols] * win_ref[pl.ds(off, rows), cols]
    return _silu(acc)


def _spread(w, e_ref):
    hi = w.astype(BF16)
    lo = (w - hi.astype(F32)).astype(BF16)
    return (jnp.dot(hi, e_ref[...], preferred_element_type=F32)
            + jnp.dot(lo, e_ref[...], preferred_element_type=F32))


def _cumsum_rows(a):
    rows = a.shape[0]
    ridx = lax.broadcasted_iota(jnp.int32, a.shape, 0)
    sh = 1
    while sh < rows:
        a = a + jnp.where(ridx >= sh, pltpu.roll(a, sh, 0), 0.0)
        sh *= 2
    return a


def _ssd_body(z_ref, x_ref, bm_ref, cm_ref, dt_ref, cw_ref, cb_ref, dtb_ref, alog_ref, dsk_ref, gn_ref, e_ref,
              y_ref, state_ref, win_ref):
    c = pl.program_id(1)
    l = SSM_CHUNK
    ns = SSM_D_STATE
    di = SSM_D_INNER
    gn = SSM_GROUPS * ns
    hg = SSM_HEADS // SSM_GROUPS
    gw = hg * SSM_HEAD_DIM

    @pl.when(c == 0)
    def _():
        state_ref[...] = jnp.zeros_like(state_ref)
        win_ref[0:SUBLANES, :] = jnp.zeros((SUBLANES, win_ref.shape[1]), F32)

    win_ref[SUBLANES:, 0:di] = x_ref[...]
    win_ref[SUBLANES:, di:di + gn] = bm_ref[...]
    win_ref[SUBLANES:, di + gn:] = cm_ref[...]
    xs = [_conv_silu(win_ref, cw_ref, cb_ref, slice(g * gw, (g + 1) * gw)) for g in range(SSM_GROUPS)]
    bmat = _conv_silu(win_ref, cw_ref, cb_ref, slice(di, di + gn)).astype(BF16)
    cmat = _conv_silu(win_ref, cw_ref, cb_ref, slice(di + gn, di + 2 * gn)).astype(BF16)
    win_ref[0:SUBLANES, :] = win_ref[l:l + SUBLANES, :]

    dt = jax.nn.softplus(dt_ref[...] + dtb_ref[...])
    a_cs = _cumsum_rows(dt * (-jnp.exp(alog_ref[...])))
    a_cs_t = a_cs.T
    last = a_cs[l - 1:l, :]
    dt_e = _spread(dt, e_ref)
    exp_a_e = _spread(jnp.exp(a_cs), e_ref)
    state_decay_e = _spread(jnp.exp(last - a_cs), e_ref)

    causal = (lax.broadcasted_iota(jnp.int32, (l, l), 0) >= lax.broadcasted_iota(jnp.int32, (l, l), 1))
    lane_head = lax.broadcasted_iota(jnp.int32, (1, gw), 1) // SSM_HEAD_DIM
    tn_dims = (((0,), (0,)), ((), ()))
    nt_dims = (((1,), (1,)), ((), ()))

    for g in range(SSM_GROUPS):
        gs = slice(g * gw, (g + 1) * gw)
        cg = cmat[:, g * ns:(g + 1) * ns]
        bg = bmat[:, g * ns:(g + 1) * ns]
        xg = xs[g]
        xdt = xg * dt_e[:, gs]
        xdt_b = xdt.astype(BF16)
        cb = lax.dot_general(cg, bg, nt_dims, preferred_element_type=F32)
        y_diag = None
        for j in range(hg):
            hh = hg * g + j
            seg = a_cs[:, hh:hh + 1] - a_cs_t[hh:hh + 1, :]
            decay = jnp.where(causal, jnp.exp(seg), 0.0)
            part = jnp.dot((cb * decay).astype(BF16), jnp.where(lane_head == j, xdt_b, 0),
                           preferred_element_type=F32)
            y_diag = part if y_diag is None else y_diag + part
        prev = state_ref[g]
        y_off = jnp.dot(cg, prev.astype(BF16), preferred_element_type=F32) * exp_a_e[:, gs]
        new = lax.dot_general(bg, (xdt * state_decay_e[:, gs]).astype(BF16), tn_dims,
                              preferred_element_type=F32)
        state_ref[g] = prev * exp_a_e[l - 1:l, gs] + new
        y = y_diag + y_off + xg * dsk_ref[:, gs]
        zg = z_ref[:, gs]
        y = y * _silu(zg)
        y_ref[:, gs] = _rms(y, gn_ref[:, gs]).astype(y_ref.dtype)


def ssd_core(zx, dt_raw, conv_w, conv_b, dt_bias, a_log, d_skip, gate_norm):
    b, s, _ = zx.shape
    l = SSM_CHUNK
    di = SSM_D_INNER
    gn = SSM_GROUPS * SSM_D_STATE
    conv_dim = di + 2 * gn
    z_spec = pl.BlockSpec((None, l, di), lambda bi, c: (bi, c, 0))
    x_spec = pl.BlockSpec((None, l, di), lambda bi, c: (bi, c, 1))
    b_spec = pl.BlockSpec((None, l, gn), lambda bi, c: (bi, c, 2 * di // gn))
    c_spec = pl.BlockSpec((None, l, gn), lambda bi, c: (bi, c, 2 * di // gn + 1))

    def whole(shape):
        return pl.BlockSpec(shape, lambda bi, c: (0, 0))

    head_lanes = (jnp.arange(LANES)[:, None] == jnp.arange(di)[None, :] // SSM_HEAD_DIM).astype(BF16)
    return pl.pallas_call(
        _ssd_body,
        grid=(b, s // l),
        in_specs=[z_spec, x_spec, b_spec, c_spec,
                  pl.BlockSpec((None, l, LANES), lambda bi, c: (bi, c, 0)),
                  whole((SSM_CONV, conv_dim)), whole((1, conv_dim)),
                  whole((1, LANES)), whole((1, LANES)), whole((1, di)), whole((1, di)), whole((LANES, di))],
        out_specs=pl.BlockSpec((None, l, di), lambda bi, c: (bi, c, 0)),
        out_shape=jax.ShapeDtypeStruct((b, s, di), BF16),
        scratch_shapes=[pltpu.VMEM((SSM_GROUPS, SSM_D_STATE, (SSM_HEADS // SSM_GROUPS) * SSM_HEAD_DIM), F32),
                        pltpu.VMEM((SUBLANES + l, conv_dim), F32)],
        compiler_params=_cparams(("arbitrary", "arbitrary")),
        name="ssd_core",
    )(zx, zx, zx, zx, dt_raw, conv_w, conv_b, dt_bias, a_log, d_skip, gate_norm, head_lanes)


def _dilated_layer(h, norm_g, w_qkv, w_o, dil_biases):
    b, s, d = h.shape
    width = 3 * DIL_HEADS * HEAD_DIM
    outs, lses = [], []
    for g, (_, dilation) in enumerate(DIL_PATTERNS):
        qkv = norm_matmul(h, norm_g, w_qkv, n=width, tn=width, col_block=g, dilation=dilation)
        o, lse = dilated_attention(qkv, dil_biases[g], dilation)
        outs.append(o)
        lses.append(lse)
    return dilated_combine(outs, lses, w_o, h)


def _diff_layer(h, norm_g, w_qkv, lam_q1, lam_k1, lam_q2, lam_k2, subln, w_o, bias, lambda_init):
    b, s, d = h.shape
    width = w_qkv.shape[1]
    qkv = norm_matmul(h, norm_g, w_qkv, n=width, tn=width).reshape(b, s, width)
    lam_vecs = jnp.stack([lam_q1, lam_k1, lam_q2, lam_k2])
    o = diff_attention(qkv, bias, lam_vecs, subln, lambda_init)
    return matmul_residual(o.reshape(b * s, -1), w_o, h.reshape(b * s, d)).reshape(b, s, d)


def _ssd_layer(h, norm_g, w_in, conv_w, conv_b, dt_bias, a_log, d_skip, gate_norm, w_out):
    b, s, d = h.shape
    wide = SSM_D_INNER + conv_w.shape[1]
    pad = LANES - SSM_HEADS
    zx = norm_matmul(h, norm_g, w_in, n=wide, tn=wide // 2, out_dtype=F32).reshape(b, s, wide)
    w_dt = jnp.pad(w_in[:, wide:], ((0, 0), (0, pad)))
    dt_raw = norm_matmul(h, norm_g, w_dt, n=LANES, tn=LANES, out_dtype=F32).reshape(b, s, LANES)
    y = ssd_core(zx, dt_raw, conv_w, conv_b.reshape(1, -1),
                 jnp.pad(dt_bias, (0, pad)).reshape(1, LANES), jnp.pad(a_log, (0, pad)).reshape(1, LANES),
                 jnp.repeat(d_skip, SSM_HEAD_DIM).reshape(1, SSM_D_INNER), gate_norm.reshape(1, SSM_D_INNER))
    return matmul_residual(y.reshape(b * s, -1), w_out, h.reshape(b * s, d)).reshape(b, s, d)


def _diff_lambda_init(layer):
    return 0.8 - 0.6 * math.exp(-0.3 * layer)


def kernel(x, rel_bias,
           l0_mix_norm, l0_dil_w_qkv, l0_dil_w_o, l0_mlp_norm, l0_mlp_w_up, l0_mlp_w_down,
           l1_mix_norm, l1_diff_w_qkv, l1_diff_lam_q1, l1_diff_lam_k1, l1_diff_lam_q2, l1_diff_lam_k2,
           l1_diff_subln, l1_diff_w_o, l1_mlp_norm, l1_mlp_w_up, l1_mlp_w_down,
           l2_mix_norm, l2_ssm_w_in, l2_ssm_conv_w, l2_ssm_conv_b, l2_ssm_dt_bias, l2_ssm_A_log, l2_ssm_D,
           l2_ssm_gate_norm, l2_ssm_w_out, l2_mlp_norm, l2_mlp_w_up, l2_mlp_w_down,
           l3_mix_norm, l3_dil_w_qkv, l3_dil_w_o, l3_mlp_norm, l3_mlp_w_up, l3_mlp_w_down,
           final_norm):
    b, s, d = x.shape
    dil_biases = [dilated_bias(rel_bias, dilation) for _, dilation in DIL_PATTERNS]
    dif_bias = diff_bias(rel_bias, s)

    def mlp(h, g, w_up, w_down, final_gain=None):
        return mlp_block(h.reshape(b * s, d), g, w_up, w_down, final_gain).reshape(b, s, d)

    h = _dilated_layer(x, l0_mix_norm, l0_dil_w_qkv, l0_dil_w_o, dil_biases)
    h = mlp(h, l0_mlp_norm, l0_mlp_w_up, l0_mlp_w_down)
    h = _diff_layer(h, l1_mix_norm, l1_diff_w_qkv, l1_diff_lam_q1, l1_diff_lam_k1, l1_diff_lam_q2,
                    l1_diff_lam_k2, l1_diff_subln, l1_diff_w_o, dif_bias, _diff_lambda_init(1))
    h = mlp(h, l1_mlp_norm, l1_mlp_w_up, l1_mlp_w_down)
    h = _ssd_layer(h, l2_mix_norm, l2_ssm_w_in, l2_ssm_conv_w, l2_ssm_conv_b, l2_ssm_dt_bias, l2_ssm_A_log,
                   l2_ssm_D, l2_ssm_gate_norm, l2_ssm_w_out)
    h = mlp(h, l2_mlp_norm, l2_mlp_w_up, l2_mlp_w_down)
    h = _dilated_layer(h, l3_mix_norm, l3_dil_w_qkv, l3_dil_w_o, dil_biases)
    return mlp(h, l3_mlp_norm, l3_mlp_w_up, l3_mlp_w_down, final_norm)
```

```python
import functools
import math

import jax
import jax.numpy as jnp
from jax import lax
from jax.experimental import pallas as pl
from jax.experimental.pallas import tpu as pltpu

F32 = jnp.float32
BF16 = jnp.bfloat16

NORM_EPS = 1e-5
HEAD_DIM = 64
DIL_PATTERNS = ((128, 1), (512, 4), (2048, 16))
DIL_HEADS = 16
DIL_STEPS = 128
DIFF_HEADS = 8
N_BUCKETS = 32
MAX_DISTANCE = 2048
SSM_HEADS = 32
SSM_GROUPS = 8
SSM_D_STATE = 128
SSM_HEAD_DIM = 64
SSM_D_INNER = SSM_HEADS * SSM_HEAD_DIM
SSM_CONV = 4
SSM_CHUNK = 128
LANES = 128
SUBLANES = 8
MASK_VALUE = -1e30
MIB = 1024 * 1024
VMEM_PROJECTION = 34 * MIB
VMEM_MLP = 40 * MIB
VMEM_RESIDUAL = 18 * MIB
VMEM_COMBINE = 30 * MIB
VMEM_DIFF_ATTENTION = 20 * MIB
VMEM_SSD = 20 * MIB
VMEM_DIFF_BIAS = 14 * MIB
VMEM_SMALL = 6 * MIB

DIFF_TILE = 512


def _bucket_thresholds():
    exact = N_BUCKETS // 2

    def bucket(d):
        if d < exact:
            return d
        big = exact + int(math.log(d / exact) / math.log(MAX_DISTANCE / exact) * (N_BUCKETS - exact))
        return min(big, N_BUCKETS - 1)

    thr = {}
    d = 0
    for b in range(1, N_BUCKETS):
        while bucket(d) < b:
            d += 1
        thr[b] = d
    return thr, bucket


_BUCKET_THR, _bucket_of = _bucket_thresholds()


def _cparams(grid_rank, vmem_bytes):
    return pltpu.CompilerParams(dimension_semantics=("arbitrary",) * grid_rank, vmem_limit_bytes=vmem_bytes)


def _rms(x, gain):
    ms = jnp.mean(x * x, axis=-1, keepdims=True)
    return x * lax.rsqrt(ms + NORM_EPS) * gain


def _norm_matmul_body(x_ref, g_ref, w_ref, o_ref, *slab_ref, r):
    rows, d = x_ref.shape
    if r == 1:
        x = x_ref[...]
    else:
        slab = slab_ref[0]
        per = rows // r
        for c in range(d // LANES):
            slab[c] = x_ref[:, c * LANES:(c + 1) * LANES]
        x = jnp.concatenate(
            [jnp.concatenate([slab[c, pl.ds(rho, per, stride=r), :] for c in range(d // LANES)], axis=1)
             for rho in range(r)], axis=0)
    xn = _rms(x, g_ref[...]).astype(BF16)
    res = jnp.dot(xn, w_ref[...].astype(BF16), preferred_element_type=F32).astype(o_ref.dtype)
    o_ref[...] = res.reshape(o_ref.shape)


def norm_matmul(x, gain, w, *, n, tn, col_block=0, dilation=1, rows=512, out_dtype=BF16):
    b, s, d = x.shape
    r = dilation
    per = rows // r
    scratch = [pltpu.VMEM((d // LANES, rows, LANES), F32)] if r > 1 else []
    return pl.pallas_call(
        functools.partial(_norm_matmul_body, r=r),
        grid=(n // tn, b, s // rows),
        in_specs=[
            pl.BlockSpec((None, rows, d), lambda j, bi, i: (bi, i, 0)),
            pl.BlockSpec((1, d), lambda j, bi, i: (0, 0)),
            pl.BlockSpec((d, tn), lambda j, bi, i: (0, col_block + j), pipeline_mode=pl.Buffered(1)),
        ],
        out_specs=pl.BlockSpec((None, r, per, tn), lambda j, bi, i: (bi, 0, i, j)),
        out_shape=jax.ShapeDtypeStruct((b, r, s // r, n), out_dtype),
        scratch_shapes=scratch,
        compiler_params=_cparams(3, VMEM_PROJECTION),
        name="norm_matmul",
    )(x, gain.reshape(1, d), w)


def _matmul_res_body(a_ref, w_ref, r_ref, o_ref):
    o_ref[...] = r_ref[...] + jnp.dot(a_ref[...], w_ref[...].astype(BF16), preferred_element_type=F32)


def matmul_residual(a, w, res, *, tm=512):
    m, k = a.shape
    n = w.shape[1]
    return pl.pallas_call(
        _matmul_res_body,
        grid=(m // tm,),
        in_specs=[
            pl.BlockSpec((tm, k), lambda i: (i, 0)),
            pl.BlockSpec((k, n), lambda i: (0, 0)),
            pl.BlockSpec((tm, n), lambda i: (i, 0)),
        ],
        out_specs=pl.BlockSpec((tm, n), lambda i: (i, 0)),
        out_shape=jax.ShapeDtypeStruct((m, n), F32),
        compiler_params=_cparams(1, VMEM_RESIDUAL),
        name="matmul_residual",
    )(a, w, res)


def _mlp_body(h_ref, g_ref, wu_ref, wd_ref, fg_ref, o_ref, *, final_norm):
    f = pl.program_id(1)
    last = pl.num_programs(1) - 1

    def contribution():
        xn = _rms(h_ref[...], g_ref[...]).astype(BF16)
        u = jnp.maximum(jnp.dot(xn, wu_ref[...].astype(BF16), preferred_element_type=F32), 0.0)
        return jnp.dot((u * u).astype(BF16), wd_ref[...].astype(BF16), preferred_element_type=F32)

    @pl.when(f == 0)
    def _():
        o_ref[...] = h_ref[...] + contribution()

    @pl.when((f > 0) & (f < last))
    def _():
        o_ref[...] += contribution()

    @pl.when(f == last)
    def _():
        out = o_ref[...] + contribution()
        if final_norm:
            out = _rms(out, fg_ref[...])
        o_ref[...] = out


def mlp_block(h, gain, w_up, w_down, final_gain=None, *, tm=1024, tf=1024):
    m, d = h.shape
    f = w_up.shape[1]
    assert f // tf >= 2
    final_norm = final_gain is not None
    fg = (final_gain if final_norm else gain).reshape(1, d)
    return pl.pallas_call(
        functools.partial(_mlp_body, final_norm=final_norm),
        grid=(m // tm, f // tf),
        in_specs=[
            pl.BlockSpec((tm, d), lambda i, j: (i, 0)),
            pl.BlockSpec((1, d), lambda i, j: (0, 0)),
            pl.BlockSpec((d, tf), lambda i, j: (0, j)),
            pl.BlockSpec((tf, d), lambda i, j: (j, 0)),
            pl.BlockSpec((1, d), lambda i, j: (0, 0)),
        ],
        out_specs=pl.BlockSpec((tm, d), lambda i, j: (i, 0)),
        out_shape=jax.ShapeDtypeStruct((m, d), F32),
        compiler_params=_cparams(2, VMEM_MLP),
        name="mlp_block",
    )(h, gain.reshape(1, d), w_up, w_down, fg)


def _bias_from_dist(dist, col_of_bucket, d_lo, d_hi):
    val = jnp.full(dist.shape, col_of_bucket(_bucket_of(d_lo)), F32)
    for b in range(1, N_BUCKETS):
        t = _BUCKET_THR[b]
        if d_lo < t <= d_hi:
            val = jnp.where(dist >= t, col_of_bucket(b), val)
    return val


def _dil_bias_body(tab_ref, o_ref, *, dilation):
    c = pl.program_id(0)
    n = DIL_STEPS
    qi = lax.broadcasted_iota(jnp.int32, (n, 2 * n), 0)
    kj = lax.broadcasted_iota(jnp.int32, (n, 2 * n), 1)
    steps = n + qi - kj
    band = (steps >= 0) & (steps <= n)
    dist = jnp.clip(steps, 0, n) * dilation
    val = _bias_from_dist(dist, lambda b: tab_ref[b, c], 0, n * dilation)
    later = jnp.where(band, val, MASK_VALUE)
    o_ref[1] = later
    o_ref[0] = jnp.where(kj >= n, later, MASK_VALUE)


def dilated_bias(rel_bias, dilation):
    n = DIL_STEPS
    return pl.pallas_call(
        functools.partial(_dil_bias_body, dilation=dilation),
        grid=(DIL_HEADS,),
        in_specs=[pl.BlockSpec(memory_space=pltpu.SMEM)],
        out_specs=pl.BlockSpec((2, None, n, 2 * n), lambda c: (0, c, 0, 0)),
        out_shape=jax.ShapeDtypeStruct((2, DIL_HEADS, n, 2 * n), F32),
        compiler_params=_cparams(1, VMEM_SMALL),
        name="dilated_bias",
    )(rel_bias)


def _diff_bias_body(tab_ref, o_ref, *, n_delta):
    m = pl.program_id(0)
    h = pl.program_id(1)
    c = m * DIFF_HEADS + h
    t = DIFF_TILE
    qi = lax.broadcasted_iota(jnp.int32, (t, t), 0)
    kj = lax.broadcasted_iota(jnp.int32, (t, t), 1)
    for delta in range(n_delta):
        dist = delta * t + qi - kj
        d_lo = max(delta * t - (t - 1), 0)
        d_hi = delta * t + t - 1
        val = _bias_from_dist(dist, lambda b: tab_ref[b, c], d_lo, d_hi)
        if delta == 0:
            val = jnp.where(dist >= 0, val, MASK_VALUE)
        o_ref[delta] = val


def _diff_first_constant_delta():
    t = DIFF_TILE
    last_thr = _BUCKET_THR[N_BUCKETS - 1]
    delta = 1
    while delta * t - (t - 1) < last_thr:
        delta += 1
    return delta


def _diff_n_delta(s):
    return min(_diff_first_constant_delta() + 1, s // DIFF_TILE)


def diff_bias(rel_bias, s):
    t = DIFF_TILE
    n_delta = _diff_n_delta(s)
    return pl.pallas_call(
        functools.partial(_diff_bias_body, n_delta=n_delta),
        grid=(2, DIFF_HEADS),
        in_specs=[pl.BlockSpec(memory_space=pltpu.SMEM)],
        out_specs=pl.BlockSpec((None, None, n_delta, t, t), lambda m, h: (h, m, 0, 0, 0)),
        out_shape=jax.ShapeDtypeStruct((DIFF_HEADS, 2, n_delta, t, t), F32),
        compiler_params=_cparams(2, VMEM_DIFF_BIAS),
        name="diff_bias",
    )(rel_bias)


def _dil_attn_body(q_ref, kc_ref, vc_ref, bias_ref, o_ref, lse_ref, kp_ref, vp_ref):
    n = DIL_STEPS
    hd = DIL_HEADS * HEAD_DIM
    nt = (((1,), (1,)), ((), ()))
    lane = lax.broadcasted_iota(jnp.int32, (1, LANES), 1)
    low = lane < HEAD_DIM
    ones_low = jnp.where(low, 1, 0).astype(BF16) + jnp.zeros((2 * n, LANES), BF16)
    ones_high = jnp.where(low, 0, 1).astype(BF16) + jnp.zeros((2 * n, LANES), BF16)

    @pl.when(pl.program_id(2) == 0)
    def _():
        kp_ref[...] = jnp.zeros_like(kp_ref)
        vp_ref[...] = jnp.zeros_like(vp_ref)

    lse_tile = jnp.zeros((n, LANES), F32)
    for pr in range(hd // LANES):
        cs = slice(pr * LANES, (pr + 1) * LANES)
        q2 = q_ref[:, cs] * jnp.asarray(HEAD_DIM ** -0.5, BF16)
        k2 = jnp.concatenate([kp_ref[:, cs], kc_ref[:, cs]], axis=0)
        v2 = jnp.concatenate([vp_ref[:, cs], vc_ref[:, cs]], axis=0)
        qs = jnp.concatenate([jnp.where(low, q2, 0), jnp.where(low, 0, q2)], axis=0)
        bias = bias_ref[2 * pr:2 * pr + 2].reshape(2 * n, 2 * n)
        s = lax.dot_general(qs, k2, nt, preferred_element_type=F32) + bias
        m = jnp.max(jnp.maximum(s[:, :n], s[:, n:]), axis=-1, keepdims=True)
        p = jnp.exp(s - m).astype(BF16)
        w = jnp.concatenate(
            [jnp.concatenate([jnp.where(low, v2, 0), ones_low], axis=1),
             jnp.concatenate([jnp.where(low, 0, v2), ones_high], axis=1)], axis=0)
        r = jnp.dot(jnp.concatenate([p[:n], p[n:]], axis=1), w, preferred_element_type=F32)
        den = r[:, LANES:]
        o_ref[:, cs] = (r[:, :LANES] / den).astype(o_ref.dtype)
        lse = jnp.where(low, m[:n], m[n:]) + jnp.log(den)
        lse_tile = jnp.where((lane == pr) | (lane == HEAD_DIM + pr), lse, lse_tile)
    lse_ref[...] = lse_tile
    kp_ref[...] = kc_ref[...]
    vp_ref[...] = vc_ref[...]


def dilated_attention(qkv, bias, dilation):
    b, r, length, _ = qkv.shape
    n = DIL_STEPS
    hd = DIL_HEADS * HEAD_DIM
    nb = length // n

    def cur(col):
        return pl.BlockSpec((None, None, n, hd), lambda bi, rho, i: (bi, rho, i, col))

    return pl.pallas_call(
        _dil_attn_body,
        grid=(b, r, nb),
        in_specs=[cur(0), cur(1), cur(2),
                  pl.BlockSpec((None, DIL_HEADS, n, 2 * n), lambda bi, rho, i: (jnp.minimum(i, 1), 0, 0, 0))],
        out_specs=[pl.BlockSpec((None, None, n, hd), lambda bi, rho, i: (bi, rho, i, 0)),
                   pl.BlockSpec((None, None, n, LANES), lambda bi, rho, i: (bi, rho, i, 0))],
        out_shape=[jax.ShapeDtypeStruct((b, r, length, hd), BF16),
                   jax.ShapeDtypeStruct((b, r, length, LANES), F32)],
        scratch_shapes=[pltpu.VMEM((n, hd), BF16), pltpu.VMEM((n, hd), BF16)],
        compiler_params=_cparams(3, VMEM_SMALL),
        name=f"dilated_attention_r{dilation}",
    )(qkv, qkv, qkv, bias)


def _dil_combine_body(o1_ref, o2_ref, o3_ref, l1_ref, l2_ref, l3_ref, e_ref, w_ref, h_ref, out_ref,
                      oslab_ref, lslab_ref):
    tm = out_ref.shape[0]
    nslab = oslab_ref.shape[1]

    def token_order(o_ref, l_ref, k):
        r = o_ref.shape[0]
        if r == 1:
            return o_ref[0].astype(F32), l_ref[0]
        per = tm // r
        for rho in range(r):
            blk = o_ref[rho].astype(F32)
            for c in range(nslab):
                oslab_ref[k, c, pl.ds(rho, per, stride=r), :] = blk[:, c * LANES:(c + 1) * LANES]
            lslab_ref[k, pl.ds(rho, per, stride=r), :] = l_ref[rho]
        return jnp.concatenate([oslab_ref[k, c] for c in range(nslab)], axis=1), lslab_ref[k]

    o1, l1 = token_order(o1_ref, l1_ref, 0)
    o2, l2 = token_order(o2_ref, l2_ref, 0)
    o3, l3 = token_order(o3_ref, l3_ref, 1)
    mx = jnp.maximum(jnp.maximum(l1, l2), l3)
    e1, e2, e3 = jnp.exp(l1 - mx), jnp.exp(l2 - mx), jnp.exp(l3 - mx)
    inv = 1.0 / (e1 + e2 + e3)

    def spread(w):
        hi = w.astype(BF16)
        lo = (w - hi.astype(F32)).astype(BF16)
        return (jnp.dot(hi, e_ref[...], preferred_element_type=F32)
                + jnp.dot(lo, e_ref[...], preferred_element_type=F32))

    mix = o3 + spread(e1 * inv) * (o1 - o3) + spread(e2 * inv) * (o2 - o3)
    out_ref[...] = h_ref[...] + jnp.dot(mix.astype(BF16), w_ref[...].astype(BF16), preferred_element_type=F32)


def _head_spread_matrix():
    hd = DIL_HEADS * HEAD_DIM
    lane = jnp.arange(LANES)[:, None]
    col = jnp.arange(hd)[None, :]
    head = col // HEAD_DIM
    src = jnp.where(head % 2 == 0, head // 2, HEAD_DIM + head // 2)
    return (lane == src).astype(BF16)


def dilated_combine(outs, lses, w_o, h, *, tm=512):
    b, s, d = h.shape
    hd = w_o.shape[0]

    def regrouped(arr):
        r = arr.shape[1]
        return pl.BlockSpec((None, r, tm // r, arr.shape[3]), lambda bi, i: (bi, 0, i, 0))

    return pl.pallas_call(
        _dil_combine_body,
        grid=(b, s // tm),
        in_specs=[regrouped(a) for a in (*outs, *lses)] + [
            pl.BlockSpec((LANES, hd), lambda bi, i: (0, 0)),
            pl.BlockSpec(w_o.shape, lambda bi, i: (0, 0)),
            pl.BlockSpec((None, tm, d), lambda bi, i: (bi, i, 0))],
        out_specs=pl.BlockSpec((None, tm, d), lambda bi, i: (bi, i, 0)),
        out_shape=jax.ShapeDtypeStruct((b, s, d), F32),
        scratch_shapes=[pltpu.VMEM((2, hd // LANES, tm, LANES), F32), pltpu.VMEM((2, tm, LANES), F32)],
        compiler_params=_cparams(2, VMEM_COMBINE),
        name="dilated_combine",
    )(*outs, *lses, _head_spread_matrix(), w_o, h)


def _diff_attn_body(qi_ref, ki_ref, q_ref, k_ref, v_ref, bias_ref, lam_ref, g_ref, o_ref,
                    m_ref, acc_ref, *, lambda_init):
    p = pl.program_id(1)
    qi = qi_ref[p]
    ki = ki_ref[p]
    nbatch, t, hw = q_ref.shape
    nt = (((1,), (1,)), ((), ()))
    low = lax.broadcasted_iota(jnp.int32, (1, hw), 1) < HEAD_DIM

    @pl.when(ki == 0)
    def _():
        m_ref[...] = jnp.full(m_ref.shape, MASK_VALUE, F32)
        acc_ref[...] = jnp.zeros_like(acc_ref)

    for bi in range(nbatch):
        q = q_ref[bi] * jnp.asarray(HEAD_DIM ** -0.5, BF16)
        k = k_ref[bi]
        v1 = jnp.concatenate([v_ref[bi], jnp.ones((t, hw), BF16)], axis=1)
        for mp in range(2):
            qm = jnp.where(low, q, 0) if mp == 0 else jnp.where(low, 0, q)
            s = lax.dot_general(qm, k, nt, preferred_element_type=F32) + bias_ref[mp]
            m_old = m_ref[bi, mp]
            m_new = jnp.maximum(m_old, jnp.max(s, axis=-1, keepdims=True))
            alpha = jnp.exp(m_old - m_new)
            pr = jnp.exp(s - jnp.concatenate([m_new] * (t // hw), axis=1))
            acc_ref[bi, mp] = (jnp.concatenate([alpha, alpha], axis=1) * acc_ref[bi, mp]
                               + jnp.dot(pr.astype(BF16), v1, preferred_element_type=F32))
            m_ref[bi, mp] = m_new

    @pl.when(ki == qi)
    def _():
        lam = (jnp.exp(jnp.sum(lam_ref[0:1, :] * lam_ref[1:2, :], axis=-1, keepdims=True))
               - jnp.exp(jnp.sum(lam_ref[2:3, :] * lam_ref[3:4, :], axis=-1, keepdims=True)) + lambda_init)
        gain = g_ref[...] * (1.0 - lambda_init)

        def finish(bi, carry):
            a0 = acc_ref[bi, 0]
            a1 = acc_ref[bi, 1]
            o = a0[:, :hw] / a0[:, hw:] - lam * (a1[:, :hw] / a1[:, hw:])
            o_ref[bi] = _rms(o, gain).astype(o_ref.dtype)
            return carry

        lax.fori_loop(0, nbatch, finish, 0)


def diff_attention(qkv, bias, lam_vecs, subln, lambda_init):
    b, s, _ = qkv.shape
    t = DIFF_TILE
    nq = s // t
    n_delta = bias.shape[2]
    pairs = [(qi, ki) for qi in range(nq) for ki in range(qi + 1)]
    qi_arr = jnp.asarray([pq for pq, _ in pairs], jnp.int32)
    ki_arr = jnp.asarray([pk for _, pk in pairs], jnp.int32)
    hw = 2 * HEAD_DIM
    nh = DIFF_HEADS
    grid_spec = pltpu.PrefetchScalarGridSpec(
        num_scalar_prefetch=2,
        grid=(nh, len(pairs)),
        in_specs=[
            pl.BlockSpec((b, t, hw), lambda h, p, qa, ka: (0, qa[p], h)),
            pl.BlockSpec((b, t, hw), lambda h, p, qa, ka: (0, ka[p], nh + h)),
            pl.BlockSpec((b, t, hw), lambda h, p, qa, ka: (0, ka[p], 2 * nh + h)),
            pl.BlockSpec((None, 2, None, t, t),
                         lambda h, p, qa, ka: (h, 0, jnp.minimum(qa[p] - ka[p], n_delta - 1), 0, 0)),
            pl.BlockSpec((4, HEAD_DIM), lambda h, p, qa, ka: (0, 0)),
            pl.BlockSpec((1, hw), lambda h, p, qa, ka: (0, 0)),
        ],
        out_specs=pl.BlockSpec((b, t, hw), lambda h, p, qa, ka: (0, qa[p], h)),
        scratch_shapes=[pltpu.VMEM((b, 2, t, hw), F32), pltpu.VMEM((b, 2, t, 2 * hw), F32)],
    )
    return pl.pallas_call(
        functools.partial(_diff_attn_body, lambda_init=lambda_init),
        grid_spec=grid_spec,
        out_shape=jax.ShapeDtypeStruct((b, s, nh * hw), BF16),
        compiler_params=_cparams(2, VMEM_DIFF_ATTENTION),
        name="diff_attention",
    )(qi_arr, ki_arr, qkv, qkv, qkv, bias, lam_vecs, subln.reshape(1, hw))


def _silu(x):
    h = 0.5 * x
    return h + h * jnp.tanh(h)


def _conv_silu(win_ref, w_ref, b_ref, cols):
    rows = win_ref.shape[0] - SUBLANES
    acc = b_ref[:, cols]
    for k in range(SSM_CONV):
        off = SUBLANES - (SSM_CONV - 1) + k
        acc = acc + w_ref[k:k + 1, cols] * win_ref[pl.ds(off, rows), cols]
    return _silu(acc)


def _spread(w, e_ref):
    hi = w.astype(BF16)
    lo = (w - hi.astype(F32)).astype(BF16)
    return (jnp.dot(hi, e_ref[...], preferred_element_type=F32)
            + jnp.dot(lo, e_ref[...], preferred_element_type=F32))


def _cumsum_rows(a):
    rows = a.shape[0]
    ridx = lax.broadcasted_iota(jnp.int32, a.shape, 0)
    sh = 1
    while sh < rows:
        a = a + jnp.where(ridx >= sh, pltpu.roll(a, sh, 0), 0.0)
        sh *= 2
    return a


def _ssd_body(z_ref, x_ref, bm_ref, cm_ref, dt_ref, cw_ref, cb_ref, dtb_ref, alog_ref, dsk_ref, gn_ref, e_ref,
              y_ref, state_ref, win_ref):
    c = pl.program_id(1)
    l = SSM_CHUNK
    ns = SSM_D_STATE
    di = SSM_D_INNER
    gn = SSM_GROUPS * ns
    hg = SSM_HEADS // SSM_GROUPS
    gw = hg * SSM_HEAD_DIM

    @pl.when(c == 0)
    def _():
        state_ref[...] = jnp.zeros_like(state_ref)
        win_ref[0:SUBLANES, :] = jnp.zeros((SUBLANES, win_ref.shape[1]), F32)

    win_ref[SUBLANES:, 0:di] = x_ref[...]
    win_ref[SUBLANES:, di:di + gn] = bm_ref[...]
    win_ref[SUBLANES:, di + gn:] = cm_ref[...]
    xs = [_conv_silu(win_ref, cw_ref, cb_ref, slice(g * gw, (g + 1) * gw)) for g in range(SSM_GROUPS)]
    bmat = _conv_silu(win_ref, cw_ref, cb_ref, slice(di, di + gn)).astype(BF16)
    cmat = _conv_silu(win_ref, cw_ref, cb_ref, slice(di + gn, di + 2 * gn)).astype(BF16)
    win_ref[0:SUBLANES, :] = win_ref[l:l + SUBLANES, :]

    dt = jax.nn.softplus(dt_ref[...] + dtb_ref[...])
    a_cs = _cumsum_rows(dt * (-jnp.exp(alog_ref[...])))
    a_cs_t = a_cs.T
    last = a_cs[l - 1:l, :]
    dt_e = _spread(dt, e_ref)
    exp_a_e = _spread(jnp.exp(a_cs), e_ref)
    state_decay_e = _spread(jnp.exp(last - a_cs), e_ref)

    causal = (lax.broadcasted_iota(jnp.int32, (l, l), 0) >= lax.broadcasted_iota(jnp.int32, (l, l), 1))
    lane_head = lax.broadcasted_iota(jnp.int32, (1, gw), 1) // SSM_HEAD_DIM
    tn_dims = (((0,), (0,)), ((), ()))
    nt_dims = (((1,), (1,)), ((), ()))

    for g in range(SSM_GROUPS):
        gs = slice(g * gw, (g + 1) * gw)
        cg = cmat[:, g * ns:(g + 1) * ns]
        bg = bmat[:, g * ns:(g + 1) * ns]
        xg = xs[g]
        xdt = xg * dt_e[:, gs]
        xdt_b = xdt.astype(BF16)
        cb = lax.dot_general(cg, bg, nt_dims, preferred_element_type=F32)
        y_diag = None
        for j in range(hg):
            hh = hg * g + j
            seg = a_cs[:, hh:hh + 1] - a_cs_t[hh:hh + 1, :]
            decay = jnp.where(causal, jnp.exp(seg), 0.0)
            part = jnp.dot((cb * decay).astype(BF16), jnp.where(lane_head == j, xdt_b, 0),
                           preferred_element_type=F32)
            y_diag = part if y_diag is None else y_diag + part
        prev = state_ref[g]
        y_off = jnp.dot(cg, prev.astype(BF16), preferred_element_type=F32) * exp_a_e[:, gs]
        new = lax.dot_general(bg, (xdt * state_decay_e[:, gs]).astype(BF16), tn_dims,
                              preferred_element_type=F32)
        state_ref[g] = prev * exp_a_e[l - 1:l, gs] + new
        y = y_diag + y_off + xg * dsk_ref[:, gs]
        zg = z_ref[:, gs]
        y = y * _silu(zg)
        y_ref[:, gs] = _rms(y, gn_ref[:, gs]).astype(y_ref.dtype)


def ssd_core(zx, dt_raw, conv_w, conv_b, dt_bias, a_log, d_skip, gate_norm):
    b, s, _ = zx.shape
    l = SSM_CHUNK
    di = SSM_D_INNER
    gn = SSM_GROUPS * SSM_D_STATE
    conv_dim = di + 2 * gn
    z_spec = pl.BlockSpec((None, l, di), lambda bi, c: (bi, c, 0))
    x_spec = pl.BlockSpec((None, l, di), lambda bi, c: (bi, c, 1))
    b_spec = pl.BlockSpec((None, l, gn), lambda bi, c: (bi, c, 2 * di // gn))
    c_spec = pl.BlockSpec((None, l, gn), lambda bi, c: (bi, c, 2 * di // gn + 1))

    def whole(shape):
        return pl.BlockSpec(shape, lambda bi, c: (0, 0))

    head_lanes = (jnp.arange(LANES)[:, None] == jnp.arange(di)[None, :] // SSM_HEAD_DIM).astype(BF16)
    return pl.pallas_call(
        _ssd_body,
        grid=(b, s // l),
        in_specs=[z_spec, x_spec, b_spec, c_spec,
                  pl.BlockSpec((None, l, LANES), lambda bi, c: (bi, c, 0)),
                  whole((SSM_CONV, conv_dim)), whole((1, conv_dim)),
                  whole((1, LANES)), whole((1, LANES)), whole((1, di)), whole((1, di)), whole((LANES, di))],
        out_specs=pl.BlockSpec((None, l, di), lambda bi, c: (bi, c, 0)),
        out_shape=jax.ShapeDtypeStruct((b, s, di), BF16),
        scratch_shapes=[pltpu.VMEM((SSM_GROUPS, SSM_D_STATE, (SSM_HEADS // SSM_GROUPS) * SSM_HEAD_DIM), F32),
                        pltpu.VMEM((SUBLANES + l, conv_dim), F32)],
        compiler_params=_cparams(2, VMEM_SSD),
        name="ssd_core",
    )(zx, zx, zx, zx, dt_raw, conv_w, conv_b, dt_bias, a_log, d_skip, gate_norm, head_lanes)


def _dilated_layer(h, norm_g, w_qkv, w_o, dil_biases):
    b, s, d = h.shape
    width = 3 * DIL_HEADS * HEAD_DIM
    outs, lses = [], []
    for g, (_, dilation) in enumerate(DIL_PATTERNS):
        qkv = norm_matmul(h, norm_g, w_qkv, n=width, tn=width, col_block=g, dilation=dilation)
        o, lse = dilated_attention(qkv, dil_biases[g], dilation)
        outs.append(o)
        lses.append(lse)
    return dilated_combine(outs, lses, w_o, h)


def _diff_layer(h, norm_g, w_qkv, lam_q1, lam_k1, lam_q2, lam_k2, subln, w_o, bias, lambda_init):
    b, s, d = h.shape
    width = w_qkv.shape[1]
    qkv = norm_matmul(h, norm_g, w_qkv, n=width, tn=width).reshape(b, s, width)
    lam_vecs = jnp.stack([lam_q1, lam_k1, lam_q2, lam_k2])
    o = diff_attention(qkv, bias, lam_vecs, subln, lambda_init)
    return matmul_residual(o.reshape(b * s, -1), w_o, h.reshape(b * s, d)).reshape(b, s, d)


def _ssd_layer(h, norm_g, w_in, conv_w, conv_b, dt_bias, a_log, d_skip, gate_norm, w_out):
    b, s, d = h.shape
    wide = SSM_D_INNER + conv_w.shape[1]
    pad = LANES - SSM_HEADS
    zx = norm_matmul(h, norm_g, w_in, n=wide, tn=wide // 2, out_dtype=F32).reshape(b, s, wide)
    w_dt = jnp.pad(w_in[:, wide:], ((0, 0), (0, pad)))
    dt_raw = norm_matmul(h, norm_g, w_dt, n=LANES, tn=LANES, out_dtype=F32).reshape(b, s, LANES)
    y = ssd_core(zx, dt_raw, conv_w, conv_b.reshape(1, -1),
                 jnp.pad(dt_bias, (0, pad)).reshape(1, LANES), jnp.pad(a_log, (0, pad)).reshape(1, LANES),
                 jnp.repeat(d_skip, SSM_HEAD_DIM).reshape(1, SSM_D_INNER), gate_norm.reshape(1, SSM_D_INNER))
    return matmul_residual(y.reshape(b * s, -1), w_out, h.reshape(b * s, d)).reshape(b, s, d)


def _diff_lambda_init(layer):
    return 0.8 - 0.6 * math.exp(-0.3 * layer)


def kernel(x, rel_bias,
           l0_mix_norm, l0_dil_w_qkv, l0_dil_w_o, l0_mlp_norm, l0_mlp_w_up, l0_mlp_w_down,
           l1_mix_norm, l1_diff_w_qkv, l1_diff_lam_q1, l1_diff_lam_k1, l1_diff_lam_q2, l1_diff_lam_k2,
           l1_diff_subln, l1_diff_w_o, l1_mlp_norm, l1_mlp_w_up, l1_mlp_w_down,
           l2_mix_norm, l2_ssm_w_in, l2_ssm_conv_w, l2_ssm_conv_b, l2_ssm_dt_bias, l2_ssm_A_log, l2_ssm_D,
           l2_ssm_gate_norm, l2_ssm_w_out, l2_mlp_norm, l2_mlp_w_up, l2_mlp_w_down,
           l3_mix_norm, l3_dil_w_qkv, l3_dil_w_o, l3_mlp_norm, l3_mlp_w_up, l3_mlp_w_down,
           final_norm):
    b, s, d = x.shape
    dil_biases = [dilated_bias(rel_bias, dilation) for _, dilation in DIL_PATTERNS]
    dif_bias = diff_bias(rel_bias, s)

    def mlp(h, g, w_up, w_down, final_gain=None):
        return mlp_block(h.reshape(b * s, d), g, w_up, w_down, final_gain).reshape(b, s, d)

    h = _dilated_layer(x, l0_mix_norm, l0_dil_w_qkv, l0_dil_w_o, dil_biases)
    h = mlp(h, l0_mlp_norm, l0_mlp_w_up, l0_mlp_w_down)
    h = _diff_layer(h, l1_mix_norm, l1_diff_w_qkv, l1_diff_lam_q1, l1_diff_lam_k1, l1_diff_lam_q2,
                    l1_diff_lam_k2, l1_diff_subln, l1_diff_w_o, dif_bias, _diff_lambda_init(1))
    h = mlp(h, l1_mlp_norm, l1_mlp_w_up, l1_mlp_w_down)
    h = _ssd_layer(h, l2_mix_norm, l2_ssm_w_in, l2_ssm_conv_w, l2_ssm_conv_b, l2_ssm_dt_bias, l2_ssm_A_log,
                   l2_ssm_D, l2_ssm_gate_norm, l2_ssm_w_out)
    h = mlp(h, l2_mlp_norm, l2_mlp_w_up, l2_mlp_w_down)
    h = _dilated_layer(h, l3_mix_norm, l3_dil_w_qkv, l3_dil_w_o, dil_biases)
    return mlp(h, l3_mlp_norm, l3_mlp_w_up, l3_mlp_w_down, final_norm)
```

```python
import functools
import math

import jax
import jax.numpy as jnp
from jax import lax
from jax.experimental import pallas as pl
from jax.experimental.pallas import tpu as pltpu

F32 = jnp.float32
BF16 = jnp.bfloat16

NORM_EPS = 1e-5
HEAD_DIM = 64
DIL_PATTERNS = ((128, 1), (512, 4), (2048, 16))
DIL_HEADS = 16
DIL_STEPS = 128
DIFF_HEADS = 8
N_BUCKETS = 32
MAX_DISTANCE = 2048
SSM_HEADS = 32
SSM_GROUPS = 8
SSM_D_STATE = 128
SSM_HEAD_DIM = 64
SSM_D_INNER = SSM_HEADS * SSM_HEAD_DIM
SSM_CONV = 4
SSM_CHUNK = 128
LANES = 128
SUBLANES = 8
MASK_VALUE = -1e30
MIB = 1024 * 1024
VMEM_PROJECTION = 34 * MIB
VMEM_MLP = 52 * MIB
VMEM_RESIDUAL = 36 * MIB
VMEM_COMBINE = 40 * MIB
VMEM_DIFF_ATTENTION = 20 * MIB
VMEM_SSD = 20 * MIB
VMEM_DIFF_BIAS = 14 * MIB
VMEM_SMALL = 6 * MIB

DIFF_TILE = 512


def _bucket_thresholds():
    exact = N_BUCKETS // 2

    def bucket(d):
        if d < exact:
            return d
        big = exact + int(math.log(d / exact) / math.log(MAX_DISTANCE / exact) * (N_BUCKETS - exact))
        return min(big, N_BUCKETS - 1)

    thr = {}
    d = 0
    for b in range(1, N_BUCKETS):
        while bucket(d) < b:
            d += 1
        thr[b] = d
    return thr, bucket


_BUCKET_THR, _bucket_of = _bucket_thresholds()


def _cparams(grid_rank, vmem_bytes):
    return pltpu.CompilerParams(dimension_semantics=("arbitrary",) * grid_rank, vmem_limit_bytes=vmem_bytes)


def _rms(x, gain):
    ms = jnp.mean(x * x, axis=-1, keepdims=True)
    return x * lax.rsqrt(ms + NORM_EPS) * gain


def _norm_matmul_body(x_ref, g_ref, w_ref, o_ref, *slab_ref, r):
    rows, d = x_ref.shape
    if r == 1:
        x = x_ref[...]
    else:
        slab = slab_ref[0]
        per = rows // r
        for c in range(d // LANES):
            slab[c] = x_ref[:, c * LANES:(c + 1) * LANES]
        x = jnp.concatenate(
            [jnp.concatenate([slab[c, pl.ds(rho, per, stride=r), :] for c in range(d // LANES)], axis=1)
             for rho in range(r)], axis=0)
    xn = _rms(x, g_ref[...]).astype(BF16)
    res = jnp.dot(xn, w_ref[...].astype(BF16), preferred_element_type=F32).astype(o_ref.dtype)
    o_ref[...] = res.reshape(o_ref.shape)


def norm_matmul(x, gain, w, *, n, tn, col_block=0, dilation=1, rows=512, out_dtype=BF16):
    b, s, d = x.shape
    r = dilation
    per = rows // r
    scratch = [pltpu.VMEM((d // LANES, rows, LANES), F32)] if r > 1 else []
    return pl.pallas_call(
        functools.partial(_norm_matmul_body, r=r),
        grid=(n // tn, b, s // rows),
        in_specs=[
            pl.BlockSpec((None, rows, d), lambda j, bi, i: (bi, i, 0)),
            pl.BlockSpec((1, d), lambda j, bi, i: (0, 0)),
            pl.BlockSpec((d, tn), lambda j, bi, i: (0, col_block + j), pipeline_mode=pl.Buffered(1)),
        ],
        out_specs=pl.BlockSpec((None, r, per, tn), lambda j, bi, i: (bi, 0, i, j)),
        out_shape=jax.ShapeDtypeStruct((b, r, s // r, n), out_dtype),
        scratch_shapes=scratch,
        compiler_params=_cparams(3, VMEM_PROJECTION),
        name="norm_matmul",
    )(x, gain.reshape(1, d), w)


def _matmul_res_body(a_ref, w_ref, r_ref, o_ref):
    o_ref[...] = r_ref[...] + jnp.dot(a_ref[...], w_ref[...].astype(BF16), preferred_element_type=F32)


def matmul_residual(a, w, res, *, tm=512):
    m, k = a.shape
    n = w.shape[1]
    return pl.pallas_call(
        _matmul_res_body,
        grid=(m // tm,),
        in_specs=[
            pl.BlockSpec((tm, k), lambda i: (i, 0)),
            pl.BlockSpec((k, n), lambda i: (0, 0)),
            pl.BlockSpec((tm, n), lambda i: (i, 0)),
        ],
        out_specs=pl.BlockSpec((tm, n), lambda i: (i, 0)),
        out_shape=jax.ShapeDtypeStruct((m, n), F32),
        compiler_params=_cparams(1, VMEM_RESIDUAL),
        name="matmul_residual",
    )(a, w, res)


def _mlp_body(h_ref, g_ref, wu_ref, wd_ref, fg_ref, o_ref, *, final_norm):
    f = pl.program_id(1)
    last = pl.num_programs(1) - 1

    def contribution():
        xn = _rms(h_ref[...], g_ref[...]).astype(BF16)
        u = jnp.maximum(jnp.dot(xn, wu_ref[...].astype(BF16), preferred_element_type=F32), 0.0)
        return jnp.dot((u * u).astype(BF16), wd_ref[...].astype(BF16), preferred_element_type=F32)

    @pl.when(f == 0)
    def _():
        o_ref[...] = h_ref[...] + contribution()

    @pl.when((f > 0) & (f < last))
    def _():
        o_ref[...] += contribution()

    @pl.when(f == last)
    def _():
        out = o_ref[...] + contribution()
        if final_norm:
            out = _rms(out, fg_ref[...])
        o_ref[...] = out


def mlp_block(h, gain, w_up, w_down, final_gain=None, *, tm=1024, tf=1024):
    m, d = h.shape
    f = w_up.shape[1]
    assert f // tf >= 2
    final_norm = final_gain is not None
    fg = (final_gain if final_norm else gain).reshape(1, d)
    return pl.pallas_call(
        functools.partial(_mlp_body, final_norm=final_norm),
        grid=(m // tm, f // tf),
        in_specs=[
            pl.BlockSpec((tm, d), lambda i, j: (i, 0)),
            pl.BlockSpec((1, d), lambda i, j: (0, 0)),
            pl.BlockSpec((d, tf), lambda i, j: (0, j)),
            pl.BlockSpec((tf, d), lambda i, j: (j, 0)),
            pl.BlockSpec((1, d), lambda i, j: (0, 0)),
        ],
        out_specs=pl.BlockSpec((tm, d), lambda i, j: (i, 0)),
        out_shape=jax.ShapeDtypeStruct((m, d), F32),
        compiler_params=_cparams(2, VMEM_MLP),
        name="mlp_block",
    )(h, gain.reshape(1, d), w_up, w_down, fg)


def _bias_from_dist(dist, col_of_bucket, d_lo, d_hi):
    val = jnp.full(dist.shape, col_of_bucket(_bucket_of(d_lo)), F32)
    for b in range(1, N_BUCKETS):
        t = _BUCKET_THR[b]
        if d_lo < t <= d_hi:
            val = jnp.where(dist >= t, col_of_bucket(b), val)
    return val


def _dil_bias_body(tab_ref, o_ref, *, dilation):
    c = pl.program_id(0)
    n = DIL_STEPS
    qi = lax.broadcasted_iota(jnp.int32, (n, 2 * n), 0)
    kj = lax.broadcasted_iota(jnp.int32, (n, 2 * n), 1)
    steps = n + qi - kj
    band = (steps >= 0) & (steps <= n)
    dist = jnp.clip(steps, 0, n) * dilation
    val = _bias_from_dist(dist, lambda b: tab_ref[b, c], 0, n * dilation)
    later = jnp.where(band, val, MASK_VALUE)
    o_ref[1] = later
    o_ref[0] = jnp.where(kj >= n, later, MASK_VALUE)


def dilated_bias(rel_bias, dilation):
    n = DIL_STEPS
    return pl.pallas_call(
        functools.partial(_dil_bias_body, dilation=dilation),
        grid=(DIL_HEADS,),
        in_specs=[pl.BlockSpec(memory_space=pltpu.SMEM)],
        out_specs=pl.BlockSpec((2, None, n, 2 * n), lambda c: (0, c, 0, 0)),
        out_shape=jax.ShapeDtypeStruct((2, DIL_HEADS, n, 2 * n), F32),
        compiler_params=_cparams(1, VMEM_SMALL),
        name="dilated_bias",
    )(rel_bias)


def _diff_bias_body(tab_ref, o_ref, *, n_delta):
    m = pl.program_id(0)
    h = pl.program_id(1)
    c = m * DIFF_HEADS + h
    t = DIFF_TILE
    qi = lax.broadcasted_iota(jnp.int32, (t, t), 0)
    kj = lax.broadcasted_iota(jnp.int32, (t, t), 1)
    for delta in range(n_delta):
        dist = delta * t + qi - kj
        d_lo = max(delta * t - (t - 1), 0)
        d_hi = delta * t + t - 1
        val = _bias_from_dist(dist, lambda b: tab_ref[b, c], d_lo, d_hi)
        if delta == 0:
            val = jnp.where(dist >= 0, val, MASK_VALUE)
        o_ref[delta] = val


def _diff_first_constant_delta():
    t = DIFF_TILE
    last_thr = _BUCKET_THR[N_BUCKETS - 1]
    delta = 1
    while delta * t - (t - 1) < last_thr:
        delta += 1
    return delta


def _diff_n_delta(s):
    return min(_diff_first_constant_delta() + 1, s // DIFF_TILE)


def diff_bias(rel_bias, s):
    t = DIFF_TILE
    n_delta = _diff_n_delta(s)
    return pl.pallas_call(
        functools.partial(_diff_bias_body, n_delta=n_delta),
        grid=(2, DIFF_HEADS),
        in_specs=[pl.BlockSpec(memory_space=pltpu.SMEM)],
        out_specs=pl.BlockSpec((None, None, n_delta, t, t), lambda m, h: (h, m, 0, 0, 0)),
        out_shape=jax.ShapeDtypeStruct((DIFF_HEADS, 2, n_delta, t, t), F32),
        compiler_params=_cparams(2, VMEM_DIFF_BIAS),
        name="diff_bias",
    )(rel_bias)


def _dil_attn_body(q_ref, kc_ref, vc_ref, bias_ref, o_ref, lse_ref, kp_ref, vp_ref):
    n = DIL_STEPS
    hd = DIL_HEADS * HEAD_DIM
    nt = (((1,), (1,)), ((), ()))
    lane = lax.broadcasted_iota(jnp.int32, (1, LANES), 1)
    low = lane < HEAD_DIM
    ones_low = jnp.where(low, 1, 0).astype(BF16) + jnp.zeros((2 * n, LANES), BF16)
    ones_high = jnp.where(low, 0, 1).astype(BF16) + jnp.zeros((2 * n, LANES), BF16)

    @pl.when(pl.program_id(2) == 0)
    def _():
        kp_ref[...] = jnp.zeros_like(kp_ref)
        vp_ref[...] = jnp.zeros_like(vp_ref)

    npairs = hd // LANES

    def scores(pr):
        cs = slice(pr * LANES, (pr + 1) * LANES)
        q2 = q_ref[:, cs] * jnp.asarray(HEAD_DIM ** -0.5, BF16)
        k2 = jnp.concatenate([kp_ref[:, cs], kc_ref[:, cs]], axis=0)
        qs = jnp.concatenate([jnp.where(low, q2, 0), jnp.where(low, 0, q2)], axis=0)
        bias = bias_ref[2 * pr:2 * pr + 2].reshape(2 * n, 2 * n)
        return lax.dot_general(qs, k2, nt, preferred_element_type=F32) + bias

    lse_tile = jnp.zeros((n, LANES), F32)
    s_next = scores(0)
    for pr in range(npairs):
        cs = slice(pr * LANES, (pr + 1) * LANES)
        s = s_next
        if pr + 1 < npairs:
            s_next = scores(pr + 1)
        v2 = jnp.concatenate([vp_ref[:, cs], vc_ref[:, cs]], axis=0)
        m = jnp.max(jnp.maximum(s[:, :n], s[:, n:]), axis=-1, keepdims=True)
        p = jnp.exp(s - m).astype(BF16)
        w = jnp.concatenate(
            [jnp.concatenate([jnp.where(low, v2, 0), ones_low], axis=1),
             jnp.concatenate([jnp.where(low, 0, v2), ones_high], axis=1)], axis=0)
        r = jnp.dot(jnp.concatenate([p[:n], p[n:]], axis=1), w, preferred_element_type=F32)
        den = r[:, LANES:]
        o_ref[:, cs] = (r[:, :LANES] / den).astype(o_ref.dtype)
        lse = jnp.where(low, m[:n], m[n:]) + jnp.log(den)
        lse_tile = jnp.where((lane == pr) | (lane == HEAD_DIM + pr), lse, lse_tile)
    lse_ref[...] = lse_tile
    kp_ref[...] = kc_ref[...]
    vp_ref[...] = vc_ref[...]


def dilated_attention(qkv, bias, dilation):
    b, r, length, _ = qkv.shape
    n = DIL_STEPS
    hd = DIL_HEADS * HEAD_DIM
    nb = length // n

    def cur(col):
        return pl.BlockSpec((None, None, n, hd), lambda bi, rho, i: (bi, rho, i, col))

    return pl.pallas_call(
        _dil_attn_body,
        grid=(b, r, nb),
        in_specs=[cur(0), cur(1), cur(2),
                  pl.BlockSpec((None, DIL_HEADS, n, 2 * n), lambda bi, rho, i: (jnp.minimum(i, 1), 0, 0, 0))],
        out_specs=[pl.BlockSpec((None, None, n, hd), lambda bi, rho, i: (bi, rho, i, 0)),
                   pl.BlockSpec((None, None, n, LANES), lambda bi, rho, i: (bi, rho, i, 0))],
        out_shape=[jax.ShapeDtypeStruct((b, r, length, hd), BF16),
                   jax.ShapeDtypeStruct((b, r, length, LANES), F32)],
        scratch_shapes=[pltpu.VMEM((n, hd), BF16), pltpu.VMEM((n, hd), BF16)],
        compiler_params=_cparams(3, VMEM_SMALL),
        name=f"dilated_attention_r{dilation}",
    )(qkv, qkv, qkv, bias)


def _dil_combine_body(o1_ref, o2_ref, o3_ref, l1_ref, l2_ref, l3_ref, e_ref, w_ref, h_ref, out_ref,
                      oslab_ref, lslab_ref):
    tm = out_ref.shape[0]
    nslab = oslab_ref.shape[1]

    def token_order(o_ref, l_ref, k):
        r = o_ref.shape[0]
        if r == 1:
            return o_ref[0].astype(F32), l_ref[0]
        per = tm // r
        for rho in range(r):
            blk = o_ref[rho].astype(F32)
            for c in range(nslab):
                oslab_ref[k, c, pl.ds(rho, per, stride=r), :] = blk[:, c * LANES:(c + 1) * LANES]
            lslab_ref[k, pl.ds(rho, per, stride=r), :] = l_ref[rho]
        return jnp.concatenate([oslab_ref[k, c] for c in range(nslab)], axis=1), lslab_ref[k]

    o1, l1 = token_order(o1_ref, l1_ref, 0)
    o2, l2 = token_order(o2_ref, l2_ref, 0)
    o3, l3 = token_order(o3_ref, l3_ref, 1)
    mx = jnp.maximum(jnp.maximum(l1, l2), l3)
    e1, e2, e3 = jnp.exp(l1 - mx), jnp.exp(l2 - mx), jnp.exp(l3 - mx)
    inv = 1.0 / (e1 + e2 + e3)

    mix = o3 + _spread(e1 * inv, e_ref) * (o1 - o3) + _spread(e2 * inv, e_ref) * (o2 - o3)
    out_ref[...] = h_ref[...] + jnp.dot(mix.astype(BF16), w_ref[...].astype(BF16), preferred_element_type=F32)


def _head_spread_matrix():
    hd = DIL_HEADS * HEAD_DIM
    lane = jnp.arange(LANES)[:, None]
    col = jnp.arange(hd)[None, :]
    head = col // HEAD_DIM
    src = jnp.where(head % 2 == 0, head // 2, HEAD_DIM + head // 2)
    return (lane == src).astype(BF16)


def dilated_combine(outs, lses, w_o, h, *, tm=512):
    b, s, d = h.shape
    hd = w_o.shape[0]

    def regrouped(arr):
        r = arr.shape[1]
        return pl.BlockSpec((None, r, tm // r, arr.shape[3]), lambda bi, i: (bi, 0, i, 0))

    return pl.pallas_call(
        _dil_combine_body,
        grid=(b, s // tm),
        in_specs=[regrouped(a) for a in (*outs, *lses)] + [
            pl.BlockSpec((2 * LANES, hd), lambda bi, i: (0, 0)),
            pl.BlockSpec(w_o.shape, lambda bi, i: (0, 0)),
            pl.BlockSpec((None, tm, d), lambda bi, i: (bi, i, 0))],
        out_specs=pl.BlockSpec((None, tm, d), lambda bi, i: (bi, i, 0)),
        out_shape=jax.ShapeDtypeStruct((b, s, d), F32),
        scratch_shapes=[pltpu.VMEM((2, hd // LANES, tm, LANES), F32), pltpu.VMEM((2, tm, LANES), F32)],
        compiler_params=_cparams(2, VMEM_COMBINE),
        name="dilated_combine",
    )(*outs, *lses, _stacked(_head_spread_matrix()), w_o, h)


def _diff_attn_body(qi_ref, ki_ref, q_ref, k_ref, v_ref, bias_ref, lam_ref, g_ref, o_ref,
                    m_ref, acc_ref, *, lambda_init):
    p = pl.program_id(1)
    qi = qi_ref[p]
    ki = ki_ref[p]
    nbatch, t, hw = q_ref.shape
    nt = (((1,), (1,)), ((), ()))
    low = lax.broadcasted_iota(jnp.int32, (1, hw), 1) < HEAD_DIM

    @pl.when(ki == 0)
    def _():
        m_ref[...] = jnp.full(m_ref.shape, MASK_VALUE, F32)
        acc_ref[...] = jnp.zeros_like(acc_ref)

    for bi in range(nbatch):
        q = q_ref[bi] * jnp.asarray(HEAD_DIM ** -0.5, BF16)
        k = k_ref[bi]
        v1 = jnp.concatenate([v_ref[bi], jnp.ones((t, hw), BF16)], axis=1)
        for mp in range(2):
            qm = jnp.where(low, q, 0) if mp == 0 else jnp.where(low, 0, q)
            s = lax.dot_general(qm, k, nt, preferred_element_type=F32) + bias_ref[mp]
            m_old = m_ref[bi, mp]
            m_new = jnp.maximum(m_old, jnp.max(s, axis=-1, keepdims=True))
            alpha = jnp.exp(m_old - m_new)
            pr = jnp.exp(s - jnp.concatenate([m_new] * (t // hw), axis=1))
            acc_ref[bi, mp] = (jnp.concatenate([alpha, alpha], axis=1) * acc_ref[bi, mp]
                               + jnp.dot(pr.astype(BF16), v1, preferred_element_type=F32))
            m_ref[bi, mp] = m_new

    @pl.when(ki == qi)
    def _():
        lam = (jnp.exp(jnp.sum(lam_ref[0:1, :] * lam_ref[1:2, :], axis=-1, keepdims=True))
               - jnp.exp(jnp.sum(lam_ref[2:3, :] * lam_ref[3:4, :], axis=-1, keepdims=True)) + lambda_init)
        gain = g_ref[...] * (1.0 - lambda_init)

        def finish(bi, carry):
            a0 = acc_ref[bi, 0]
            a1 = acc_ref[bi, 1]
            o = a0[:, :hw] / a0[:, hw:] - lam * (a1[:, :hw] / a1[:, hw:])
            o_ref[bi] = _rms(o, gain).astype(o_ref.dtype)
            return carry

        lax.fori_loop(0, nbatch, finish, 0)


def diff_attention(qkv, bias, lam_vecs, subln, lambda_init):
    b, s, _ = qkv.shape
    t = DIFF_TILE
    nq = s // t
    n_delta = bias.shape[2]
    pairs = [(qi, ki) for qi in range(nq) for ki in range(qi + 1)]
    qi_arr = jnp.asarray([pq for pq, _ in pairs], jnp.int32)
    ki_arr = jnp.asarray([pk for _, pk in pairs], jnp.int32)
    hw = 2 * HEAD_DIM
    nh = DIFF_HEADS
    grid_spec = pltpu.PrefetchScalarGridSpec(
        num_scalar_prefetch=2,
        grid=(nh, len(pairs)),
        in_specs=[
            pl.BlockSpec((b, t, hw), lambda h, p, qa, ka: (0, qa[p], h)),
            pl.BlockSpec((b, t, hw), lambda h, p, qa, ka: (0, ka[p], nh + h)),
            pl.BlockSpec((b, t, hw), lambda h, p, qa, ka: (0, ka[p], 2 * nh + h)),
            pl.BlockSpec((None, 2, None, t, t),
                         lambda h, p, qa, ka: (h, 0, jnp.minimum(qa[p] - ka[p], n_delta - 1), 0, 0)),
            pl.BlockSpec((4, HEAD_DIM), lambda h, p, qa, ka: (0, 0)),
            pl.BlockSpec((1, hw), lambda h, p, qa, ka: (0, 0)),
        ],
        out_specs=pl.BlockSpec((b, t, hw), lambda h, p, qa, ka: (0, qa[p], h)),
        scratch_shapes=[pltpu.VMEM((b, 2, t, hw), F32), pltpu.VMEM((b, 2, t, 2 * hw), F32)],
    )
    return pl.pallas_call(
        functools.partial(_diff_attn_body, lambda_init=lambda_init),
        grid_spec=grid_spec,
        out_shape=jax.ShapeDtypeStruct((b, s, nh * hw), BF16),
        compiler_params=_cparams(2, VMEM_DIFF_ATTENTION),
        name="diff_attention",
    )(qi_arr, ki_arr, qkv, qkv, qkv, bias, lam_vecs, subln.reshape(1, hw))


def _silu(x):
    h = 0.5 * x
    return h + h * jnp.tanh(h)


def _conv_silu(cur, win_ref, shift_ref, w_ref, b_ref, cols):
    l = cur.shape[0]
    lagged = jnp.dot(shift_ref[...], win_ref[:, cols], preferred_element_type=F32)
    acc = b_ref[:, cols] + w_ref[SSM_CONV - 1:SSM_CONV, cols] * cur
    for k in range(SSM_CONV - 1):
        acc = acc + w_ref[k:k + 1, cols] * lagged[k * l:(k + 1) * l]
    return _silu(acc)


def _shift_matrix(l):
    row = jnp.arange((SSM_CONV - 1) * l)[:, None]
    col = jnp.arange(2 * l)[None, :]
    k, t = row // l, row % l
    return (col == l + t - (SSM_CONV - 1 - k)).astype(BF16)


def _spread(w, e_ref):
    hi = w.astype(BF16)
    lo = (w - hi.astype(F32)).astype(BF16)
    return jnp.dot(jnp.concatenate([hi, lo], axis=1), e_ref[...], preferred_element_type=F32)


def _stacked(e):
    return jnp.concatenate([e, e], axis=0)


def _cumsum_rows(a):
    rows = a.shape[0]
    ridx = lax.broadcasted_iota(jnp.int32, a.shape, 0)
    sh = 1
    while sh < rows:
        a = a + jnp.where(ridx >= sh, pltpu.roll(a, sh, 0), 0.0)
        sh *= 2
    return a


def _ssd_body(z_ref, x_ref, bm_ref, cm_ref, dt_ref, cw_ref, cb_ref, dtb_ref, alog_ref, dsk_ref, gn_ref, e_ref,
              sh_ref, y_ref, state_ref, win_ref):
    c = pl.program_id(1)
    l = SSM_CHUNK
    ns = SSM_D_STATE
    di = SSM_D_INNER
    gn = SSM_GROUPS * ns
    hg = SSM_HEADS // SSM_GROUPS
    gw = hg * SSM_HEAD_DIM
    tail = 2 * SUBLANES

    @pl.when(c == 0)
    def _():
        state_ref[...] = jnp.zeros_like(state_ref)
        win_ref[0:l, :] = jnp.zeros((l, win_ref.shape[1]), BF16)

    win_ref[l:, 0:di] = x_ref[...].astype(BF16)
    win_ref[l:, di:di + gn] = bm_ref[...].astype(BF16)
    win_ref[l:, di + gn:] = cm_ref[...].astype(BF16)
    xs = [_conv_silu(x_ref[:, g * gw:(g + 1) * gw], win_ref, sh_ref, cw_ref, cb_ref, slice(g * gw, (g + 1) * gw))
          for g in range(SSM_GROUPS)]
    bmat = _conv_silu(bm_ref[...], win_ref, sh_ref, cw_ref, cb_ref, slice(di, di + gn)).astype(BF16)
    cmat = _conv_silu(cm_ref[...], win_ref, sh_ref, cw_ref, cb_ref, slice(di + gn, di + 2 * gn)).astype(BF16)
    win_ref[l - tail:l, :] = win_ref[2 * l - tail:, :]

    dt = jax.nn.softplus(dt_ref[...] + dtb_ref[...])
    a_cs = _cumsum_rows(dt * (-jnp.exp(alog_ref[...])))
    a_cs_t = a_cs.T
    last = a_cs[l - 1:l, :]
    dt_e = _spread(dt, e_ref)
    exp_a_e = _spread(jnp.exp(a_cs), e_ref)
    state_decay_e = _spread(jnp.exp(last - a_cs), e_ref)

    causal = (lax.broadcasted_iota(jnp.int32, (l, l), 0) >= lax.broadcasted_iota(jnp.int32, (l, l), 1))
    lane_head = lax.broadcasted_iota(jnp.int32, (1, gw), 1) // SSM_HEAD_DIM
    tn_dims = (((0,), (0,)), ((), ()))
    nt_dims = (((1,), (1,)), ((), ()))

    for g in range(SSM_GROUPS):
        gs = slice(g * gw, (g + 1) * gw)
        cg = cmat[:, g * ns:(g + 1) * ns]
        bg = bmat[:, g * ns:(g + 1) * ns]
        xg = xs[g]
        xdt = xg * dt_e[:, gs]
        xdt_b = xdt.astype(BF16)
        cb = lax.dot_general(cg, bg, nt_dims, preferred_element_type=F32)
        weights, operands = [], []
        for j in range(hg):
            hh = hg * g + j
            seg = a_cs[:, hh:hh + 1] - a_cs_t[hh:hh + 1, :]
            decay = jnp.where(causal, jnp.exp(seg), 0.0)
            weights.append((cb * decay).astype(BF16))
            operands.append(jnp.where(lane_head == j, xdt_b, 0))
        y_diag = jnp.dot(jnp.concatenate(weights, axis=1), jnp.concatenate(operands, axis=0),
                         preferred_element_type=F32)
        prev = state_ref[g]
        y_off = jnp.dot(cg, prev.astype(BF16), preferred_element_type=F32) * exp_a_e[:, gs]
        new = lax.dot_general(bg, (xdt * state_decay_e[:, gs]).astype(BF16), tn_dims,
                              preferred_element_type=F32)
        state_ref[g] = prev * exp_a_e[l - 1:l, gs] + new
        y = y_diag + y_off + xg * dsk_ref[:, gs]
        zg = z_ref[:, gs]
        y = y * _silu(zg)
        y_ref[:, gs] = _rms(y, gn_ref[:, gs]).astype(y_ref.dtype)


def ssd_core(zx, dt_raw, conv_w, conv_b, dt_bias, a_log, d_skip, gate_norm):
    b, s, _ = zx.shape
    l = SSM_CHUNK
    di = SSM_D_INNER
    gn = SSM_GROUPS * SSM_D_STATE
    conv_dim = di + 2 * gn
    z_spec = pl.BlockSpec((None, l, di), lambda bi, c: (bi, c, 0))
    x_spec = pl.BlockSpec((None, l, di), lambda bi, c: (bi, c, 1))
    b_spec = pl.BlockSpec((None, l, gn), lambda bi, c: (bi, c, 2 * di // gn))
    c_spec = pl.BlockSpec((None, l, gn), lambda bi, c: (bi, c, 2 * di // gn + 1))

    def whole(shape):
        return pl.BlockSpec(shape, lambda bi, c: (0, 0))

    head_lanes = (jnp.arange(LANES)[:, None] == jnp.arange(di)[None, :] // SSM_HEAD_DIM).astype(BF16)
    return pl.pallas_call(
        _ssd_body,
        grid=(b, s // l),
        in_specs=[z_spec, x_spec, b_spec, c_spec,
                  pl.BlockSpec((None, l, LANES), lambda bi, c: (bi, c, 0)),
                  whole((SSM_CONV, conv_dim)), whole((1, conv_dim)),
                  whole((1, LANES)), whole((1, LANES)), whole((1, di)), whole((1, di)), whole((2 * LANES, di)),
                  whole(((SSM_CONV - 1) * l, 2 * l))],
        out_specs=pl.BlockSpec((None, l, di), lambda bi, c: (bi, c, 0)),
        out_shape=jax.ShapeDtypeStruct((b, s, di), BF16),
        scratch_shapes=[pltpu.VMEM((SSM_GROUPS, SSM_D_STATE, (SSM_HEADS // SSM_GROUPS) * SSM_HEAD_DIM), F32),
                        pltpu.VMEM((2 * l, conv_dim), BF16)],
        compiler_params=_cparams(2, VMEM_SSD),
        name="ssd_core",
    )(zx, zx, zx, zx, dt_raw, conv_w, conv_b, dt_bias, a_log, d_skip, gate_norm, _stacked(head_lanes), _shift_matrix(l))


def _dilated_layer(h, norm_g, w_qkv, w_o, dil_biases):
    b, s, d = h.shape
    width = 3 * DIL_HEADS * HEAD_DIM
    outs, lses = [], []
    for g, (_, dilation) in enumerate(DIL_PATTERNS):
        qkv = norm_matmul(h, norm_g, w_qkv, n=width, tn=width, col_block=g, dilation=dilation)
        o, lse = dilated_attention(qkv, dil_biases[g], dilation)
        outs.append(o)
        lses.append(lse)
    return dilated_combine(outs, lses, w_o, h)


def _diff_layer(h, norm_g, w_qkv, lam_q1, lam_k1, lam_q2, lam_k2, subln, w_o, bias, lambda_init):
    b, s, d = h.shape
    width = w_qkv.shape[1]
    qkv = norm_matmul(h, norm_g, w_qkv, n=width, tn=width).reshape(b, s, width)
    lam_vecs = jnp.stack([lam_q1, lam_k1, lam_q2, lam_k2])
    o = diff_attention(qkv, bias, lam_vecs, subln, lambda_init)
    return matmul_residual(o.reshape(b * s, -1), w_o, h.reshape(b * s, d)).reshape(b, s, d)


def _ssd_layer(h, norm_g, w_in, conv_w, conv_b, dt_bias, a_log, d_skip, gate_norm, w_out):
    b, s, d = h.shape
    wide = SSM_D_INNER + conv_w.shape[1]
    pad = LANES - SSM_HEADS
    zx = norm_matmul(h, norm_g, w_in, n=wide, tn=wide // 2, out_dtype=F32).reshape(b, s, wide)
    w_dt = jnp.pad(w_in[:, wide:], ((0, 0), (0, pad)))
    dt_raw = norm_matmul(h, norm_g, w_dt, n=LANES, tn=LANES, out_dtype=F32).reshape(b, s, LANES)
    y = ssd_core(zx, dt_raw, conv_w, conv_b.reshape(1, -1),
                 jnp.pad(dt_bias, (0, pad)).reshape(1, LANES), jnp.pad(a_log, (0, pad)).reshape(1, LANES),
                 jnp.repeat(d_skip, SSM_HEAD_DIM).reshape(1, SSM_D_INNER), gate_norm.reshape(1, SSM_D_INNER))
    return matmul_residual(y.reshape(b * s, -1), w_out, h.reshape(b * s, d)).reshape(b, s, d)


def _diff_lambda_init(layer):
    return 0.8 - 0.6 * math.exp(-0.3 * layer)


def kernel(x, rel_bias,
           l0_mix_norm, l0_dil_w_qkv, l0_dil_w_o, l0_mlp_norm, l0_mlp_w_up, l0_mlp_w_down,
           l1_mix_norm, l1_diff_w_qkv, l1_diff_lam_q1, l1_diff_lam_k1, l1_diff_lam_q2, l1_diff_lam_k2,
           l1_diff_subln, l1_diff_w_o, l1_mlp_norm, l1_mlp_w_up, l1_mlp_w_down,
           l2_mix_norm, l2_ssm_w_in, l2_ssm_conv_w, l2_ssm_conv_b, l2_ssm_dt_bias, l2_ssm_A_log, l2_ssm_D,
           l2_ssm_gate_norm, l2_ssm_w_out, l2_mlp_norm, l2_mlp_w_up, l2_mlp_w_down,
           l3_mix_norm, l3_dil_w_qkv, l3_dil_w_o, l3_mlp_norm, l3_mlp_w_up, l3_mlp_w_down,
           final_norm):
    b, s, d = x.shape
    dil_biases = [dilated_bias(rel_bias, dilation) for _, dilation in DIL_PATTERNS]
    dif_bias = diff_bias(rel_bias, s)

    def mlp(h, g, w_up, w_down, final_gain=None):
        return mlp_block(h.reshape(b * s, d), g, w_up, w_down, final_gain).reshape(b, s, d)

    h = _dilated_layer(x, l0_mix_norm, l0_dil_w_qkv, l0_dil_w_o, dil_biases)
    h = mlp(h, l0_mlp_norm, l0_mlp_w_up, l0_mlp_w_down)
    h = _diff_layer(h, l1_mix_norm, l1_diff_w_qkv, l1_diff_lam_q1, l1_diff_lam_k1, l1_diff_lam_q2,
                    l1_diff_lam_k2, l1_diff_subln, l1_diff_w_o, dif_bias, _diff_lambda_init(1))
    h = mlp(h, l1_mlp_norm, l1_mlp_w_up, l1_mlp_w_down)
    h = _ssd_layer(h, l2_mix_norm, l2_ssm_w_in, l2_ssm_conv_w, l2_ssm_conv_b, l2_ssm_dt_bias, l2_ssm_A_log,
                   l2_ssm_D, l2_ssm_gate_norm, l2_ssm_w_out)
    h = mlp(h, l2_mlp_norm, l2_mlp_w_up, l2_mlp_w_down)
    h = _dilated_layer(h, l3_mix_norm, l3_dil_w_qkv, l3_dil_w_o, dil_biases)
    return mlp(h, l3_mlp_norm, l3_mlp_w_up, l3_mlp_w_down, final_norm)
```

```python
import functools
import math

import jax
import jax.numpy as jnp
from jax import lax
from jax.experimental import pallas as pl
from jax.experimental.pallas import tpu as pltpu

F32 = jnp.float32
BF16 = jnp.bfloat16

NORM_EPS = 1e-5
HEAD_DIM = 64
DIL_PATTERNS = ((128, 1), (512, 4), (2048, 16))
DIL_HEADS = 16
DIL_STEPS = 128
DIFF_HEADS = 8
N_BUCKETS = 32
MAX_DISTANCE = 2048
SSM_HEADS = 32
SSM_GROUPS = 8
SSM_D_STATE = 128
SSM_HEAD_DIM = 64
SSM_D_INNER = SSM_HEADS * SSM_HEAD_DIM
SSM_CONV = 4
SSM_CHUNK = 128
LANES = 128
SUBLANES = 8
MASK_VALUE = -1e30
MIB = 1024 * 1024
VMEM_PROJECTION = 34 * MIB
VMEM_MLP = 52 * MIB
VMEM_RESIDUAL = 36 * MIB
VMEM_COMBINE = 40 * MIB
VMEM_DIFF_ATTENTION = 36 * MIB
VMEM_SSD = 20 * MIB
VMEM_DIFF_BIAS = 14 * MIB
VMEM_DIL_ATTENTION = 8 * MIB
VMEM_SMALL = 6 * MIB

DIFF_TILE = 512
DIFF_HEADS_PER_STEP = 2


def _bucket_thresholds():
    exact = N_BUCKETS // 2

    def bucket(d):
        if d < exact:
            return d
        big = exact + int(math.log(d / exact) / math.log(MAX_DISTANCE / exact) * (N_BUCKETS - exact))
        return min(big, N_BUCKETS - 1)

    thr = {}
    d = 0
    for b in range(1, N_BUCKETS):
        while bucket(d) < b:
            d += 1
        thr[b] = d
    return thr, bucket


_BUCKET_THR, _bucket_of = _bucket_thresholds()


def _cparams(grid_rank, vmem_bytes):
    return pltpu.CompilerParams(dimension_semantics=("arbitrary",) * grid_rank, vmem_limit_bytes=vmem_bytes)


def _rms(x, gain):
    ms = jnp.mean(x * x, axis=-1, keepdims=True)
    return x * lax.rsqrt(ms + NORM_EPS) * gain


def _norm_matmul_body(x_ref, g_ref, w_ref, o_ref, *slab_ref, r):
    rows, d = x_ref.shape
    if r == 1:
        x = x_ref[...]
    else:
        slab = slab_ref[0]
        per = rows // r
        for c in range(d // LANES):
            slab[c] = x_ref[:, c * LANES:(c + 1) * LANES]
        x = jnp.concatenate(
            [jnp.concatenate([slab[c, pl.ds(rho, per, stride=r), :] for c in range(d // LANES)], axis=1)
             for rho in range(r)], axis=0)
    xn = _rms(x, g_ref[...]).astype(BF16)
    res = jnp.dot(xn, w_ref[...].astype(BF16), preferred_element_type=F32).astype(o_ref.dtype)
    o_ref[...] = res.reshape(o_ref.shape)


def norm_matmul(x, gain, w, *, n, tn, col_block=0, dilation=1, rows=512, out_dtype=BF16):
    b, s, d = x.shape
    r = dilation
    per = rows // r
    scratch = [pltpu.VMEM((d // LANES, rows, LANES), F32)] if r > 1 else []
    return pl.pallas_call(
        functools.partial(_norm_matmul_body, r=r),
        grid=(n // tn, b, s // rows),
        in_specs=[
            pl.BlockSpec((None, rows, d), lambda j, bi, i: (bi, i, 0)),
            pl.BlockSpec((1, d), lambda j, bi, i: (0, 0)),
            pl.BlockSpec((d, tn), lambda j, bi, i: (0, col_block + j), pipeline_mode=pl.Buffered(1)),
        ],
        out_specs=pl.BlockSpec((None, r, per, tn), lambda j, bi, i: (bi, 0, i, j)),
        out_shape=jax.ShapeDtypeStruct((b, r, s // r, n), out_dtype),
        scratch_shapes=scratch,
        compiler_params=_cparams(3, VMEM_PROJECTION),
        name="norm_matmul",
    )(x, gain.reshape(1, d), w)


def _matmul_res_body(a_ref, w_ref, r_ref, o_ref):
    o_ref[...] = r_ref[...] + jnp.dot(a_ref[...], w_ref[...].astype(BF16), preferred_element_type=F32)


def matmul_residual(a, w, res, *, tm=512):
    m, k = a.shape
    n = w.shape[1]
    return pl.pallas_call(
        _matmul_res_body,
        grid=(m // tm,),
        in_specs=[
            pl.BlockSpec((tm, k), lambda i: (i, 0)),
            pl.BlockSpec((k, n), lambda i: (0, 0)),
            pl.BlockSpec((tm, n), lambda i: (i, 0)),
        ],
        out_specs=pl.BlockSpec((tm, n), lambda i: (i, 0)),
        out_shape=jax.ShapeDtypeStruct((m, n), F32),
        compiler_params=_cparams(1, VMEM_RESIDUAL),
        name="matmul_residual",
    )(a, w, res)


def _mlp_body(h_ref, g_ref, wu_ref, wd_ref, fg_ref, o_ref, *, final_norm):
    f = pl.program_id(1)
    last = pl.num_programs(1) - 1

    def contribution():
        xn = _rms(h_ref[...], g_ref[...]).astype(BF16)
        u = jnp.maximum(jnp.dot(xn, wu_ref[...].astype(BF16), preferred_element_type=F32), 0.0)
        return jnp.dot((u * u).astype(BF16), wd_ref[...].astype(BF16), preferred_element_type=F32)

    @pl.when(f == 0)
    def _():
        o_ref[...] = h_ref[...] + contribution()

    @pl.when((f > 0) & (f < last))
    def _():
        o_ref[...] += contribution()

    @pl.when(f == last)
    def _():
        out = o_ref[...] + contribution()
        if final_norm:
            out = _rms(out, fg_ref[...])
        o_ref[...] = out


def mlp_block(h, gain, w_up, w_down, final_gain=None, *, tm=1024, tf=1024):
    m, d = h.shape
    f = w_up.shape[1]
    assert f // tf >= 2
    final_norm = final_gain is not None
    fg = (final_gain if final_norm else gain).reshape(1, d)
    return pl.pallas_call(
        functools.partial(_mlp_body, final_norm=final_norm),
        grid=(m // tm, f // tf),
        in_specs=[
            pl.BlockSpec((tm, d), lambda i, j: (i, 0)),
            pl.BlockSpec((1, d), lambda i, j: (0, 0)),
            pl.BlockSpec((d, tf), lambda i, j: (0, j)),
            pl.BlockSpec((tf, d), lambda i, j: (j, 0)),
            pl.BlockSpec((1, d), lambda i, j: (0, 0)),
        ],
        out_specs=pl.BlockSpec((tm, d), lambda i, j: (i, 0)),
        out_shape=jax.ShapeDtypeStruct((m, d), F32),
        compiler_params=_cparams(2, VMEM_MLP),
        name="mlp_block",
    )(h, gain.reshape(1, d), w_up, w_down, fg)


def _bias_from_dist(dist, col_of_bucket, d_lo, d_hi):
    val = jnp.full(dist.shape, col_of_bucket(_bucket_of(d_lo)), F32)
    for b in range(1, N_BUCKETS):
        t = _BUCKET_THR[b]
        if d_lo < t <= d_hi:
            val = jnp.where(dist >= t, col_of_bucket(b), val)
    return val


def _dil_bias_body(tab_ref, o_ref, *, dilation):
    c = pl.program_id(0)
    n = DIL_STEPS
    qi = lax.broadcasted_iota(jnp.int32, (n, 2 * n), 0)
    kj = lax.broadcasted_iota(jnp.int32, (n, 2 * n), 1)
    steps = n + qi - kj
    band = (steps >= 0) & (steps <= n)
    dist = jnp.clip(steps, 0, n) * dilation
    val = _bias_from_dist(dist, lambda b: tab_ref[b, c], 0, n * dilation)
    later = jnp.where(band, val, MASK_VALUE)
    o_ref[1] = later
    o_ref[0] = jnp.where(kj >= n, later, MASK_VALUE)


def dilated_bias(rel_bias, dilation):
    n = DIL_STEPS
    return pl.pallas_call(
        functools.partial(_dil_bias_body, dilation=dilation),
        grid=(DIL_HEADS,),
        in_specs=[pl.BlockSpec(memory_space=pltpu.SMEM)],
        out_specs=pl.BlockSpec((2, None, n, 2 * n), lambda c: (0, c, 0, 0)),
        out_shape=jax.ShapeDtypeStruct((2, DIL_HEADS, n, 2 * n), F32),
        compiler_params=_cparams(1, VMEM_SMALL),
        name="dilated_bias",
    )(rel_bias)


def _diff_bias_body(tab_ref, o_ref, *, n_delta):
    m = pl.program_id(0)
    h = pl.program_id(1)
    c = m * DIFF_HEADS + h
    t = DIFF_TILE
    qi = lax.broadcasted_iota(jnp.int32, (t, t), 0)
    kj = lax.broadcasted_iota(jnp.int32, (t, t), 1)
    for delta in range(n_delta):
        dist = delta * t + qi - kj
        d_lo = max(delta * t - (t - 1), 0)
        d_hi = delta * t + t - 1
        val = _bias_from_dist(dist, lambda b: tab_ref[b, c], d_lo, d_hi)
        if delta == 0:
            val = jnp.where(dist >= 0, val, MASK_VALUE)
        o_ref[delta] = val


def _diff_first_constant_delta():
    t = DIFF_TILE
    last_thr = _BUCKET_THR[N_BUCKETS - 1]
    delta = 1
    while delta * t - (t - 1) < last_thr:
        delta += 1
    return delta


def _diff_n_delta(s):
    return min(_diff_first_constant_delta() + 1, s // DIFF_TILE)


def diff_bias(rel_bias, s):
    t = DIFF_TILE
    n_delta = _diff_n_delta(s)
    return pl.pallas_call(
        functools.partial(_diff_bias_body, n_delta=n_delta),
        grid=(2, DIFF_HEADS),
        in_specs=[pl.BlockSpec(memory_space=pltpu.SMEM)],
        out_specs=pl.BlockSpec((None, None, n_delta, t, t), lambda m, h: (h, m, 0, 0, 0)),
        out_shape=jax.ShapeDtypeStruct((DIFF_HEADS, 2, n_delta, t, t), F32),
        compiler_params=_cparams(2, VMEM_DIFF_BIAS),
        name="diff_bias",
    )(rel_bias)


def _dil_attn_body(q_ref, kc_ref, vc_ref, bias_ref, o_ref, lse_ref, kp_ref, vp_ref):
    n = DIL_STEPS
    hd = DIL_HEADS * HEAD_DIM
    nt = (((1,), (1,)), ((), ()))
    lane = lax.broadcasted_iota(jnp.int32, (1, LANES), 1)
    low = lane < HEAD_DIM
    ones_low = jnp.where(low, 1, 0).astype(BF16) + jnp.zeros((2 * n, LANES), BF16)
    ones_high = jnp.where(low, 0, 1).astype(BF16) + jnp.zeros((2 * n, LANES), BF16)

    @pl.when(pl.program_id(2) == 0)
    def _():
        kp_ref[...] = jnp.zeros_like(kp_ref)
        vp_ref[...] = jnp.zeros_like(vp_ref)

    npairs = hd // LANES
    first_variant = jnp.minimum(pl.program_id(2), 1)

    def keys_values(blk, cs):
        if blk == 0:
            return (jnp.concatenate([kp_ref[:, cs], kc_ref[0:n, cs]], axis=0),
                    jnp.concatenate([vp_ref[:, cs], vc_ref[0:n, cs]], axis=0))
        return kc_ref[:, cs], vc_ref[:, cs]

    lse_tiles = [jnp.zeros((n, LANES), F32), jnp.zeros((n, LANES), F32)]
    for pr in range(npairs):
        cs = slice(pr * LANES, (pr + 1) * LANES)
        for blk in range(2):
            rows = slice(blk * n, (blk + 1) * n)
            k2, v2 = keys_values(blk, cs)
            q2 = q_ref[rows, cs] * jnp.asarray(HEAD_DIM ** -0.5, BF16)
            qs = jnp.concatenate([jnp.where(low, q2, 0), jnp.where(low, 0, q2)], axis=0)
            variant = first_variant if blk == 0 else 1
            bias = bias_ref[variant, 2 * pr:2 * pr + 2].reshape(2 * n, 2 * n)
            s = lax.dot_general(qs, k2, nt, preferred_element_type=F32) + bias
            m = jnp.max(jnp.maximum(s[:, :n], s[:, n:]), axis=-1, keepdims=True)
            p = jnp.exp(s - m).astype(BF16)
            w = jnp.concatenate(
                [jnp.concatenate([jnp.where(low, v2, 0), ones_low], axis=1),
                 jnp.concatenate([jnp.where(low, 0, v2), ones_high], axis=1)], axis=0)
            r = jnp.dot(jnp.concatenate([p[:n], p[n:]], axis=1), w, preferred_element_type=F32)
            den = r[:, LANES:]
            o_ref[rows, cs] = (r[:, :LANES] / den).astype(o_ref.dtype)
            lse = jnp.where(low, m[:n], m[n:]) + jnp.log(den)
            lse_tiles[blk] = jnp.where((lane == pr) | (lane == HEAD_DIM + pr), lse, lse_tiles[blk])
    lse_ref[0:n, :] = lse_tiles[0]
    lse_ref[n:, :] = lse_tiles[1]
    kp_ref[...] = kc_ref[n:, :]
    vp_ref[...] = vc_ref[n:, :]


def dilated_attention(qkv, bias, dilation):
    b, r, length, _ = qkv.shape
    n = DIL_STEPS
    hd = DIL_HEADS * HEAD_DIM
    rows = 2 * n
    assert length % rows == 0

    def cur(col):
        return pl.BlockSpec((None, None, rows, hd), lambda bi, rho, i: (bi, rho, i, col))

    return pl.pallas_call(
        _dil_attn_body,
        grid=(b, r, length // rows),
        in_specs=[cur(0), cur(1), cur(2),
                  pl.BlockSpec((2, DIL_HEADS, n, 2 * n), lambda bi, rho, i: (0, 0, 0, 0))],
        out_specs=[pl.BlockSpec((None, None, rows, hd), lambda bi, rho, i: (bi, rho, i, 0)),
                   pl.BlockSpec((None, None, rows, LANES), lambda bi, rho, i: (bi, rho, i, 0))],
        out_shape=[jax.ShapeDtypeStruct((b, r, length, hd), BF16),
                   jax.ShapeDtypeStruct((b, r, length, LANES), F32)],
        scratch_shapes=[pltpu.VMEM((n, hd), BF16), pltpu.VMEM((n, hd), BF16)],
        compiler_params=_cparams(3, VMEM_DIL_ATTENTION),
        name=f"dilated_attention_r{dilation}",
    )(qkv, qkv, qkv, bias)


def _dil_combine_body(o1_ref, o2_ref, o3_ref, l1_ref, l2_ref, l3_ref, e_ref, w_ref, h_ref, out_ref,
                      oslab_ref, lslab_ref):
    tm = out_ref.shape[0]
    nslab = oslab_ref.shape[1]

    def token_order(o_ref, l_ref, k):
        r = o_ref.shape[0]
        if r == 1:
            return o_ref[0].astype(F32), l_ref[0]
        per = tm // r
        for rho in range(r):
            blk = o_ref[rho].astype(F32)
            for c in range(nslab):
                oslab_ref[k, c, pl.ds(rho, per, stride=r), :] = blk[:, c * LANES:(c + 1) * LANES]
            lslab_ref[k, pl.ds(rho, per, stride=r), :] = l_ref[rho]
        return jnp.concatenate([oslab_ref[k, c] for c in range(nslab)], axis=1), lslab_ref[k]

    o1, l1 = token_order(o1_ref, l1_ref, 0)
    o2, l2 = token_order(o2_ref, l2_ref, 0)
    o3, l3 = token_order(o3_ref, l3_ref, 1)
    mx = jnp.maximum(jnp.maximum(l1, l2), l3)
    e1, e2, e3 = jnp.exp(l1 - mx), jnp.exp(l2 - mx), jnp.exp(l3 - mx)
    inv = 1.0 / (e1 + e2 + e3)

    mix = o3 + _spread(e1 * inv, e_ref) * (o1 - o3) + _spread(e2 * inv, e_ref) * (o2 - o3)
    out_ref[...] = h_ref[...] + jnp.dot(mix.astype(BF16), w_ref[...].astype(BF16), preferred_element_type=F32)


def _head_spread_matrix():
    hd = DIL_HEADS * HEAD_DIM
    lane = jnp.arange(LANES)[:, None]
    col = jnp.arange(hd)[None, :]
    head = col // HEAD_DIM
    src = jnp.where(head % 2 == 0, head // 2, HEAD_DIM + head // 2)
    return (lane == src).astype(BF16)


def dilated_combine(outs, lses, w_o, h, *, tm=512):
    b, s, d = h.shape
    hd = w_o.shape[0]

    def regrouped(arr):
        r = arr.shape[1]
        return pl.BlockSpec((None, r, tm // r, arr.shape[3]), lambda bi, i: (bi, 0, i, 0))

    return pl.pallas_call(
        _dil_combine_body,
        grid=(b, s // tm),
        in_specs=[regrouped(a) for a in (*outs, *lses)] + [
            pl.BlockSpec((2 * LANES, hd), lambda bi, i: (0, 0)),
            pl.BlockSpec(w_o.shape, lambda bi, i: (0, 0)),
            pl.BlockSpec((None, tm, d), lambda bi, i: (bi, i, 0))],
        out_specs=pl.BlockSpec((None, tm, d), lambda bi, i: (bi, i, 0)),
        out_shape=jax.ShapeDtypeStruct((b, s, d), F32),
        scratch_shapes=[pltpu.VMEM((2, hd // LANES, tm, LANES), F32), pltpu.VMEM((2, tm, LANES), F32)],
        compiler_params=_cparams(2, VMEM_COMBINE),
        name="dilated_combine",
    )(*outs, *lses, _stacked(_head_spread_matrix()), w_o, h)


def _diff_attn_body(qi_ref, ki_ref, q_ref, k_ref, v_ref, bias_ref, lam_ref, g_ref, o_ref,
                    m_ref, acc_ref, *, lambda_init):
    p = pl.program_id(1)
    qi = qi_ref[p]
    ki = ki_ref[p]
    nbatch, t, _ = q_ref.shape
    hw = 2 * HEAD_DIM
    nheads = q_ref.shape[2] // hw
    nt = (((1,), (1,)), ((), ()))
    low = lax.broadcasted_iota(jnp.int32, (1, hw), 1) < HEAD_DIM

    @pl.when(ki == 0)
    def _():
        m_ref[...] = jnp.full(m_ref.shape, MASK_VALUE, F32)
        acc_ref[...] = jnp.zeros_like(acc_ref)

    for bi in range(nbatch):
        for hh in range(nheads):
            hs = slice(hh * hw, (hh + 1) * hw)
            q = q_ref[bi, :, hs] * jnp.asarray(HEAD_DIM ** -0.5, BF16)
            k = k_ref[bi, :, hs]
            v1 = jnp.concatenate([v_ref[bi, :, hs], jnp.ones((t, hw), BF16)], axis=1)
            for mp in range(2):
                qm = jnp.where(low, q, 0) if mp == 0 else jnp.where(low, 0, q)
                s = lax.dot_general(qm, k, nt, preferred_element_type=F32) + bias_ref[hh, mp]
                m_old = m_ref[bi, hh, mp]
                m_new = jnp.maximum(m_old, jnp.max(s, axis=-1, keepdims=True))
                alpha = jnp.exp(m_old - m_new)
                pr = jnp.exp(s - jnp.concatenate([m_new] * (t // hw), axis=1))
                acc_ref[bi, hh, mp] = (jnp.concatenate([alpha, alpha], axis=1) * acc_ref[bi, hh, mp]
                                       + jnp.dot(pr.astype(BF16), v1, preferred_element_type=F32))
                m_ref[bi, hh, mp] = m_new

    @pl.when(ki == qi)
    def _():
        lam = (jnp.exp(jnp.sum(lam_ref[0:1, :] * lam_ref[1:2, :], axis=-1, keepdims=True))
               - jnp.exp(jnp.sum(lam_ref[2:3, :] * lam_ref[3:4, :], axis=-1, keepdims=True)) + lambda_init)
        gain = g_ref[...] * (1.0 - lambda_init)

        def finish(bi, carry):
            for hh in range(nheads):
                a0 = acc_ref[bi, hh, 0]
                a1 = acc_ref[bi, hh, 1]
                o = a0[:, :hw] / a0[:, hw:] - lam * (a1[:, :hw] / a1[:, hw:])
                o_ref[bi, :, hh * hw:(hh + 1) * hw] = _rms(o, gain).astype(o_ref.dtype)
            return carry

        lax.fori_loop(0, nbatch, finish, 0)


def diff_attention(qkv, bias, lam_vecs, subln, lambda_init):
    b, s, _ = qkv.shape
    t = DIFF_TILE
    nq = s // t
    n_delta = bias.shape[2]
    pairs = [(qi, ki) for qi in range(nq) for ki in range(qi + 1)]
    qi_arr = jnp.asarray([pq for pq, _ in pairs], jnp.int32)
    ki_arr = jnp.asarray([pk for _, pk in pairs], jnp.int32)
    hw = 2 * HEAD_DIM
    nh = DIFF_HEADS
    hps = DIFF_HEADS_PER_STEP
    ng = nh // hps
    gw = hps * hw
    grid_spec = pltpu.PrefetchScalarGridSpec(
        num_scalar_prefetch=2,
        grid=(ng, len(pairs)),
        in_specs=[
            pl.BlockSpec((b, t, gw), lambda h, p, qa, ka: (0, qa[p], h)),
            pl.BlockSpec((b, t, gw), lambda h, p, qa, ka: (0, ka[p], ng + h)),
            pl.BlockSpec((b, t, gw), lambda h, p, qa, ka: (0, ka[p], 2 * ng + h)),
            pl.BlockSpec((hps, 2, None, t, t),
                         lambda h, p, qa, ka: (h, 0, jnp.minimum(qa[p] - ka[p], n_delta - 1), 0, 0)),
            pl.BlockSpec((4, HEAD_DIM), lambda h, p, qa, ka: (0, 0)),
            pl.BlockSpec((1, hw), lambda h, p, qa, ka: (0, 0)),
        ],
        out_specs=pl.BlockSpec((b, t, gw), lambda h, p, qa, ka: (0, qa[p], h)),
        scratch_shapes=[pltpu.VMEM((b, hps, 2, t, hw), F32), pltpu.VMEM((b, hps, 2, t, 2 * hw), F32)],
    )
    return pl.pallas_call(
        functools.partial(_diff_attn_body, lambda_init=lambda_init),
        grid_spec=grid_spec,
        out_shape=jax.ShapeDtypeStruct((b, s, nh * hw), BF16),
        compiler_params=_cparams(2, VMEM_DIFF_ATTENTION),
        name="diff_attention",
    )(qi_arr, ki_arr, qkv, qkv, qkv, bias, lam_vecs, subln.reshape(1, hw))


def _silu(x):
    h = 0.5 * x
    return h + h * jnp.tanh(h)


def _conv_silu(cur, win_ref, shift_ref, w_ref, b_ref, cols):
    l = cur.shape[0]
    lagged = jnp.dot(shift_ref[...], win_ref[:, cols], preferred_element_type=F32)
    acc = b_ref[:, cols] + w_ref[SSM_CONV - 1:SSM_CONV, cols] * cur
    for k in range(SSM_CONV - 1):
        acc = acc + w_ref[k:k + 1, cols] * lagged[k * l:(k + 1) * l]
    return _silu(acc)


def _shift_matrix(l):
    row = jnp.arange((SSM_CONV - 1) * l)[:, None]
    col = jnp.arange(2 * l)[None, :]
    k, t = row // l, row % l
    return (col == l + t - (SSM_CONV - 1 - k)).astype(BF16)


def _spread(w, e_ref):
    hi = w.astype(BF16)
    lo = (w - hi.astype(F32)).astype(BF16)
    return jnp.dot(jnp.concatenate([hi, lo], axis=1), e_ref[...], preferred_element_type=F32)


def _stacked(e):
    return jnp.concatenate([e, e], axis=0)


def _cumsum_rows(a):
    rows = a.shape[0]
    ridx = lax.broadcasted_iota(jnp.int32, a.shape, 0)
    sh = 1
    while sh < rows:
        a = a + jnp.where(ridx >= sh, pltpu.roll(a, sh, 0), 0.0)
        sh *= 2
    return a


def _ssd_body(z_ref, x_ref, bm_ref, cm_ref, dt_ref, cw_ref, cb_ref, dtb_ref, alog_ref, dsk_ref, gn_ref, e_ref,
              sh_ref, y_ref, state_ref, win_ref):
    c = pl.program_id(1)
    l = SSM_CHUNK
    ns = SSM_D_STATE
    di = SSM_D_INNER
    gn = SSM_GROUPS * ns
    hg = SSM_HEADS // SSM_GROUPS
    gw = hg * SSM_HEAD_DIM
    tail = 2 * SUBLANES

    @pl.when(c == 0)
    def _():
        state_ref[...] = jnp.zeros_like(state_ref)
        win_ref[0:l, :] = jnp.zeros((l, win_ref.shape[1]), BF16)

    win_ref[l:, 0:di] = x_ref[...].astype(BF16)
    win_ref[l:, di:di + gn] = bm_ref[...].astype(BF16)
    win_ref[l:, di + gn:] = cm_ref[...].astype(BF16)
    xs = [_conv_silu(x_ref[:, g * gw:(g + 1) * gw], win_ref, sh_ref, cw_ref, cb_ref, slice(g * gw, (g + 1) * gw))
          for g in range(SSM_GROUPS)]
    bmat = _conv_silu(bm_ref[...], win_ref, sh_ref, cw_ref, cb_ref, slice(di, di + gn)).astype(BF16)
    cmat = _conv_silu(cm_ref[...], win_ref, sh_ref, cw_ref, cb_ref, slice(di + gn, di + 2 * gn)).astype(BF16)
    win_ref[l - tail:l, :] = win_ref[2 * l - tail:, :]

    dt = jax.nn.softplus(dt_ref[...] + dtb_ref[...])
    a_cs = _cumsum_rows(dt * (-jnp.exp(alog_ref[...])))
    a_cs_t = a_cs.T
    last = a_cs[l - 1:l, :]
    dt_e = _spread(dt, e_ref)
    exp_a_e = _spread(jnp.exp(a_cs), e_ref)
    state_decay_e = _spread(jnp.exp(last - a_cs), e_ref)

    causal = (lax.broadcasted_iota(jnp.int32, (l, l), 0) >= lax.broadcasted_iota(jnp.int32, (l, l), 1))
    lane_head = lax.broadcasted_iota(jnp.int32, (1, gw), 1) // SSM_HEAD_DIM
    tn_dims = (((0,), (0,)), ((), ()))
    nt_dims = (((1,), (1,)), ((), ()))

    for g in range(SSM_GROUPS):
        gs = slice(g * gw, (g + 1) * gw)
        cg = cmat[:, g * ns:(g + 1) * ns]
        bg = bmat[:, g * ns:(g + 1) * ns]
        xg = xs[g]
        xdt = xg * dt_e[:, gs]
        xdt_b = xdt.astype(BF16)
        cb = lax.dot_general(cg, bg, nt_dims, preferred_element_type=F32)
        weights, operands = [], []
        for j in range(hg):
            hh = hg * g + j
            seg = a_cs[:, hh:hh + 1] - a_cs_t[hh:hh + 1, :]
            decay = jnp.where(causal, jnp.exp(seg), 0.0)
            weights.append((cb * decay).astype(BF16))
            operands.append(jnp.where(lane_head == j, xdt_b, 0))
        y_diag = jnp.dot(jnp.concatenate(weights, axis=1), jnp.concatenate(operands, axis=0),
                         preferred_element_type=F32)
        prev = state_ref[g]
        y_off = jnp.dot(cg, prev.astype(BF16), preferred_element_type=F32) * exp_a_e[:, gs]
        new = lax.dot_general(bg, (xdt * state_decay_e[:, gs]).astype(BF16), tn_dims,
                              preferred_element_type=F32)
        state_ref[g] = prev * exp_a_e[l - 1:l, gs] + new
        y = y_diag + y_off + xg * dsk_ref[:, gs]
        zg = z_ref[:, gs]
        y = y * _silu(zg)
        y_ref[:, gs] = _rms(y, gn_ref[:, gs]).astype(y_ref.dtype)


def ssd_core(zx, dt_raw, conv_w, conv_b, dt_bias, a_log, d_skip, gate_norm):
    b, s, _ = zx.shape
    l = SSM_CHUNK
    di = SSM_D_INNER
    gn = SSM_GROUPS * SSM_D_STATE
    conv_dim = di + 2 * gn
    z_spec = pl.BlockSpec((None, l, di), lambda bi, c: (bi, c, 0))
    x_spec = pl.BlockSpec((None, l, di), lambda bi, c: (bi, c, 1))
    b_spec = pl.BlockSpec((None, l, gn), lambda bi, c: (bi, c, 2 * di // gn))
    c_spec = pl.BlockSpec((None, l, gn), lambda bi, c: (bi, c, 2 * di // gn + 1))

    def whole(shape):
        return pl.BlockSpec(shape, lambda bi, c: (0, 0))

    head_lanes = (jnp.arange(LANES)[:, None] == jnp.arange(di)[None, :] // SSM_HEAD_DIM).astype(BF16)
    return pl.pallas_call(
        _ssd_body,
        grid=(b, s // l),
        in_specs=[z_spec, x_spec, b_spec, c_spec,
                  pl.BlockSpec((None, l, LANES), lambda bi, c: (bi, c, 0)),
                  whole((SSM_CONV, conv_dim)), whole((1, conv_dim)),
                  whole((1, LANES)), whole((1, LANES)), whole((1, di)), whole((1, di)), whole((2 * LANES, di)),
                  whole(((SSM_CONV - 1) * l, 2 * l))],
        out_specs=pl.BlockSpec((None, l, di), lambda bi, c: (bi, c, 0)),
        out_shape=jax.ShapeDtypeStruct((b, s, di), BF16),
        scratch_shapes=[pltpu.VMEM((SSM_GROUPS, SSM_D_STATE, (SSM_HEADS // SSM_GROUPS) * SSM_HEAD_DIM), F32),
                        pltpu.VMEM((2 * l, conv_dim), BF16)],
        compiler_params=_cparams(2, VMEM_SSD),
        name="ssd_core",
    )(zx, zx, zx, zx, dt_raw, conv_w, conv_b, dt_bias, a_log, d_skip, gate_norm, _stacked(head_lanes), _shift_matrix(l))


def _dilated_layer(h, norm_g, w_qkv, w_o, dil_biases):
    b, s, d = h.shape
    width = 3 * DIL_HEADS * HEAD_DIM
    outs, lses = [], []
    for g, (_, dilation) in enumerate(DIL_PATTERNS):
        qkv = norm_matmul(h, norm_g, w_qkv, n=width, tn=width, col_block=g, dilation=dilation)
        o, lse = dilated_attention(qkv, dil_biases[g], dilation)
        outs.append(o)
        lses.append(lse)
    return dilated_combine(outs, lses, w_o, h)


def _diff_layer(h, norm_g, w_qkv, lam_q1, lam_k1, lam_q2, lam_k2, subln, w_o, bias, lambda_init):
    b, s, d = h.shape
    width = w_qkv.shape[1]
    qkv = norm_matmul(h, norm_g, w_qkv, n=width, tn=width).reshape(b, s, width)
    lam_vecs = jnp.stack([lam_q1, lam_k1, lam_q2, lam_k2])
    o = diff_attention(qkv, bias, lam_vecs, subln, lambda_init)
    return matmul_residual(o.reshape(b * s, -1), w_o, h.reshape(b * s, d)).reshape(b, s, d)


def _ssd_layer(h, norm_g, w_in, conv_w, conv_b, dt_bias, a_log, d_skip, gate_norm, w_out):
    b, s, d = h.shape
    wide = SSM_D_INNER + conv_w.shape[1]
    pad = LANES - SSM_HEADS
    zx = norm_matmul(h, norm_g, w_in, n=wide, tn=wide // 2, out_dtype=F32).reshape(b, s, wide)
    w_dt = jnp.pad(w_in[:, wide:], ((0, 0), (0, pad)))
    dt_raw = norm_matmul(h, norm_g, w_dt, n=LANES, tn=LANES, out_dtype=F32).reshape(b, s, LANES)
    y = ssd_core(zx, dt_raw, conv_w, conv_b.reshape(1, -1),
                 jnp.pad(dt_bias, (0, pad)).reshape(1, LANES), jnp.pad(a_log, (0, pad)).reshape(1, LANES),
                 jnp.repeat(d_skip, SSM_HEAD_DIM).reshape(1, SSM_D_INNER), gate_norm.reshape(1, SSM_D_INNER))
    return matmul_residual(y.reshape(b * s, -1), w_out, h.reshape(b * s, d)).reshape(b, s, d)


def _diff_lambda_init(layer):
    return 0.8 - 0.6 * math.exp(-0.3 * layer)


def kernel(x, rel_bias,
           l0_mix_norm, l0_dil_w_qkv, l0_dil_w_o, l0_mlp_norm, l0_mlp_w_up, l0_mlp_w_down,
           l1_mix_norm, l1_diff_w_qkv, l1_diff_lam_q1, l1_diff_lam_k1, l1_diff_lam_q2, l1_diff_lam_k2,
           l1_diff_subln, l1_diff_w_o, l1_mlp_norm, l1_mlp_w_up, l1_mlp_w_down,
           l2_mix_norm, l2_ssm_w_in, l2_ssm_conv_w, l2_ssm_conv_b, l2_ssm_dt_bias, l2_ssm_A_log, l2_ssm_D,
           l2_ssm_gate_norm, l2_ssm_w_out, l2_mlp_norm, l2_mlp_w_up, l2_mlp_w_down,
           l3_mix_norm, l3_dil_w_qkv, l3_dil_w_o, l3_mlp_norm, l3_mlp_w_up, l3_mlp_w_down,
           final_norm):
    b, s, d = x.shape
    dil_biases = [dilated_bias(rel_bias, dilation) for _, dilation in DIL_PATTERNS]
    dif_bias = diff_bias(rel_bias, s)

    def mlp(h, g, w_up, w_down, final_gain=None):
        return mlp_block(h.reshape(b * s, d), g, w_up, w_down, final_gain).reshape(b, s, d)

    h = _dilated_layer(x, l0_mix_norm, l0_dil_w_qkv, l0_dil_w_o, dil_biases)
    h = mlp(h, l0_mlp_norm, l0_mlp_w_up, l0_mlp_w_down)
    h = _diff_layer(h, l1_mix_norm, l1_diff_w_qkv, l1_diff_lam_q1, l1_diff_lam_k1, l1_diff_lam_q2,
                    l1_diff_lam_k2, l1_diff_subln, l1_diff_w_o, dif_bias, _diff_lambda_init(1))
    h = mlp(h, l1_mlp_norm, l1_mlp_w_up, l1_mlp_w_down)
    h = _ssd_layer(h, l2_mix_norm, l2_ssm_w_in, l2_ssm_conv_w, l2_ssm_conv_b, l2_ssm_dt_bias, l2_ssm_A_log,
                   l2_ssm_D, l2_ssm_gate_norm, l2_ssm_w_out)
    h = mlp(h, l2_mlp_norm, l2_mlp_w_up, l2_mlp_w_down)
    h = _dilated_layer(h, l3_mix_norm, l3_dil_w_qkv, l3_dil_w_o, dil_biases)
    return mlp(h, l3_mlp_norm, l3_mlp_w_up, l3_mlp_w_down, final_norm)
```

```python
import functools
import math

import jax
import jax.numpy as jnp
from jax import lax
from jax.experimental import pallas as pl
from jax.experimental.pallas import tpu as pltpu

F32 = jnp.float32
BF16 = jnp.bfloat16

NORM_EPS = 1e-5
HEAD_DIM = 64
DIL_PATTERNS = ((128, 1), (512, 4), (2048, 16))
DIL_HEADS = 16
DIL_STEPS = 128
DIFF_HEADS = 8
N_BUCKETS = 32
MAX_DISTANCE = 2048
SSM_HEADS = 32
SSM_GROUPS = 8
SSM_D_STATE = 128
SSM_HEAD_DIM = 64
SSM_D_INNER = SSM_HEADS * SSM_HEAD_DIM
SSM_CONV = 4
SSM_CHUNK = 128
LANES = 128
SUBLANES = 8
MASK_VALUE = -1e30
MIB = 1024 * 1024
VMEM_PROJECTION = 34 * MIB
VMEM_PROJECTION_WIDE = 42 * MIB
VMEM_MLP = 52 * MIB
VMEM_RESIDUAL = 36 * MIB
VMEM_COMBINE = 40 * MIB
VMEM_DIFF_ATTENTION = 36 * MIB
VMEM_SSD = 20 * MIB
VMEM_DIFF_BIAS = 14 * MIB
VMEM_DIL_ATTENTION = 40 * MIB
VMEM_SMALL = 6 * MIB

DIFF_TILE = 512
DIFF_HEADS_PER_STEP = 2


def _bucket_thresholds():
    exact = N_BUCKETS // 2

    def bucket(d):
        if d < exact:
            return d
        big = exact + int(math.log(d / exact) / math.log(MAX_DISTANCE / exact) * (N_BUCKETS - exact))
        return min(big, N_BUCKETS - 1)

    thr = {}
    d = 0
    for b in range(1, N_BUCKETS):
        while bucket(d) < b:
            d += 1
        thr[b] = d
    return thr, bucket


_BUCKET_THR, _bucket_of = _bucket_thresholds()


def _cparams(grid_rank, vmem_bytes):
    return pltpu.CompilerParams(dimension_semantics=("arbitrary",) * grid_rank, vmem_limit_bytes=vmem_bytes)


def _rms(x, gain):
    ms = jnp.mean(x * x, axis=-1, keepdims=True)
    return x * lax.rsqrt(ms + NORM_EPS) * gain


def _norm_matmul_body(x_ref, g_ref, w_ref, o_ref, *slab_ref, r, valid_cols):
    rows, d = x_ref.shape
    if r == 1:
        x = x_ref[...]
    else:
        slab = slab_ref[0]
        per = rows // r
        for c in range(d // LANES):
            slab[c] = x_ref[:, c * LANES:(c + 1) * LANES]
        x = jnp.concatenate(
            [jnp.concatenate([slab[c, pl.ds(rho, per, stride=r), :] for c in range(d // LANES)], axis=1)
             for rho in range(r)], axis=0)
    xn = _rms(x, g_ref[...]).astype(BF16)
    w = w_ref[...]
    if valid_cols is not None:
        w = jnp.where(lax.broadcasted_iota(jnp.int32, (1, w.shape[1]), 1) < valid_cols, w, 0.0)
    res = jnp.dot(xn, w.astype(BF16), preferred_element_type=F32).astype(o_ref.dtype)
    o_ref[...] = res.reshape(o_ref.shape)


def norm_matmul(x, gain, w, *, n, tn, col_block=0, dilation=1, rows=512, out_dtype=BF16,
                vmem_bytes=VMEM_PROJECTION):
    first_col = col_block * tn
    valid_cols = w.shape[1] - first_col if first_col + n > w.shape[1] else None
    assert valid_cols is None or n == tn
    b, s, d = x.shape
    r = dilation
    per = rows // r
    scratch = [pltpu.VMEM((d // LANES, rows, LANES), F32)] if r > 1 else []
    return pl.pallas_call(
        functools.partial(_norm_matmul_body, r=r, valid_cols=valid_cols),
        grid=(n // tn, b, s // rows),
        in_specs=[
            pl.BlockSpec((None, rows, d), lambda j, bi, i: (bi, i, 0)),
            pl.BlockSpec((1, d), lambda j, bi, i: (0, 0)),
            pl.BlockSpec((d, tn), lambda j, bi, i: (0, col_block + j), pipeline_mode=pl.Buffered(1)),
        ],
        out_specs=pl.BlockSpec((None, r, per, tn), lambda j, bi, i: (bi, 0, i, j)),
        out_shape=jax.ShapeDtypeStruct((b, r, s // r, n), out_dtype),
        scratch_shapes=scratch,
        compiler_params=_cparams(3, vmem_bytes),
        name="norm_matmul",
    )(x, gain.reshape(1, d), w)


def _matmul_res_body(a_ref, w_ref, r_ref, o_ref):
    o_ref[...] = r_ref[...] + jnp.dot(a_ref[...], w_ref[...].astype(BF16), preferred_element_type=F32)


def matmul_residual(a, w, res, *, tm=512):
    m, k = a.shape
    n = w.shape[1]
    return pl.pallas_call(
        _matmul_res_body,
        grid=(m // tm,),
        in_specs=[
            pl.BlockSpec((tm, k), lambda i: (i, 0)),
            pl.BlockSpec((k, n), lambda i: (0, 0)),
            pl.BlockSpec((tm, n), lambda i: (i, 0)),
        ],
        out_specs=pl.BlockSpec((tm, n), lambda i: (i, 0)),
        out_shape=jax.ShapeDtypeStruct((m, n), F32),
        compiler_params=_cparams(1, VMEM_RESIDUAL),
        name="matmul_residual",
    )(a, w, res)


def _mlp_body(h_ref, g_ref, wu_ref, wd_ref, fg_ref, o_ref, *, final_norm):
    f = pl.program_id(1)
    last = pl.num_programs(1) - 1

    def contribution():
        xn = _rms(h_ref[...], g_ref[...]).astype(BF16)
        u = jnp.maximum(jnp.dot(xn, wu_ref[...].astype(BF16), preferred_element_type=F32), 0.0)
        return jnp.dot((u * u).astype(BF16), wd_ref[...].astype(BF16), preferred_element_type=F32)

    @pl.when(f == 0)
    def _():
        o_ref[...] = h_ref[...] + contribution()

    @pl.when((f > 0) & (f < last))
    def _():
        o_ref[...] += contribution()

    @pl.when(f == last)
    def _():
        out = o_ref[...] + contribution()
        if final_norm:
            out = _rms(out, fg_ref[...])
        o_ref[...] = out


def mlp_block(h, gain, w_up, w_down, final_gain=None, *, tm=1024, tf=1024):
    m, d = h.shape
    f = w_up.shape[1]
    assert f // tf >= 2
    final_norm = final_gain is not None
    fg = (final_gain if final_norm else gain).reshape(1, d)
    return pl.pallas_call(
        functools.partial(_mlp_body, final_norm=final_norm),
        grid=(m // tm, f // tf),
        in_specs=[
            pl.BlockSpec((tm, d), lambda i, j: (i, 0)),
            pl.BlockSpec((1, d), lambda i, j: (0, 0)),
            pl.BlockSpec((d, tf), lambda i, j: (0, j)),
            pl.BlockSpec((tf, d), lambda i, j: (j, 0)),
            pl.BlockSpec((1, d), lambda i, j: (0, 0)),
        ],
        out_specs=pl.BlockSpec((tm, d), lambda i, j: (i, 0)),
        out_shape=jax.ShapeDtypeStruct((m, d), F32),
        compiler_params=_cparams(2, VMEM_MLP),
        name="mlp_block",
    )(h, gain.reshape(1, d), w_up, w_down, fg)


def _bias_from_dist(dist, col_of_bucket, d_lo, d_hi):
    val = jnp.full(dist.shape, col_of_bucket(_bucket_of(d_lo)), F32)
    for b in range(1, N_BUCKETS):
        t = _BUCKET_THR[b]
        if d_lo < t <= d_hi:
            val = jnp.where(dist >= t, col_of_bucket(b), val)
    return val


def _dil_bias_body(tab_ref, o_ref, *, dilation):
    c = pl.program_id(0)
    n = DIL_STEPS
    qi = lax.broadcasted_iota(jnp.int32, (n, 2 * n), 0)
    kj = lax.broadcasted_iota(jnp.int32, (n, 2 * n), 1)
    steps = n + qi - kj
    band = (steps >= 0) & (steps <= n)
    dist = jnp.clip(steps, 0, n) * dilation
    val = _bias_from_dist(dist, lambda b: tab_ref[b, c], 0, n * dilation)
    later = jnp.where(band, val, MASK_VALUE)
    o_ref[1] = later
    o_ref[0] = jnp.where(kj >= n, later, MASK_VALUE)


def dilated_bias(rel_bias, dilation):
    n = DIL_STEPS
    return pl.pallas_call(
        functools.partial(_dil_bias_body, dilation=dilation),
        grid=(DIL_HEADS,),
        in_specs=[pl.BlockSpec(memory_space=pltpu.SMEM)],
        out_specs=pl.BlockSpec((2, None, n, 2 * n), lambda c: (0, c, 0, 0)),
        out_shape=jax.ShapeDtypeStruct((2, DIL_HEADS, n, 2 * n), F32),
        compiler_params=_cparams(1, VMEM_SMALL),
        name="dilated_bias",
    )(rel_bias)


def _diff_bias_body(tab_ref, o_ref, *, n_delta):
    m = pl.program_id(0)
    h = pl.program_id(1)
    c = m * DIFF_HEADS + h
    t = DIFF_TILE
    qi = lax.broadcasted_iota(jnp.int32, (t, t), 0)
    kj = lax.broadcasted_iota(jnp.int32, (t, t), 1)
    for delta in range(n_delta):
        dist = delta * t + qi - kj
        d_lo = max(delta * t - (t - 1), 0)
        d_hi = delta * t + t - 1
        val = _bias_from_dist(dist, lambda b: tab_ref[b, c], d_lo, d_hi)
        if delta == 0:
            val = jnp.where(dist >= 0, val, MASK_VALUE)
        o_ref[delta] = val


def _diff_first_constant_delta():
    t = DIFF_TILE
    last_thr = _BUCKET_THR[N_BUCKETS - 1]
    delta = 1
    while delta * t - (t - 1) < last_thr:
        delta += 1
    return delta


def _diff_n_delta(s):
    return min(_diff_first_constant_delta() + 1, s // DIFF_TILE)


def diff_bias(rel_bias, s):
    t = DIFF_TILE
    n_delta = _diff_n_delta(s)
    return pl.pallas_call(
        functools.partial(_diff_bias_body, n_delta=n_delta),
        grid=(2, DIFF_HEADS),
        in_specs=[pl.BlockSpec(memory_space=pltpu.SMEM)],
        out_specs=pl.BlockSpec((None, None, n_delta, t, t), lambda m, h: (h, m, 0, 0, 0)),
        out_shape=jax.ShapeDtypeStruct((DIFF_HEADS, 2, n_delta, t, t), F32),
        compiler_params=_cparams(2, VMEM_DIFF_BIAS),
        name="diff_bias",
    )(rel_bias)


def _dil_attn_body(q_ref, kc_ref, vc_ref, bias_ref, o_ref, lse_ref, kp_ref, vp_ref):
    n = DIL_STEPS
    hd = DIL_HEADS * HEAD_DIM
    nt = (((1,), (1,)), ((), ()))
    lane = lax.broadcasted_iota(jnp.int32, (1, LANES), 1)
    low = lane < HEAD_DIM
    ones_low = jnp.where(low, 1, 0).astype(BF16) + jnp.zeros((2 * n, LANES), BF16)
    ones_high = jnp.where(low, 0, 1).astype(BF16) + jnp.zeros((2 * n, LANES), BF16)

    @pl.when(pl.program_id(2) == 0)
    def _():
        kp_ref[...] = jnp.zeros_like(kp_ref)
        vp_ref[...] = jnp.zeros_like(vp_ref)

    npairs = hd // LANES
    first_variant = jnp.minimum(pl.program_id(2), 1)

    def keys_values(blk, cs):
        if blk == 0:
            return (jnp.concatenate([kp_ref[:, cs], kc_ref[0:n, cs]], axis=0),
                    jnp.concatenate([vp_ref[:, cs], vc_ref[0:n, cs]], axis=0))
        return kc_ref[:, cs], vc_ref[:, cs]

    lse_tiles = [jnp.zeros((n, LANES), F32), jnp.zeros((n, LANES), F32)]
    for pr in range(npairs):
        cs = slice(pr * LANES, (pr + 1) * LANES)
        for blk in range(2):
            rows = slice(blk * n, (blk + 1) * n)
            k2, v2 = keys_values(blk, cs)
            q2 = q_ref[rows, cs] * jnp.asarray(HEAD_DIM ** -0.5, BF16)
            qs = jnp.concatenate([jnp.where(low, q2, 0), jnp.where(low, 0, q2)], axis=0)
            variant = first_variant if blk == 0 else 1
            bias = bias_ref[variant, 2 * pr:2 * pr + 2].reshape(2 * n, 2 * n)
            s = lax.dot_general(qs, k2, nt, preferred_element_type=F32) + bias
            m = jnp.max(jnp.maximum(s[:, :n], s[:, n:]), axis=-1, keepdims=True)
            p = jnp.exp(s - m).astype(BF16)
            w = jnp.concatenate(
                [jnp.concatenate([jnp.where(low, v2, 0), ones_low], axis=1),
                 jnp.concatenate([jnp.where(low, 0, v2), ones_high], axis=1)], axis=0)
            r = jnp.dot(jnp.concatenate([p[:n], p[n:]], axis=1), w, preferred_element_type=F32)
            den = r[:, LANES:]
            o_ref[rows, cs] = (r[:, :LANES] / den).astype(o_ref.dtype)
            lse = jnp.where(low, m[:n], m[n:]) + jnp.log(den)
            lse_tiles[blk] = jnp.where((lane == pr) | (lane == HEAD_DIM + pr), lse, lse_tiles[blk])
    lse_ref[0:n, :] = lse_tiles[0]
    lse_ref[n:, :] = lse_tiles[1]
    kp_ref[...] = kc_ref[n:, :]
    vp_ref[...] = vc_ref[n:, :]


def dilated_attention(qkv, bias, dilation):
    b, r, length, _ = qkv.shape
    n = DIL_STEPS
    hd = DIL_HEADS * HEAD_DIM
    rows = 2 * n
    assert length % rows == 0

    def cur(col):
        return pl.BlockSpec((None, None, rows, hd), lambda bi, rho, i: (bi, rho, i, col))

    return pl.pallas_call(
        _dil_attn_body,
        grid=(b, r, length // rows),
        in_specs=[cur(0), cur(1), cur(2),
                  pl.BlockSpec((2, DIL_HEADS, n, 2 * n), lambda bi, rho, i: (0, 0, 0, 0))],
        out_specs=[pl.BlockSpec((None, None, rows, hd), lambda bi, rho, i: (bi, rho, i, 0)),
                   pl.BlockSpec((None, None, rows, LANES), lambda bi, rho, i: (bi, rho, i, 0))],
        out_shape=[jax.ShapeDtypeStruct((b, r, length, hd), BF16),
                   jax.ShapeDtypeStruct((b, r, length, LANES), F32)],
        scratch_shapes=[pltpu.VMEM((n, hd), BF16), pltpu.VMEM((n, hd), BF16)],
        compiler_params=_cparams(3, VMEM_DIL_ATTENTION),
        name=f"dilated_attention_r{dilation}",
    )(qkv, qkv, qkv, bias)


def _dil_combine_body(o1_ref, o2_ref, o3_ref, l1_ref, l2_ref, l3_ref, e_ref, w_ref, h_ref, out_ref,
                      oslab_ref, lslab_ref):
    tm = out_ref.shape[0]
    nslab = oslab_ref.shape[1]

    def token_order(o_ref, l_ref, k):
        r = o_ref.shape[0]
        if r == 1:
            return o_ref[0].astype(F32), l_ref[0]
        per = tm // r
        for rho in range(r):
            blk = o_ref[rho].astype(F32)
            for c in range(nslab):
                oslab_ref[k, c, pl.ds(rho, per, stride=r), :] = blk[:, c * LANES:(c + 1) * LANES]
            lslab_ref[k, pl.ds(rho, per, stride=r), :] = l_ref[rho]
        return jnp.concatenate([oslab_ref[k, c] for c in range(nslab)], axis=1), lslab_ref[k]

    o1, l1 = token_order(o1_ref, l1_ref, 0)
    o2, l2 = token_order(o2_ref, l2_ref, 0)
    o3, l3 = token_order(o3_ref, l3_ref, 1)
    mx = jnp.maximum(jnp.maximum(l1, l2), l3)
    e1, e2, e3 = jnp.exp(l1 - mx), jnp.exp(l2 - mx), jnp.exp(l3 - mx)
    inv = 1.0 / (e1 + e2 + e3)

    mix = o3 + _spread(e1 * inv, e_ref) * (o1 - o3) + _spread(e2 * inv, e_ref) * (o2 - o3)
    out_ref[...] = h_ref[...] + jnp.dot(mix.astype(BF16), w_ref[...].astype(BF16), preferred_element_type=F32)


def _head_spread_matrix():
    hd = DIL_HEADS * HEAD_DIM
    lane = jnp.arange(LANES)[:, None]
    col = jnp.arange(hd)[None, :]
    head = col // HEAD_DIM
    src = jnp.where(head % 2 == 0, head // 2, HEAD_DIM + head // 2)
    return (lane == src).astype(BF16)


def dilated_combine(outs, lses, w_o, h, *, tm=512):
    b, s, d = h.shape
    hd = w_o.shape[0]

    def regrouped(arr):
        r = arr.shape[1]
        return pl.BlockSpec((None, r, tm // r, arr.shape[3]), lambda bi, i: (bi, 0, i, 0))

    return pl.pallas_call(
        _dil_combine_body,
        grid=(b, s // tm),
        in_specs=[regrouped(a) for a in (*outs, *lses)] + [
            pl.BlockSpec((2 * LANES, hd), lambda bi, i: (0, 0)),
            pl.BlockSpec(w_o.shape, lambda bi, i: (0, 0)),
            pl.BlockSpec((None, tm, d), lambda bi, i: (bi, i, 0))],
        out_specs=pl.BlockSpec((None, tm, d), lambda bi, i: (bi, i, 0)),
        out_shape=jax.ShapeDtypeStruct((b, s, d), F32),
        scratch_shapes=[pltpu.VMEM((2, hd // LANES, tm, LANES), F32), pltpu.VMEM((2, tm, LANES), F32)],
        compiler_params=_cparams(2, VMEM_COMBINE),
        name="dilated_combine",
    )(*outs, *lses, _stacked(_head_spread_matrix()), w_o, h)


def _diff_attn_body(qi_ref, ki_ref, q_ref, k_ref, v_ref, bias_ref, lam_ref, g_ref, o_ref,
                    m_ref, acc_ref, *, lambda_init):
    p = pl.program_id(1)
    qi = qi_ref[p]
    ki = ki_ref[p]
    nbatch, t, _ = q_ref.shape
    hw = 2 * HEAD_DIM
    nheads = q_ref.shape[2] // hw
    nt = (((1,), (1,)), ((), ()))
    low = lax.broadcasted_iota(jnp.int32, (1, hw), 1) < HEAD_DIM

    @pl.when(ki == 0)
    def _():
        m_ref[...] = jnp.full(m_ref.shape, MASK_VALUE, F32)
        acc_ref[...] = jnp.zeros_like(acc_ref)

    for bi in range(nbatch):
        for hh in range(nheads):
            hs = slice(hh * hw, (hh + 1) * hw)
            q = q_ref[bi, :, hs] * jnp.asarray(HEAD_DIM ** -0.5, BF16)
            k = k_ref[bi, :, hs]
            v1 = jnp.concatenate([v_ref[bi, :, hs], jnp.ones((t, hw), BF16)], axis=1)
            for mp in range(2):
                qm = jnp.where(low, q, 0) if mp == 0 else jnp.where(low, 0, q)
                s = lax.dot_general(qm, k, nt, preferred_element_type=F32) + bias_ref[hh, mp]
                m_old = m_ref[bi, hh, mp]
                m_new = jnp.maximum(m_old, jnp.max(s, axis=-1, keepdims=True))
                alpha = jnp.exp(m_old - m_new)
                pr = jnp.exp(s - jnp.concatenate([m_new] * (t // hw), axis=1))
                acc_ref[bi, hh, mp] = (jnp.concatenate([alpha, alpha], axis=1) * acc_ref[bi, hh, mp]
                                       + jnp.dot(pr.astype(BF16), v1, preferred_element_type=F32))
                m_ref[bi, hh, mp] = m_new

    @pl.when(ki == qi)
    def _():
        lam = (jnp.exp(jnp.sum(lam_ref[0:1, :] * lam_ref[1:2, :], axis=-1, keepdims=True))
               - jnp.exp(jnp.sum(lam_ref[2:3, :] * lam_ref[3:4, :], axis=-1, keepdims=True)) + lambda_init)
        gain = g_ref[...] * (1.0 - lambda_init)

        def finish(bi, carry):
            for hh in range(nheads):
                a0 = acc_ref[bi, hh, 0]
                a1 = acc_ref[bi, hh, 1]
                o = a0[:, :hw] / a0[:, hw:] - lam * (a1[:, :hw] / a1[:, hw:])
                o_ref[bi, :, hh * hw:(hh + 1) * hw] = _rms(o, gain).astype(o_ref.dtype)
            return carry

        lax.fori_loop(0, nbatch, finish, 0)


def diff_attention(qkv, bias, lam_vecs, subln, lambda_init):
    b, s, _ = qkv.shape
    t = DIFF_TILE
    nq = s // t
    n_delta = bias.shape[2]
    pairs = [(qi, ki) for qi in range(nq) for ki in range(qi + 1)]
    qi_arr = jnp.asarray([pq for pq, _ in pairs], jnp.int32)
    ki_arr = jnp.asarray([pk for _, pk in pairs], jnp.int32)
    hw = 2 * HEAD_DIM
    nh = DIFF_HEADS
    hps = DIFF_HEADS_PER_STEP
    ng = nh // hps
    gw = hps * hw
    grid_spec = pltpu.PrefetchScalarGridSpec(
        num_scalar_prefetch=2,
        grid=(ng, len(pairs)),
        in_specs=[
            pl.BlockSpec((b, t, gw), lambda h, p, qa, ka: (0, qa[p], h)),
            pl.BlockSpec((b, t, gw), lambda h, p, qa, ka: (0, ka[p], ng + h)),
            pl.BlockSpec((b, t, gw), lambda h, p, qa, ka: (0, ka[p], 2 * ng + h)),
            pl.BlockSpec((hps, 2, None, t, t),
                         lambda h, p, qa, ka: (h, 0, jnp.minimum(qa[p] - ka[p], n_delta - 1), 0, 0)),
            pl.BlockSpec((4, HEAD_DIM), lambda h, p, qa, ka: (0, 0)),
            pl.BlockSpec((1, hw), lambda h, p, qa, ka: (0, 0)),
        ],
        out_specs=pl.BlockSpec((b, t, gw), lambda h, p, qa, ka: (0, qa[p], h)),
        scratch_shapes=[pltpu.VMEM((b, hps, 2, t, hw), F32), pltpu.VMEM((b, hps, 2, t, 2 * hw), F32)],
    )
    return pl.pallas_call(
        functools.partial(_diff_attn_body, lambda_init=lambda_init),
        grid_spec=grid_spec,
        out_shape=jax.ShapeDtypeStruct((b, s, nh * hw), BF16),
        compiler_params=_cparams(2, VMEM_DIFF_ATTENTION),
        name="diff_attention",
    )(qi_arr, ki_arr, qkv, qkv, qkv, bias, lam_vecs, subln.reshape(1, hw))


def _silu(x):
    h = 0.5 * x
    return h + h * jnp.tanh(h)


def _conv_silu(cur, win_ref, shift_ref, w_ref, b_ref, cols):
    l = cur.shape[0]
    lagged = jnp.dot(shift_ref[...], win_ref[:, cols], preferred_element_type=F32)
    acc = b_ref[:, cols] + w_ref[SSM_CONV - 1:SSM_CONV, cols] * cur
    for k in range(SSM_CONV - 1):
        acc = acc + w_ref[k:k + 1, cols] * lagged[k * l:(k + 1) * l]
    return _silu(acc)


def _shift_matrix(l):
    row = jnp.arange((SSM_CONV - 1) * l)[:, None]
    col = jnp.arange(2 * l)[None, :]
    k, t = row // l, row % l
    return (col == l + t - (SSM_CONV - 1 - k)).astype(BF16)


def _spread(w, e_ref):
    hi = w.astype(BF16)
    lo = (w - hi.astype(F32)).astype(BF16)
    return jnp.dot(jnp.concatenate([hi, lo], axis=1), e_ref[...], preferred_element_type=F32)


def _stacked(e):
    return jnp.concatenate([e, e], axis=0)


def _cumsum_rows(a):
    rows = a.shape[0]
    ridx = lax.broadcasted_iota(jnp.int32, a.shape, 0)
    sh = 1
    while sh < rows:
        a = a + jnp.where(ridx >= sh, pltpu.roll(a, sh, 0), 0.0)
        sh *= 2
    return a


def _ssd_body(z_ref, x_ref, bm_ref, cm_ref, dt_ref, cw_ref, cb_ref, dtb_ref, alog_ref, dsk_ref, gn_ref, e_ref,
              sh_ref, y_ref, state_ref, win_ref):
    c = pl.program_id(1)
    l = SSM_CHUNK
    ns = SSM_D_STATE
    di = SSM_D_INNER
    gn = SSM_GROUPS * ns
    hg = SSM_HEADS // SSM_GROUPS
    gw = hg * SSM_HEAD_DIM
    tail = 2 * SUBLANES

    @pl.when(c == 0)
    def _():
        state_ref[...] = jnp.zeros_like(state_ref)
        win_ref[0:l, :] = jnp.zeros((l, win_ref.shape[1]), BF16)

    win_ref[l:, 0:di] = x_ref[...].astype(BF16)
    win_ref[l:, di:di + gn] = bm_ref[...].astype(BF16)
    win_ref[l:, di + gn:] = cm_ref[...].astype(BF16)
    xs = [_conv_silu(x_ref[:, g * gw:(g + 1) * gw], win_ref, sh_ref, cw_ref, cb_ref, slice(g * gw, (g + 1) * gw))
          for g in range(SSM_GROUPS)]
    bmat = _conv_silu(bm_ref[...], win_ref, sh_ref, cw_ref, cb_ref, slice(di, di + gn)).astype(BF16)
    cmat = _conv_silu(cm_ref[...], win_ref, sh_ref, cw_ref, cb_ref, slice(di + gn, di + 2 * gn)).astype(BF16)
    win_ref[l - tail:l, :] = win_ref[2 * l - tail:, :]

    dt = jax.nn.softplus(dt_ref[...] + dtb_ref[...])
    a_cs = _cumsum_rows(dt * (-jnp.exp(alog_ref[...])))
    a_cs_t = a_cs.T
    last = a_cs[l - 1:l, :]
    dt_e = _spread(dt, e_ref)
    exp_a_e = _spread(jnp.exp(a_cs), e_ref)
    state_decay_e = _spread(jnp.exp(last - a_cs), e_ref)

    causal = (lax.broadcasted_iota(jnp.int32, (l, l), 0) >= lax.broadcasted_iota(jnp.int32, (l, l), 1))
    lane_head = lax.broadcasted_iota(jnp.int32, (1, gw), 1) // SSM_HEAD_DIM
    tn_dims = (((0,), (0,)), ((), ()))
    nt_dims = (((1,), (1,)), ((), ()))

    for g in range(SSM_GROUPS):
        gs = slice(g * gw, (g + 1) * gw)
        cg = cmat[:, g * ns:(g + 1) * ns]
        bg = bmat[:, g * ns:(g + 1) * ns]
        xg = xs[g]
        xdt = xg * dt_e[:, gs]
        xdt_b = xdt.astype(BF16)
        cb = lax.dot_general(cg, bg, nt_dims, preferred_element_type=F32)
        weights, operands = [], []
        for j in range(hg):
            hh = hg * g + j
            seg = a_cs[:, hh:hh + 1] - a_cs_t[hh:hh + 1, :]
            decay = jnp.where(causal, jnp.exp(seg), 0.0)
            weights.append((cb * decay).astype(BF16))
            operands.append(jnp.where(lane_head == j, xdt_b, 0))
        y_diag = jnp.dot(jnp.concatenate(weights, axis=1), jnp.concatenate(operands, axis=0),
                         preferred_element_type=F32)
        prev = state_ref[g]
        y_off = jnp.dot(cg, prev.astype(BF16), preferred_element_type=F32) * exp_a_e[:, gs]
        new = lax.dot_general(bg, (xdt * state_decay_e[:, gs]).astype(BF16), tn_dims,
                              preferred_element_type=F32)
        state_ref[g] = prev * exp_a_e[l - 1:l, gs] + new
        y = y_diag + y_off + xg * dsk_ref[:, gs]
        zg = z_ref[:, gs]
        y = y * _silu(zg)
        y_ref[:, gs] = _rms(y, gn_ref[:, gs]).astype(y_ref.dtype)


def ssd_core(zx, dt_raw, conv_w, conv_b, dt_bias, a_log, d_skip, gate_norm):
    b, s, _ = zx.shape
    l = SSM_CHUNK
    di = SSM_D_INNER
    gn = SSM_GROUPS * SSM_D_STATE
    conv_dim = di + 2 * gn
    z_spec = pl.BlockSpec((None, l, di), lambda bi, c: (bi, c, 0))
    x_spec = pl.BlockSpec((None, l, di), lambda bi, c: (bi, c, 1))
    b_spec = pl.BlockSpec((None, l, gn), lambda bi, c: (bi, c, 2 * di // gn))
    c_spec = pl.BlockSpec((None, l, gn), lambda bi, c: (bi, c, 2 * di // gn + 1))

    def whole(shape):
        return pl.BlockSpec(shape, lambda bi, c: (0, 0))

    head_lanes = (jnp.arange(LANES)[:, None] == jnp.arange(di)[None, :] // SSM_HEAD_DIM).astype(BF16)
    return pl.pallas_call(
        _ssd_body,
        grid=(b, s // l),
        in_specs=[z_spec, x_spec, b_spec, c_spec,
                  pl.BlockSpec((None, l, LANES), lambda bi, c: (bi, c, 0)),
                  whole((SSM_CONV, conv_dim)), whole((1, conv_dim)),
                  whole((1, LANES)), whole((1, LANES)), whole((1, di)), whole((1, di)), whole((2 * LANES, di)),
                  whole(((SSM_CONV - 1) * l, 2 * l))],
        out_specs=pl.BlockSpec((None, l, di), lambda bi, c: (bi, c, 0)),
        out_shape=jax.ShapeDtypeStruct((b, s, di), BF16),
        scratch_shapes=[pltpu.VMEM((SSM_GROUPS, SSM_D_STATE, (SSM_HEADS // SSM_GROUPS) * SSM_HEAD_DIM), F32),
                        pltpu.VMEM((2 * l, conv_dim), BF16)],
        compiler_params=_cparams(2, VMEM_SSD),
        name="ssd_core",
    )(zx, zx, zx, zx, dt_raw, conv_w, conv_b, dt_bias, a_log, d_skip, gate_norm, _stacked(head_lanes), _shift_matrix(l))


def _dilated_layer(h, norm_g, w_qkv, w_o, dil_biases):
    b, s, d = h.shape
    width = 3 * DIL_HEADS * HEAD_DIM
    outs, lses = [], []
    for g, (_, dilation) in enumerate(DIL_PATTERNS):
        wide = dilation == 1
        qkv = norm_matmul(h, norm_g, w_qkv, n=width, tn=width, col_block=g, dilation=dilation,
                          rows=1024 if wide else 512,
                          vmem_bytes=VMEM_PROJECTION_WIDE if wide else VMEM_PROJECTION)
        o, lse = dilated_attention(qkv, dil_biases[g], dilation)
        outs.append(o)
        lses.append(lse)
    return dilated_combine(outs, lses, w_o, h)


def _diff_layer(h, norm_g, w_qkv, lam_q1, lam_k1, lam_q2, lam_k2, subln, w_o, bias, lambda_init):
    b, s, d = h.shape
    width = w_qkv.shape[1]
    qkv = norm_matmul(h, norm_g, w_qkv, n=width, tn=width, rows=1024,
                      vmem_bytes=VMEM_PROJECTION_WIDE).reshape(b, s, width)
    lam_vecs = jnp.stack([lam_q1, lam_k1, lam_q2, lam_k2])
    o = diff_attention(qkv, bias, lam_vecs, subln, lambda_init)
    return matmul_residual(o.reshape(b * s, -1), w_o, h.reshape(b * s, d)).reshape(b, s, d)


def _ssd_layer(h, norm_g, w_in, conv_w, conv_b, dt_bias, a_log, d_skip, gate_norm, w_out):
    b, s, d = h.shape
    wide = SSM_D_INNER + conv_w.shape[1]
    pad = LANES - SSM_HEADS
    zx = norm_matmul(h, norm_g, w_in, n=wide, tn=wide // 2, out_dtype=F32).reshape(b, s, wide)
    dt_raw = norm_matmul(h, norm_g, w_in, n=LANES, tn=LANES, col_block=wide // LANES,
                         out_dtype=F32).reshape(b, s, LANES)
    y = ssd_core(zx, dt_raw, conv_w, conv_b.reshape(1, -1),
                 jnp.pad(dt_bias, (0, pad)).reshape(1, LANES), jnp.pad(a_log, (0, pad)).reshape(1, LANES),
                 jnp.repeat(d_skip, SSM_HEAD_DIM).reshape(1, SSM_D_INNER), gate_norm.reshape(1, SSM_D_INNER))
    return matmul_residual(y.reshape(b * s, -1), w_out, h.reshape(b * s, d)).reshape(b, s, d)


def _diff_lambda_init(layer):
    return 0.8 - 0.6 * math.exp(-0.3 * layer)


def kernel(x, rel_bias,
           l0_mix_norm, l0_dil_w_qkv, l0_dil_w_o, l0_mlp_norm, l0_mlp_w_up, l0_mlp_w_down,
           l1_mix_norm, l1_diff_w_qkv, l1_diff_lam_q1, l1_diff_lam_k1, l1_diff_lam_q2, l1_diff_lam_k2,
           l1_diff_subln, l1_diff_w_o, l1_mlp_norm, l1_mlp_w_up, l1_mlp_w_down,
           l2_mix_norm, l2_ssm_w_in, l2_ssm_conv_w, l2_ssm_conv_b, l2_ssm_dt_bias, l2_ssm_A_log, l2_ssm_D,
           l2_ssm_gate_norm, l2_ssm_w_out, l2_mlp_norm, l2_mlp_w_up, l2_mlp_w_down,
           l3_mix_norm, l3_dil_w_qkv, l3_dil_w_o, l3_mlp_norm, l3_mlp_w_up, l3_mlp_w_down,
           final_norm):
    b, s, d = x.shape
    dil_biases = [dilated_bias(rel_bias, dilation) for _, dilation in DIL_PATTERNS]
    dif_bias = diff_bias(rel_bias, s)

    def mlp(h, g, w_up, w_down, final_gain=None):
        return mlp_block(h.reshape(b * s, d), g, w_up, w_down, final_gain).reshape(b, s, d)

    h = _dilated_layer(x, l0_mix_norm, l0_dil_w_qkv, l0_dil_w_o, dil_biases)
    h = mlp(h, l0_mlp_norm, l0_mlp_w_up, l0_mlp_w_down)
    h = _diff_layer(h, l1_mix_norm, l1_diff_w_qkv, l1_diff_lam_q1, l1_diff_lam_k1, l1_diff_lam_q2,
                    l1_diff_lam_k2, l1_diff_subln, l1_diff_w_o, dif_bias, _diff_lambda_init(1))
    h = mlp(h, l1_mlp_norm, l1_mlp_w_up, l1_mlp_w_down)
    h = _ssd_layer(h, l2_mix_norm, l2_ssm_w_in, l2_ssm_conv_w, l2_ssm_conv_b, l2_ssm_dt_bias, l2_ssm_A_log,
                   l2_ssm_D, l2_ssm_gate_norm, l2_ssm_w_out)
    h = mlp(h, l2_mlp_norm, l2_mlp_w_up, l2_mlp_w_down)
    h = _dilated_layer(h, l3_mix_norm, l3_dil_w_qkv, l3_dil_w_o, dil_biases)
    return mlp(h, l3_mlp_norm, l3_mlp_w_up, l3_mlp_w_down, final_norm)
```

```python
import functools
import math

import jax
import jax.numpy as jnp
from jax import lax
from jax.experimental import pallas as pl
from jax.experimental.pallas import tpu as pltpu

F32 = jnp.float32
BF16 = jnp.bfloat16

NORM_EPS = 1e-5
HEAD_DIM = 64
DIL_PATTERNS = ((128, 1), (512, 4), (2048, 16))
DIL_HEADS = 16
DIL_STEPS = 128
DIFF_HEADS = 8
N_BUCKETS = 32
MAX_DISTANCE = 2048
SSM_HEADS = 32
SSM_GROUPS = 8
SSM_D_STATE = 128
SSM_HEAD_DIM = 64
SSM_D_INNER = SSM_HEADS * SSM_HEAD_DIM
SSM_CONV = 4
SSM_CHUNK = 128
LANES = 128
SUBLANES = 8
MASK_VALUE = -1e30
LOG2E = math.log2(math.e)
LN2 = math.log(2.0)
Q_SCALE = LOG2E * HEAD_DIM ** -0.5
MIB = 1024 * 1024
VMEM_PROJECTION = 34 * MIB
VMEM_PROJECTION_WIDE = 42 * MIB
VMEM_MLP = 52 * MIB
VMEM_RESIDUAL = 36 * MIB
VMEM_COMBINE = 40 * MIB
VMEM_DIFF_ATTENTION = 36 * MIB
VMEM_SSD = 20 * MIB
VMEM_DIFF_BIAS = 14 * MIB
VMEM_DIL_ATTENTION = 40 * MIB
VMEM_SMALL = 6 * MIB

DIFF_TILE = 512
DIFF_HEADS_PER_STEP = 2


def _bucket_thresholds():
    exact = N_BUCKETS // 2

    def bucket(d):
        if d < exact:
            return d
        big = exact + int(math.log(d / exact) / math.log(MAX_DISTANCE / exact) * (N_BUCKETS - exact))
        return min(big, N_BUCKETS - 1)

    thr = {}
    d = 0
    for b in range(1, N_BUCKETS):
        while bucket(d) < b:
            d += 1
        thr[b] = d
    return thr, bucket


_BUCKET_THR, _bucket_of = _bucket_thresholds()


def _cparams(grid_rank, vmem_bytes):
    return pltpu.CompilerParams(dimension_semantics=("arbitrary",) * grid_rank, vmem_limit_bytes=vmem_bytes)


def _rms(x, gain):
    ms = jnp.mean(x * x, axis=-1, keepdims=True)
    return x * lax.rsqrt(ms + NORM_EPS) * gain


def _norm_matmul_body(x_ref, g_ref, w_ref, o_ref, *slab_ref, r, valid_cols, q_cols):
    rows, d = x_ref.shape
    if r == 1:
        x = x_ref[...]
    else:
        slab = slab_ref[0]
        per = rows // r
        for c in range(d // LANES):
            slab[c] = x_ref[:, c * LANES:(c + 1) * LANES]
        x = jnp.concatenate(
            [jnp.concatenate([slab[c, pl.ds(rho, per, stride=r), :] for c in range(d // LANES)], axis=1)
             for rho in range(r)], axis=0)
    xn = _rms(x, g_ref[...]).astype(BF16)
    w = w_ref[...]
    if valid_cols is not None:
        w = jnp.where(lax.broadcasted_iota(jnp.int32, (1, w.shape[1]), 1) < valid_cols, w, 0.0)
    res = jnp.dot(xn, w.astype(BF16), preferred_element_type=F32)
    if q_cols:
        res = jnp.concatenate([res[:, :q_cols] * Q_SCALE, res[:, q_cols:]], axis=1)
    o_ref[...] = res.astype(o_ref.dtype).reshape(o_ref.shape)


def norm_matmul(x, gain, w, *, n, tn, col_block=0, dilation=1, rows=512, out_dtype=BF16,
                vmem_bytes=VMEM_PROJECTION, q_cols=0):
    first_col = col_block * tn
    valid_cols = w.shape[1] - first_col if first_col + n > w.shape[1] else None
    assert valid_cols is None or n == tn
    b, s, d = x.shape
    r = dilation
    per = rows // r
    scratch = [pltpu.VMEM((d // LANES, rows, LANES), F32)] if r > 1 else []
    return pl.pallas_call(
        functools.partial(_norm_matmul_body, r=r, valid_cols=valid_cols, q_cols=q_cols),
        grid=(n // tn, b, s // rows),
        in_specs=[
            pl.BlockSpec((None, rows, d), lambda j, bi, i: (bi, i, 0)),
            pl.BlockSpec((1, d), lambda j, bi, i: (0, 0)),
            pl.BlockSpec((d, tn), lambda j, bi, i: (0, col_block + j), pipeline_mode=pl.Buffered(1)),
        ],
        out_specs=pl.BlockSpec((None, r, per, tn), lambda j, bi, i: (bi, 0, i, j)),
        out_shape=jax.ShapeDtypeStruct((b, r, s // r, n), out_dtype),
        scratch_shapes=scratch,
        compiler_params=_cparams(3, vmem_bytes),
        name="norm_matmul",
    )(x, gain.reshape(1, d), w)


def _matmul_res_body(a_ref, w_ref, r_ref, o_ref):
    o_ref[...] = r_ref[...] + jnp.dot(a_ref[...], w_ref[...].astype(BF16), preferred_element_type=F32)


def matmul_residual(a, w, res, *, tm=512):
    m, k = a.shape
    n = w.shape[1]
    return pl.pallas_call(
        _matmul_res_body,
        grid=(m // tm,),
        in_specs=[
            pl.BlockSpec((tm, k), lambda i: (i, 0)),
            pl.BlockSpec((k, n), lambda i: (0, 0)),
            pl.BlockSpec((tm, n), lambda i: (i, 0)),
        ],
        out_specs=pl.BlockSpec((tm, n), lambda i: (i, 0)),
        out_shape=jax.ShapeDtypeStruct((m, n), F32),
        compiler_params=_cparams(1, VMEM_RESIDUAL),
        name="matmul_residual",
    )(a, w, res)


def _mlp_body(h_ref, g_ref, wu_ref, wd_ref, fg_ref, o_ref, *, final_norm):
    f = pl.program_id(1)
    last = pl.num_programs(1) - 1

    def contribution():
        xn = _rms(h_ref[...], g_ref[...]).astype(BF16)
        u = jnp.maximum(jnp.dot(xn, wu_ref[...].astype(BF16), preferred_element_type=F32), 0.0)
        return jnp.dot((u * u).astype(BF16), wd_ref[...].astype(BF16), preferred_element_type=F32)

    @pl.when(f == 0)
    def _():
        o_ref[...] = h_ref[...] + contribution()

    @pl.when((f > 0) & (f < last))
    def _():
        o_ref[...] += contribution()

    @pl.when(f == last)
    def _():
        out = o_ref[...] + contribution()
        if final_norm:
            out = _rms(out, fg_ref[...])
        o_ref[...] = out


def mlp_block(h, gain, w_up, w_down, final_gain=None, *, tm=1024, tf=1024):
    m, d = h.shape
    f = w_up.shape[1]
    assert f // tf >= 2
    final_norm = final_gain is not None
    fg = (final_gain if final_norm else gain).reshape(1, d)
    return pl.pallas_call(
        functools.partial(_mlp_body, final_norm=final_norm),
        grid=(m // tm, f // tf),
        in_specs=[
            pl.BlockSpec((tm, d), lambda i, j: (i, 0)),
            pl.BlockSpec((1, d), lambda i, j: (0, 0)),
            pl.BlockSpec((d, tf), lambda i, j: (0, j)),
            pl.BlockSpec((tf, d), lambda i, j: (j, 0)),
            pl.BlockSpec((1, d), lambda i, j: (0, 0)),
        ],
        out_specs=pl.BlockSpec((tm, d), lambda i, j: (i, 0)),
        out_shape=jax.ShapeDtypeStruct((m, d), F32),
        compiler_params=_cparams(2, VMEM_MLP),
        name="mlp_block",
    )(h, gain.reshape(1, d), w_up, w_down, fg)


def _bias_from_dist(dist, col_of_bucket, d_lo, d_hi):
    val = jnp.full(dist.shape, col_of_bucket(_bucket_of(d_lo)) * LOG2E, F32)
    for b in range(1, N_BUCKETS):
        t = _BUCKET_THR[b]
        if d_lo < t <= d_hi:
            val = jnp.where(dist >= t, col_of_bucket(b) * LOG2E, val)
    return val


def _dil_bias_body(tab_ref, o_ref, *, dilation):
    c = pl.program_id(0)
    n = DIL_STEPS
    qi = lax.broadcasted_iota(jnp.int32, (n, 2 * n), 0)
    kj = lax.broadcasted_iota(jnp.int32, (n, 2 * n), 1)
    steps = n + qi - kj
    band = (steps >= 0) & (steps <= n)
    dist = jnp.clip(steps, 0, n) * dilation
    val = _bias_from_dist(dist, lambda b: tab_ref[b, c], 0, n * dilation)
    later = jnp.where(band, val, MASK_VALUE)
    o_ref[1] = later
    o_ref[0] = jnp.where(kj >= n, later, MASK_VALUE)


def dilated_bias(rel_bias, dilation):
    n = DIL_STEPS
    return pl.pallas_call(
        functools.partial(_dil_bias_body, dilation=dilation),
        grid=(DIL_HEADS,),
        in_specs=[pl.BlockSpec(memory_space=pltpu.SMEM)],
        out_specs=pl.BlockSpec((2, None, n, 2 * n), lambda c: (0, c, 0, 0)),
        out_shape=jax.ShapeDtypeStruct((2, DIL_HEADS, n, 2 * n), F32),
        compiler_params=_cparams(1, VMEM_SMALL),
        name="dilated_bias",
    )(rel_bias)


def _diff_bias_body(tab_ref, o_ref, *, n_delta):
    m = pl.program_id(0)
    h = pl.program_id(1)
    c = m * DIFF_HEADS + h
    t = DIFF_TILE
    qi = lax.broadcasted_iota(jnp.int32, (t, t), 0)
    kj = lax.broadcasted_iota(jnp.int32, (t, t), 1)
    for delta in range(n_delta):
        dist = delta * t + qi - kj
        d_lo = max(delta * t - (t - 1), 0)
        d_hi = delta * t + t - 1
        val = _bias_from_dist(dist, lambda b: tab_ref[b, c], d_lo, d_hi)
        if delta == 0:
            val = jnp.where(dist >= 0, val, MASK_VALUE)
        o_ref[delta] = val


def _diff_first_constant_delta():
    t = DIFF_TILE
    last_thr = _BUCKET_THR[N_BUCKETS - 1]
    delta = 1
    while delta * t - (t - 1) < last_thr:
        delta += 1
    return delta


def _diff_n_delta(s):
    return min(_diff_first_constant_delta() + 1, s // DIFF_TILE)


def diff_bias(rel_bias, s):
    t = DIFF_TILE
    n_delta = _diff_n_delta(s)
    return pl.pallas_call(
        functools.partial(_diff_bias_body, n_delta=n_delta),
        grid=(2, DIFF_HEADS),
        in_specs=[pl.BlockSpec(memory_space=pltpu.SMEM)],
        out_specs=pl.BlockSpec((None, None, n_delta, t, t), lambda m, h: (h, m, 0, 0, 0)),
        out_shape=jax.ShapeDtypeStruct((DIFF_HEADS, 2, n_delta, t, t), F32),
        compiler_params=_cparams(2, VMEM_DIFF_BIAS),
        name="diff_bias",
    )(rel_bias)


def _dil_attn_body(q_ref, kc_ref, vc_ref, bias_ref, o_ref, lse_ref, kp_ref, vp_ref):
    n = DIL_STEPS
    hd = DIL_HEADS * HEAD_DIM
    nt = (((1,), (1,)), ((), ()))
    lane = lax.broadcasted_iota(jnp.int32, (1, LANES), 1)
    low = lane < HEAD_DIM
    ones_low = jnp.where(low, 1, 0).astype(BF16) + jnp.zeros((2 * n, LANES), BF16)
    ones_high = jnp.where(low, 0, 1).astype(BF16) + jnp.zeros((2 * n, LANES), BF16)

    @pl.when(pl.program_id(2) == 0)
    def _():
        kp_ref[...] = jnp.zeros_like(kp_ref)
        vp_ref[...] = jnp.zeros_like(vp_ref)

    npairs = hd // LANES
    first_variant = jnp.minimum(pl.program_id(2), 1)

    def keys_values(blk, cs):
        if blk == 0:
            return (jnp.concatenate([kp_ref[:, cs], kc_ref[0:n, cs]], axis=0),
                    jnp.concatenate([vp_ref[:, cs], vc_ref[0:n, cs]], axis=0))
        return kc_ref[:, cs], vc_ref[:, cs]

    lse_tiles = [jnp.zeros((n, LANES), F32), jnp.zeros((n, LANES), F32)]
    for pr in range(npairs):
        cs = slice(pr * LANES, (pr + 1) * LANES)
        for blk in range(2):
            rows = slice(blk * n, (blk + 1) * n)
            k2, v2 = keys_values(blk, cs)
            q2 = q_ref[rows, cs]
            qs = jnp.concatenate([jnp.where(low, q2, 0), jnp.where(low, 0, q2)], axis=0)
            variant = first_variant if blk == 0 else 1
            bias = bias_ref[variant, 2 * pr:2 * pr + 2].reshape(2 * n, 2 * n)
            s = lax.dot_general(qs, k2, nt, preferred_element_type=F32) + bias
            m = jnp.max(jnp.maximum(s[:, :n], s[:, n:]), axis=-1, keepdims=True)
            p = jnp.exp2(s - m).astype(BF16)
            w = jnp.concatenate(
                [jnp.concatenate([jnp.where(low, v2, 0), ones_low], axis=1),
                 jnp.concatenate([jnp.where(low, 0, v2), ones_high], axis=1)], axis=0)
            r = jnp.dot(jnp.concatenate([p[:n], p[n:]], axis=1), w, preferred_element_type=F32)
            den = r[:, LANES:]
            o_ref[rows, cs] = (r[:, :LANES] / den).astype(o_ref.dtype)
            lse = jnp.where(low, m[:n], m[n:]) * LN2 + jnp.log(den)
            lse_tiles[blk] = jnp.where((lane == pr) | (lane == HEAD_DIM + pr), lse, lse_tiles[blk])
    lse_ref[0:n, :] = lse_tiles[0]
    lse_ref[n:, :] = lse_tiles[1]
    kp_ref[...] = kc_ref[n:, :]
    vp_ref[...] = vc_ref[n:, :]


def dilated_attention(qkv, bias, dilation):
    b, r, length, _ = qkv.shape
    n = DIL_STEPS
    hd = DIL_HEADS * HEAD_DIM
    rows = 2 * n
    assert length % rows == 0

    def cur(col):
        return pl.BlockSpec((None, None, rows, hd), lambda bi, rho, i: (bi, rho, i, col))

    return pl.pallas_call(
        _dil_attn_body,
        grid=(b, r, length // rows),
        in_specs=[cur(0), cur(1), cur(2),
                  pl.BlockSpec((2, DIL_HEADS, n, 2 * n), lambda bi, rho, i: (0, 0, 0, 0))],
        out_specs=[pl.BlockSpec((None, None, rows, hd), lambda bi, rho, i: (bi, rho, i, 0)),
                   pl.BlockSpec((None, None, rows, LANES), lambda bi, rho, i: (bi, rho, i, 0))],
        out_shape=[jax.ShapeDtypeStruct((b, r, length, hd), BF16),
                   jax.ShapeDtypeStruct((b, r, length, LANES), F32)],
        scratch_shapes=[pltpu.VMEM((n, hd), BF16), pltpu.VMEM((n, hd), BF16)],
        compiler_params=_cparams(3, VMEM_DIL_ATTENTION),
        name=f"dilated_attention_r{dilation}",
    )(qkv, qkv, qkv, bias)


def _dil_combine_body(o1_ref, o2_ref, o3_ref, l1_ref, l2_ref, l3_ref, e_ref, w_ref, h_ref, out_ref,
                      oslab_ref, lslab_ref):
    tm = out_ref.shape[0]
    nslab = oslab_ref.shape[1]

    def token_order(o_ref, l_ref, k):
        r = o_ref.shape[0]
        if r == 1:
            return o_ref[0].astype(F32), l_ref[0]
        per = tm // r
        for rho in range(r):
            blk = o_ref[rho].astype(F32)
            for c in range(nslab):
                oslab_ref[k, c, pl.ds(rho, per, stride=r), :] = blk[:, c * LANES:(c + 1) * LANES]
            lslab_ref[k, pl.ds(rho, per, stride=r), :] = l_ref[rho]
        return jnp.concatenate([oslab_ref[k, c] for c in range(nslab)], axis=1), lslab_ref[k]

    o1, l1 = token_order(o1_ref, l1_ref, 0)
    o2, l2 = token_order(o2_ref, l2_ref, 0)
    o3, l3 = token_order(o3_ref, l3_ref, 1)
    mx = jnp.maximum(jnp.maximum(l1, l2), l3)
    e1, e2, e3 = jnp.exp(l1 - mx), jnp.exp(l2 - mx), jnp.exp(l3 - mx)
    inv = 1.0 / (e1 + e2 + e3)

    mix = o3 + _spread(e1 * inv, e_ref) * (o1 - o3) + _spread(e2 * inv, e_ref) * (o2 - o3)
    out_ref[...] = h_ref[...] + jnp.dot(mix.astype(BF16), w_ref[...].astype(BF16), preferred_element_type=F32)


def _head_spread_matrix():
    hd = DIL_HEADS * HEAD_DIM
    lane = jnp.arange(LANES)[:, None]
    col = jnp.arange(hd)[None, :]
    head = col // HEAD_DIM
    src = jnp.where(head % 2 == 0, head // 2, HEAD_DIM + head // 2)
    return (lane == src).astype(BF16)


def dilated_combine(outs, lses, w_o, h, *, tm=512):
    b, s, d = h.shape
    hd = w_o.shape[0]

    def regrouped(arr):
        r = arr.shape[1]
        return pl.BlockSpec((None, r, tm // r, arr.shape[3]), lambda bi, i: (bi, 0, i, 0))

    return pl.pallas_call(
        _dil_combine_body,
        grid=(b, s // tm),
        in_specs=[regrouped(a) for a in (*outs, *lses)] + [
            pl.BlockSpec((2 * LANES, hd), lambda bi, i: (0, 0)),
            pl.BlockSpec(w_o.shape, lambda bi, i: (0, 0)),
            pl.BlockSpec((None, tm, d), lambda bi, i: (bi, i, 0))],
        out_specs=pl.BlockSpec((None, tm, d), lambda bi, i: (bi, i, 0)),
        out_shape=jax.ShapeDtypeStruct((b, s, d), F32),
        scratch_shapes=[pltpu.VMEM((2, hd // LANES, tm, LANES), F32), pltpu.VMEM((2, tm, LANES), F32)],
        compiler_params=_cparams(2, VMEM_COMBINE),
        name="dilated_combine",
    )(*outs, *lses, _stacked(_head_spread_matrix()), w_o, h)


def _diff_attn_body(qi_ref, ki_ref, q_ref, k_ref, v_ref, bias_ref, lam_ref, g_ref, o_ref,
                    m_ref, acc_ref, *, lambda_init):
    p = pl.program_id(1)
    qi = qi_ref[p]
    ki = ki_ref[p]
    nbatch, t, _ = q_ref.shape
    hw = 2 * HEAD_DIM
    nheads = q_ref.shape[2] // hw
    nt = (((1,), (1,)), ((), ()))
    low = lax.broadcasted_iota(jnp.int32, (1, hw), 1) < HEAD_DIM

    @pl.when(ki == 0)
    def _():
        m_ref[...] = jnp.full(m_ref.shape, MASK_VALUE, F32)
        acc_ref[...] = jnp.zeros_like(acc_ref)

    for bi in range(nbatch):
        for hh in range(nheads):
            hs = slice(hh * hw, (hh + 1) * hw)
            q = q_ref[bi, :, hs]
            k = k_ref[bi, :, hs]
            v1 = jnp.concatenate([v_ref[bi, :, hs], jnp.ones((t, hw), BF16)], axis=1)
            for mp in range(2):
                qm = jnp.where(low, q, 0) if mp == 0 else jnp.where(low, 0, q)
                s = lax.dot_general(qm, k, nt, preferred_element_type=F32) + bias_ref[hh, mp]
                m_old = m_ref[bi, hh, mp]
                m_new = jnp.maximum(m_old, jnp.max(s, axis=-1, keepdims=True))
                alpha = jnp.exp2(m_old - m_new)
                pr = jnp.exp2(s - jnp.concatenate([m_new] * (t // hw), axis=1))
                acc_ref[bi, hh, mp] = (jnp.concatenate([alpha, alpha], axis=1) * acc_ref[bi, hh, mp]
                                       + jnp.dot(pr.astype(BF16), v1, preferred_element_type=F32))
                m_ref[bi, hh, mp] = m_new

    @pl.when(ki == qi)
    def _():
        lam = (jnp.exp(jnp.sum(lam_ref[0:1, :] * lam_ref[1:2, :], axis=-1, keepdims=True))
               - jnp.exp(jnp.sum(lam_ref[2:3, :] * lam_ref[3:4, :], axis=-1, keepdims=True)) + lambda_init)
        gain = g_ref[...] * (1.0 - lambda_init)

        def finish(bi, carry):
            for hh in range(nheads):
                a0 = acc_ref[bi, hh, 0]
                a1 = acc_ref[bi, hh, 1]
                o = a0[:, :hw] / a0[:, hw:] - lam * (a1[:, :hw] / a1[:, hw:])
                o_ref[bi, :, hh * hw:(hh + 1) * hw] = _rms(o, gain).astype(o_ref.dtype)
            return carry

        lax.fori_loop(0, nbatch, finish, 0)


def diff_attention(qkv, bias, lam_vecs, subln, lambda_init):
    b, s, _ = qkv.shape
    t = DIFF_TILE
    nq = s // t
    n_delta = bias.shape[2]
    pairs = [(qi, ki) for qi in range(nq) for ki in range(qi + 1)]
    qi_arr = jnp.asarray([pq for pq, _ in pairs], jnp.int32)
    ki_arr = jnp.asarray([pk for _, pk in pairs], jnp.int32)
    hw = 2 * HEAD_DIM
    nh = DIFF_HEADS
    hps = DIFF_HEADS_PER_STEP
    ng = nh // hps
    gw = hps * hw
    grid_spec = pltpu.PrefetchScalarGridSpec(
        num_scalar_prefetch=2,
        grid=(ng, len(pairs)),
        in_specs=[
            pl.BlockSpec((b, t, gw), lambda h, p, qa, ka: (0, qa[p], h)),
            pl.BlockSpec((b, t, gw), lambda h, p, qa, ka: (0, ka[p], ng + h)),
            pl.BlockSpec((b, t, gw), lambda h, p, qa, ka: (0, ka[p], 2 * ng + h)),
            pl.BlockSpec((hps, 2, None, t, t),
                         lambda h, p, qa, ka: (h, 0, jnp.minimum(qa[p] - ka[p], n_delta - 1), 0, 0)),
            pl.BlockSpec((4, HEAD_DIM), lambda h, p, qa, ka: (0, 0)),
            pl.BlockSpec((1, hw), lambda h, p, qa, ka: (0, 0)),
        ],
        out_specs=pl.BlockSpec((b, t, gw), lambda h, p, qa, ka: (0, qa[p], h)),
        scratch_shapes=[pltpu.VMEM((b, hps, 2, t, hw), F32), pltpu.VMEM((b, hps, 2, t, 2 * hw), F32)],
    )
    return pl.pallas_call(
        functools.partial(_diff_attn_body, lambda_init=lambda_init),
        grid_spec=grid_spec,
        out_shape=jax.ShapeDtypeStruct((b, s, nh * hw), BF16),
        compiler_params=_cparams(2, VMEM_DIFF_ATTENTION),
        name="diff_attention",
    )(qi_arr, ki_arr, qkv, qkv, qkv, bias, lam_vecs, subln.reshape(1, hw))


def _silu(x):
    h = 0.5 * x
    return h + h * jnp.tanh(h)


def _conv_silu(cur, win_ref, shift_ref, w_ref, b_ref, cols):
    l = cur.shape[0]
    lagged = jnp.dot(shift_ref[...], win_ref[:, cols], preferred_element_type=F32)
    acc = b_ref[:, cols] + w_ref[SSM_CONV - 1:SSM_CONV, cols] * cur
    for k in range(SSM_CONV - 1):
        acc = acc + w_ref[k:k + 1, cols] * lagged[k * l:(k + 1) * l]
    return _silu(acc)


def _shift_matrix(l):
    row = jnp.arange((SSM_CONV - 1) * l)[:, None]
    col = jnp.arange(2 * l)[None, :]
    k, t = row // l, row % l
    return (col == l + t - (SSM_CONV - 1 - k)).astype(BF16)


def _spread(w, e_ref):
    hi = w.astype(BF16)
    lo = (w - hi.astype(F32)).astype(BF16)
    return jnp.dot(jnp.concatenate([hi, lo], axis=1), e_ref[...], preferred_element_type=F32)


def _stacked(e):
    return jnp.concatenate([e, e], axis=0)


def _cumsum_rows(a):
    rows = a.shape[0]
    ridx = lax.broadcasted_iota(jnp.int32, a.shape, 0)
    sh = 1
    while sh < rows:
        a = a + jnp.where(ridx >= sh, pltpu.roll(a, sh, 0), 0.0)
        sh *= 2
    return a


def _ssd_body(z_ref, x_ref, bm_ref, cm_ref, dt_ref, cw_ref, cb_ref, dtb_ref, alog_ref, dsk_ref, gn_ref, e_ref,
              sh_ref, y_ref, state_ref, win_ref):
    c = pl.program_id(1)
    l = SSM_CHUNK
    ns = SSM_D_STATE
    di = SSM_D_INNER
    gn = SSM_GROUPS * ns
    hg = SSM_HEADS // SSM_GROUPS
    gw = hg * SSM_HEAD_DIM
    tail = 2 * SUBLANES

    @pl.when(c == 0)
    def _():
        state_ref[...] = jnp.zeros_like(state_ref)
        win_ref[0:l, :] = jnp.zeros((l, win_ref.shape[1]), BF16)

    win_ref[l:, 0:di] = x_ref[...].astype(BF16)
    win_ref[l:, di:di + gn] = bm_ref[...].astype(BF16)
    win_ref[l:, di + gn:] = cm_ref[...].astype(BF16)
    xs = [_conv_silu(x_ref[:, g * gw:(g + 1) * gw], win_ref, sh_ref, cw_ref, cb_ref, slice(g * gw, (g + 1) * gw))
          for g in range(SSM_GROUPS)]
    bmat = _conv_silu(bm_ref[...], win_ref, sh_ref, cw_ref, cb_ref, slice(di, di + gn)).astype(BF16)
    cmat = _conv_silu(cm_ref[...], win_ref, sh_ref, cw_ref, cb_ref, slice(di + gn, di + 2 * gn)).astype(BF16)
    win_ref[l - tail:l, :] = win_ref[2 * l - tail:, :]

    dt = jax.nn.softplus(dt_ref[...] + dtb_ref[...])
    a_cs = _cumsum_rows(dt * (-jnp.exp(alog_ref[...])))
    a_cs_t = a_cs.T
    last = a_cs[l - 1:l, :]
    dt_e = _spread(dt, e_ref)
    exp_a_e = _spread(jnp.exp(a_cs), e_ref)
    state_decay_e = _spread(jnp.exp(last - a_cs), e_ref)

    causal = (lax.broadcasted_iota(jnp.int32, (l, l), 0) >= lax.broadcasted_iota(jnp.int32, (l, l), 1))
    lane_head = lax.broadcasted_iota(jnp.int32, (1, gw), 1) // SSM_HEAD_DIM
    tn_dims = (((0,), (0,)), ((), ()))
    nt_dims = (((1,), (1,)), ((), ()))

    for g in range(SSM_GROUPS):
        gs = slice(g * gw, (g + 1) * gw)
        cg = cmat[:, g * ns:(g + 1) * ns]
        bg = bmat[:, g * ns:(g + 1) * ns]
        xg = xs[g]
        xdt = xg * dt_e[:, gs]
        xdt_b = xdt.astype(BF16)
        cb = lax.dot_general(cg, bg, nt_dims, preferred_element_type=F32)
        weights, operands = [], []
        for j in range(hg):
            hh = hg * g + j
            seg = a_cs[:, hh:hh + 1] - a_cs_t[hh:hh + 1, :]
            decay = jnp.where(causal, jnp.exp(seg), 0.0)
            weights.append((cb * decay).astype(BF16))
            operands.append(jnp.where(lane_head == j, xdt_b, 0))
        y_diag = jnp.dot(jnp.concatenate(weights, axis=1), jnp.concatenate(operands, axis=0),
                         preferred_element_type=F32)
        prev = state_ref[g]
        y_off = jnp.dot(cg, prev.astype(BF16), preferred_element_type=F32) * exp_a_e[:, gs]
        new = lax.dot_general(bg, (xdt * state_decay_e[:, gs]).astype(BF16), tn_dims,
                              preferred_element_type=F32)
        state_ref[g] = prev * exp_a_e[l - 1:l, gs] + new
        y = y_diag + y_off + xg * dsk_ref[:, gs]
        zg = z_ref[:, gs]
        y = y * _silu(zg)
        y_ref[:, gs] = _rms(y, gn_ref[:, gs]).astype(y_ref.dtype)


def ssd_core(zx, dt_raw, conv_w, conv_b, dt_bias, a_log, d_skip, gate_norm):
    b, s, _ = zx.shape
    l = SSM_CHUNK
    di = SSM_D_INNER
    gn = SSM_GROUPS * SSM_D_STATE
    conv_dim = di + 2 * gn
    z_spec = pl.BlockSpec((None, l, di), lambda bi, c: (bi, c, 0))
    x_spec = pl.BlockSpec((None, l, di), lambda bi, c: (bi, c, 1))
    b_spec = pl.BlockSpec((None, l, gn), lambda bi, c: (bi, c, 2 * di // gn))
    c_spec = pl.BlockSpec((None, l, gn), lambda bi, c: (bi, c, 2 * di // gn + 1))

    def whole(shape):
        return pl.BlockSpec(shape, lambda bi, c: (0, 0))

    head_lanes = (jnp.arange(LANES)[:, None] == jnp.arange(di)[None, :] // SSM_HEAD_DIM).astype(BF16)
    return pl.pallas_call(
        _ssd_body,
        grid=(b, s // l),
        in_specs=[z_spec, x_spec, b_spec, c_spec,
                  pl.BlockSpec((None, l, LANES), lambda bi, c: (bi, c, 0)),
                  whole((SSM_CONV, conv_dim)), whole((1, conv_dim)),
                  whole((1, LANES)), whole((1, LANES)), whole((1, di)), whole((1, di)), whole((2 * LANES, di)),
                  whole(((SSM_CONV - 1) * l, 2 * l))],
        out_specs=pl.BlockSpec((None, l, di), lambda bi, c: (bi, c, 0)),
        out_shape=jax.ShapeDtypeStruct((b, s, di), BF16),
        scratch_shapes=[pltpu.VMEM((SSM_GROUPS, SSM_D_STATE, (SSM_HEADS // SSM_GROUPS) * SSM_HEAD_DIM), F32),
                        pltpu.VMEM((2 * l, conv_dim), BF16)],
        compiler_params=_cparams(2, VMEM_SSD),
        name="ssd_core",
    )(zx, zx, zx, zx, dt_raw, conv_w, conv_b, dt_bias, a_log, d_skip, gate_norm, _stacked(head_lanes), _shift_matrix(l))


def _dilated_layer(h, norm_g, w_qkv, w_o, dil_biases):
    b, s, d = h.shape
    width = 3 * DIL_HEADS * HEAD_DIM
    outs, lses = [], []
    for g, (_, dilation) in enumerate(DIL_PATTERNS):
        wide = dilation == 1
        qkv = norm_matmul(h, norm_g, w_qkv, n=width, tn=width, col_block=g, dilation=dilation,
                          rows=1024 if wide else 512, q_cols=width // 3,
                          vmem_bytes=VMEM_PROJECTION_WIDE if wide else VMEM_PROJECTION)
        o, lse = dilated_attention(qkv, dil_biases[g], dilation)
        outs.append(o)
        lses.append(lse)
    return dilated_combine(outs, lses, w_o, h)


def _diff_layer(h, norm_g, w_qkv, lam_q1, lam_k1, lam_q2, lam_k2, subln, w_o, bias, lambda_init):
    b, s, d = h.shape
    width = w_qkv.shape[1]
    qkv = norm_matmul(h, norm_g, w_qkv, n=width, tn=width, rows=1024, q_cols=width // 3,
                      vmem_bytes=VMEM_PROJECTION_WIDE).reshape(b, s, width)
    lam_vecs = jnp.stack([lam_q1, lam_k1, lam_q2, lam_k2])
    o = diff_attention(qkv, bias, lam_vecs, subln, lambda_init)
    return matmul_residual(o.reshape(b * s, -1), w_o, h.reshape(b * s, d)).reshape(b, s, d)


def _ssd_layer(h, norm_g, w_in, conv_w, conv_b, dt_bias, a_log, d_skip, gate_norm, w_out):
    b, s, d = h.shape
    wide = SSM_D_INNER + conv_w.shape[1]
    pad = LANES - SSM_HEADS
    zx = norm_matmul(h, norm_g, w_in, n=wide, tn=wide // 2, out_dtype=F32).reshape(b, s, wide)
    dt_raw = norm_matmul(h, norm_g, w_in, n=LANES, tn=LANES, col_block=wide // LANES,
                         out_dtype=F32).reshape(b, s, LANES)
    y = ssd_core(zx, dt_raw, conv_w, conv_b.reshape(1, -1),
                 jnp.pad(dt_bias, (0, pad)).reshape(1, LANES), jnp.pad(a_log, (0, pad)).reshape(1, LANES),
                 jnp.repeat(d_skip, SSM_HEAD_DIM).reshape(1, SSM_D_INNER), gate_norm.reshape(1, SSM_D_INNER))
    return matmul_residual(y.reshape(b * s, -1), w_out, h.reshape(b * s, d)).reshape(b, s, d)


def _diff_lambda_init(layer):
    return 0.8 - 0.6 * math.exp(-0.3 * layer)


def kernel(x, rel_bias,
           l0_mix_norm, l0_dil_w_qkv, l0_dil_w_o, l0_mlp_norm, l0_mlp_w_up, l0_mlp_w_down,
           l1_mix_norm, l1_diff_w_qkv, l1_diff_lam_q1, l1_diff_lam_k1, l1_diff_lam_q2, l1_diff_lam_k2,
           l1_diff_subln, l1_diff_w_o, l1_mlp_norm, l1_mlp_w_up, l1_mlp_w_down,
           l2_mix_norm, l2_ssm_w_in, l2_ssm_conv_w, l2_ssm_conv_b, l2_ssm_dt_bias, l2_ssm_A_log, l2_ssm_D,
           l2_ssm_gate_norm, l2_ssm_w_out, l2_mlp_norm, l2_mlp_w_up, l2_mlp_w_down,
           l3_mix_norm, l3_dil_w_qkv, l3_dil_w_o, l3_mlp_norm, l3_mlp_w_up, l3_mlp_w_down,
           final_norm):
    b, s, d = x.shape
    dil_biases = [dilated_bias(rel_bias, dilation) for _, dilation in DIL_PATTERNS]
    dif_bias = diff_bias(rel_bias, s)

    def mlp(h, g, w_up, w_down, final_gain=None):
        return mlp_block(h.reshape(b * s, d), g, w_up, w_down, final_gain).reshape(b, s, d)

    h = _dilated_layer(x, l0_mix_norm, l0_dil_w_qkv, l0_dil_w_o, dil_biases)
    h = mlp(h, l0_mlp_norm, l0_mlp_w_up, l0_mlp_w_down)
    h = _diff_layer(h, l1_mix_norm, l1_diff_w_qkv, l1_diff_lam_q1, l1_diff_lam_k1, l1_diff_lam_q2,
                    l1_diff_lam_k2, l1_diff_subln, l1_diff_w_o, dif_bias, _diff_lambda_init(1))
    h = mlp(h, l1_mlp_norm, l1_mlp_w_up, l1_mlp_w_down)
    h = _ssd_layer(h, l2_mix_norm, l2_ssm_w_in, l2_ssm_conv_w, l2_ssm_conv_b, l2_ssm_dt_bias, l2_ssm_A_log,
                   l2_ssm_D, l2_ssm_gate_norm, l2_ssm_w_out)
    h = mlp(h, l2_mlp_norm, l2_mlp_w_up, l2_mlp_w_down)
    h = _dilated_layer(h, l3_mix_norm, l3_dil_w_qkv, l3_dil_w_o, dil_biases)
    return mlp(h, l3_mlp_norm, l3_mlp_w_up, l3_mlp_w_down, final_norm)
```

```python
import functools
import math

import jax
import jax.numpy as jnp
from jax import lax
from jax.experimental import pallas as pl
from jax.experimental.pallas import tpu as pltpu

F32 = jnp.float32
BF16 = jnp.bfloat16

NORM_EPS = 1e-5
HEAD_DIM = 64
DIL_PATTERNS = ((128, 1), (512, 4), (2048, 16))
DIL_HEADS = 16
DIL_STEPS = 128
DIFF_HEADS = 8
N_BUCKETS = 32
MAX_DISTANCE = 2048
SSM_HEADS = 32
SSM_GROUPS = 8
SSM_D_STATE = 128
SSM_HEAD_DIM = 64
SSM_D_INNER = SSM_HEADS * SSM_HEAD_DIM
SSM_CONV = 4
SSM_CHUNK = 128
LANES = 128
SUBLANES = 8
MASK_VALUE = -1e30
LOG2E = math.log2(math.e)
LN2 = math.log(2.0)
Q_SCALE = LOG2E * HEAD_DIM ** -0.5
MIB = 1024 * 1024
VMEM_PROJECTION = 34 * MIB
VMEM_PROJECTION_WIDE = 42 * MIB
VMEM_MLP = 52 * MIB
VMEM_RESIDUAL = 36 * MIB
VMEM_COMBINE = 40 * MIB
VMEM_DIFF_ATTENTION = 36 * MIB
VMEM_SSD = 20 * MIB
VMEM_DIFF_BIAS = 14 * MIB
VMEM_DIL_ATTENTION = 40 * MIB
VMEM_SMALL = 6 * MIB

DIL_BLOCKS_PER_STEP = 4
DIFF_TILE = 512
DIFF_HEADS_PER_STEP = 2


def _bucket_thresholds():
    exact = N_BUCKETS // 2

    def bucket(d):
        if d < exact:
            return d
        big = exact + int(math.log(d / exact) / math.log(MAX_DISTANCE / exact) * (N_BUCKETS - exact))
        return min(big, N_BUCKETS - 1)

    thr = {}
    d = 0
    for b in range(1, N_BUCKETS):
        while bucket(d) < b:
            d += 1
        thr[b] = d
    return thr, bucket


_BUCKET_THR, _bucket_of = _bucket_thresholds()


def _cparams(grid_rank, vmem_bytes):
    return pltpu.CompilerParams(dimension_semantics=("arbitrary",) * grid_rank, vmem_limit_bytes=vmem_bytes)


def _rms(x, gain):
    ms = jnp.mean(x * x, axis=-1, keepdims=True)
    return x * lax.rsqrt(ms + NORM_EPS) * gain


def _norm_matmul_body(x_ref, g_ref, w_ref, o_ref, *slab_ref, r, valid_cols, q_cols):
    rows, d = x_ref.shape
    if r == 1:
        x = x_ref[...]
    else:
        slab = slab_ref[0]
        per = rows // r
        for c in range(d // LANES):
            slab[c] = x_ref[:, c * LANES:(c + 1) * LANES]
        x = jnp.concatenate(
            [jnp.concatenate([slab[c, pl.ds(rho, per, stride=r), :] for c in range(d // LANES)], axis=1)
             for rho in range(r)], axis=0)
    xn = _rms(x, g_ref[...]).astype(BF16)
    w = w_ref[...]
    if valid_cols is not None:
        w = jnp.where(lax.broadcasted_iota(jnp.int32, (1, w.shape[1]), 1) < valid_cols, w, 0.0)
    res = jnp.dot(xn, w.astype(BF16), preferred_element_type=F32)
    if q_cols:
        res = jnp.concatenate([res[:, :q_cols] * Q_SCALE, res[:, q_cols:]], axis=1)
    o_ref[...] = res.astype(o_ref.dtype).reshape(o_ref.shape)


def norm_matmul(x, gain, w, *, n, tn, col_block=0, dilation=1, rows=512, out_dtype=BF16,
                vmem_bytes=VMEM_PROJECTION, q_cols=0):
    first_col = col_block * tn
    valid_cols = w.shape[1] - first_col if first_col + n > w.shape[1] else None
    assert valid_cols is None or n == tn
    b, s, d = x.shape
    r = dilation
    per = rows // r
    scratch = [pltpu.VMEM((d // LANES, rows, LANES), F32)] if r > 1 else []
    return pl.pallas_call(
        functools.partial(_norm_matmul_body, r=r, valid_cols=valid_cols, q_cols=q_cols),
        grid=(n // tn, b, s // rows),
        in_specs=[
            pl.BlockSpec((None, rows, d), lambda j, bi, i: (bi, i, 0)),
            pl.BlockSpec((1, d), lambda j, bi, i: (0, 0)),
            pl.BlockSpec((d, tn), lambda j, bi, i: (0, col_block + j), pipeline_mode=pl.Buffered(1)),
        ],
        out_specs=pl.BlockSpec((None, r, per, tn), lambda j, bi, i: (bi, 0, i, j)),
        out_shape=jax.ShapeDtypeStruct((b, r, s // r, n), out_dtype),
        scratch_shapes=scratch,
        compiler_params=_cparams(3, vmem_bytes),
        name="norm_matmul",
    )(x, gain.reshape(1, d), w)


def _matmul_res_body(a_ref, w_ref, r_ref, o_ref):
    o_ref[...] = r_ref[...] + jnp.dot(a_ref[...], w_ref[...].astype(BF16), preferred_element_type=F32)


def matmul_residual(a, w, res, *, tm=512):
    m, k = a.shape
    n = w.shape[1]
    return pl.pallas_call(
        _matmul_res_body,
        grid=(m // tm,),
        in_specs=[
            pl.BlockSpec((tm, k), lambda i: (i, 0)),
            pl.BlockSpec((k, n), lambda i: (0, 0)),
            pl.BlockSpec((tm, n), lambda i: (i, 0)),
        ],
        out_specs=pl.BlockSpec((tm, n), lambda i: (i, 0)),
        out_shape=jax.ShapeDtypeStruct((m, n), F32),
        compiler_params=_cparams(1, VMEM_RESIDUAL),
        name="matmul_residual",
    )(a, w, res)


def _mlp_body(h_ref, g_ref, wu_ref, wd_ref, fg_ref, o_ref, *, final_norm):
    f = pl.program_id(1)
    last = pl.num_programs(1) - 1

    def contribution():
        xn = _rms(h_ref[...], g_ref[...]).astype(BF16)
        u = jnp.maximum(jnp.dot(xn, wu_ref[...].astype(BF16), preferred_element_type=F32), 0.0)
        return jnp.dot((u * u).astype(BF16), wd_ref[...].astype(BF16), preferred_element_type=F32)

    @pl.when(f == 0)
    def _():
        o_ref[...] = h_ref[...] + contribution()

    @pl.when((f > 0) & (f < last))
    def _():
        o_ref[...] += contribution()

    @pl.when(f == last)
    def _():
        out = o_ref[...] + contribution()
        if final_norm:
            out = _rms(out, fg_ref[...])
        o_ref[...] = out


def mlp_block(h, gain, w_up, w_down, final_gain=None, *, tm=1024, tf=1024):
    m, d = h.shape
    f = w_up.shape[1]
    assert f // tf >= 2
    final_norm = final_gain is not None
    fg = (final_gain if final_norm else gain).reshape(1, d)
    return pl.pallas_call(
        functools.partial(_mlp_body, final_norm=final_norm),
        grid=(m // tm, f // tf),
        in_specs=[
            pl.BlockSpec((tm, d), lambda i, j: (i, 0)),
            pl.BlockSpec((1, d), lambda i, j: (0, 0)),
            pl.BlockSpec((d, tf), lambda i, j: (0, j)),
            pl.BlockSpec((tf, d), lambda i, j: (j, 0)),
            pl.BlockSpec((1, d), lambda i, j: (0, 0)),
        ],
        out_specs=pl.BlockSpec((tm, d), lambda i, j: (i, 0)),
        out_shape=jax.ShapeDtypeStruct((m, d), F32),
        compiler_params=_cparams(2, VMEM_MLP),
        name="mlp_block",
    )(h, gain.reshape(1, d), w_up, w_down, fg)


def _bias_from_dist(dist, col_of_bucket, d_lo, d_hi):
    val = jnp.full(dist.shape, col_of_bucket(_bucket_of(d_lo)) * LOG2E, F32)
    for b in range(1, N_BUCKETS):
        t = _BUCKET_THR[b]
        if d_lo < t <= d_hi:
            val = jnp.where(dist >= t, col_of_bucket(b) * LOG2E, val)
    return val


def _dil_bias_body(tab_ref, o_ref, *, dilation):
    c = pl.program_id(0)
    n = DIL_STEPS
    qi = lax.broadcasted_iota(jnp.int32, (n, 2 * n), 0)
    kj = lax.broadcasted_iota(jnp.int32, (n, 2 * n), 1)
    steps = n + qi - kj
    band = (steps >= 0) & (steps <= n)
    dist = jnp.clip(steps, 0, n) * dilation
    val = _bias_from_dist(dist, lambda b: tab_ref[b, c], 0, n * dilation)
    later = jnp.where(band, val, MASK_VALUE)
    o_ref[1] = later
    o_ref[0] = jnp.where(kj >= n, later, MASK_VALUE)


def dilated_bias(rel_bias, dilation):
    n = DIL_STEPS
    return pl.pallas_call(
        functools.partial(_dil_bias_body, dilation=dilation),
        grid=(DIL_HEADS,),
        in_specs=[pl.BlockSpec(memory_space=pltpu.SMEM)],
        out_specs=pl.BlockSpec((2, None, n, 2 * n), lambda c: (0, c, 0, 0)),
        out_shape=jax.ShapeDtypeStruct((2, DIL_HEADS, n, 2 * n), F32),
        compiler_params=_cparams(1, VMEM_SMALL),
        name="dilated_bias",
    )(rel_bias)


def _diff_bias_body(tab_ref, o_ref, *, n_delta):
    m = pl.program_id(0)
    h = pl.program_id(1)
    c = m * DIFF_HEADS + h
    t = DIFF_TILE
    qi = lax.broadcasted_iota(jnp.int32, (t, t), 0)
    kj = lax.broadcasted_iota(jnp.int32, (t, t), 1)
    for delta in range(n_delta):
        dist = delta * t + qi - kj
        d_lo = max(delta * t - (t - 1), 0)
        d_hi = delta * t + t - 1
        val = _bias_from_dist(dist, lambda b: tab_ref[b, c], d_lo, d_hi)
        if delta == 0:
            val = jnp.where(dist >= 0, val, MASK_VALUE)
        o_ref[delta] = val


def _diff_first_constant_delta():
    t = DIFF_TILE
    last_thr = _BUCKET_THR[N_BUCKETS - 1]
    delta = 1
    while delta * t - (t - 1) < last_thr:
        delta += 1
    return delta


def _diff_n_delta(s):
    return min(_diff_first_constant_delta() + 1, s // DIFF_TILE)


def diff_bias(rel_bias, s):
    t = DIFF_TILE
    n_delta = _diff_n_delta(s)
    return pl.pallas_call(
        functools.partial(_diff_bias_body, n_delta=n_delta),
        grid=(2, DIFF_HEADS),
        in_specs=[pl.BlockSpec(memory_space=pltpu.SMEM)],
        out_specs=pl.BlockSpec((None, None, n_delta, t, t), lambda m, h: (h, m, 0, 0, 0)),
        out_shape=jax.ShapeDtypeStruct((DIFF_HEADS, 2, n_delta, t, t), F32),
        compiler_params=_cparams(2, VMEM_DIFF_BIAS),
        name="diff_bias",
    )(rel_bias)


def _dil_attn_body(q_ref, kc_ref, vc_ref, bias_ref, o_ref, lse_ref, kp_ref, vp_ref):
    n = DIL_STEPS
    nblk = q_ref.shape[0] // n
    hd = DIL_HEADS * HEAD_DIM
    nt = (((1,), (1,)), ((), ()))
    lane = lax.broadcasted_iota(jnp.int32, (1, LANES), 1)
    low = lane < HEAD_DIM
    ones_low = jnp.where(low, 1, 0).astype(BF16) + jnp.zeros((2 * n, LANES), BF16)
    ones_high = jnp.where(low, 0, 1).astype(BF16) + jnp.zeros((2 * n, LANES), BF16)

    @pl.when(pl.program_id(2) == 0)
    def _():
        kp_ref[...] = jnp.zeros_like(kp_ref)
        vp_ref[...] = jnp.zeros_like(vp_ref)

    npairs = hd // LANES
    first_variant = jnp.minimum(pl.program_id(2), 1)

    def keys_values(blk, cs):
        if blk == 0:
            return (jnp.concatenate([kp_ref[:, cs], kc_ref[0:n, cs]], axis=0),
                    jnp.concatenate([vp_ref[:, cs], vc_ref[0:n, cs]], axis=0))
        return kc_ref[(blk - 1) * n:(blk + 1) * n, cs], vc_ref[(blk - 1) * n:(blk + 1) * n, cs]

    lse_tiles = [jnp.zeros((n, LANES), F32) for _ in range(nblk)]
    for pr in range(npairs):
        cs = slice(pr * LANES, (pr + 1) * LANES)
        for blk in range(nblk):
            rows = slice(blk * n, (blk + 1) * n)
            k2, v2 = keys_values(blk, cs)
            q2 = q_ref[rows, cs]
            qs = jnp.concatenate([jnp.where(low, q2, 0), jnp.where(low, 0, q2)], axis=0)
            variant = first_variant if blk == 0 else 1
            bias = bias_ref[variant, 2 * pr:2 * pr + 2].reshape(2 * n, 2 * n)
            s = lax.dot_general(qs, k2, nt, preferred_element_type=F32) + bias
            m = jnp.max(jnp.maximum(s[:, :n], s[:, n:]), axis=-1, keepdims=True)
            p = jnp.exp2(s - m).astype(BF16)
            w = jnp.concatenate(
                [jnp.concatenate([jnp.where(low, v2, 0), ones_low], axis=1),
                 jnp.concatenate([jnp.where(low, 0, v2), ones_high], axis=1)], axis=0)
            r = jnp.dot(jnp.concatenate([p[:n], p[n:]], axis=1), w, preferred_element_type=F32)
            den = r[:, LANES:]
            o_ref[rows, cs] = (r[:, :LANES] / den).astype(o_ref.dtype)
            lse = jnp.where(low, m[:n], m[n:]) * LN2 + jnp.log(den)
            lse_tiles[blk] = jnp.where((lane == pr) | (lane == HEAD_DIM + pr), lse, lse_tiles[blk])
    for blk in range(nblk):
        lse_ref[blk * n:(blk + 1) * n, :] = lse_tiles[blk]
    kp_ref[...] = kc_ref[(nblk - 1) * n:, :]
    vp_ref[...] = vc_ref[(nblk - 1) * n:, :]


def dilated_attention(qkv, bias, dilation):
    b, r, length, _ = qkv.shape
    n = DIL_STEPS
    hd = DIL_HEADS * HEAD_DIM
    rows = min(DIL_BLOCKS_PER_STEP * n, length)
    assert length % rows == 0

    def cur(col):
        return pl.BlockSpec((None, None, rows, hd), lambda bi, rho, i: (bi, rho, i, col))

    return pl.pallas_call(
        _dil_attn_body,
        grid=(b, r, length // rows),
        in_specs=[cur(0), cur(1), cur(2),
                  pl.BlockSpec((2, DIL_HEADS, n, 2 * n), lambda bi, rho, i: (0, 0, 0, 0))],
        out_specs=[pl.BlockSpec((None, None, rows, hd), lambda bi, rho, i: (bi, rho, i, 0)),
                   pl.BlockSpec((None, None, rows, LANES), lambda bi, rho, i: (bi, rho, i, 0))],
        out_shape=[jax.ShapeDtypeStruct((b, r, length, hd), BF16),
                   jax.ShapeDtypeStruct((b, r, length, LANES), F32)],
        scratch_shapes=[pltpu.VMEM((n, hd), BF16), pltpu.VMEM((n, hd), BF16)],
        compiler_params=_cparams(3, VMEM_DIL_ATTENTION),
        name=f"dilated_attention_r{dilation}",
    )(qkv, qkv, qkv, bias)


def _dil_combine_body(o1_ref, o2_ref, o3_ref, l1_ref, l2_ref, l3_ref, e_ref, w_ref, h_ref, out_ref,
                      oslab_ref, lslab_ref):
    tm = out_ref.shape[0]
    nslab = oslab_ref.shape[1]

    def token_order(o_ref, l_ref, k):
        r = o_ref.shape[0]
        if r == 1:
            return o_ref[0].astype(F32), l_ref[0]
        per = tm // r
        for rho in range(r):
            blk = o_ref[rho].astype(F32)
            for c in range(nslab):
                oslab_ref[k, c, pl.ds(rho, per, stride=r), :] = blk[:, c * LANES:(c + 1) * LANES]
            lslab_ref[k, pl.ds(rho, per, stride=r), :] = l_ref[rho]
        return jnp.concatenate([oslab_ref[k, c] for c in range(nslab)], axis=1), lslab_ref[k]

    o1, l1 = token_order(o1_ref, l1_ref, 0)
    o2, l2 = token_order(o2_ref, l2_ref, 0)
    o3, l3 = token_order(o3_ref, l3_ref, 1)
    mx = jnp.maximum(jnp.maximum(l1, l2), l3)
    e1, e2, e3 = jnp.exp(l1 - mx), jnp.exp(l2 - mx), jnp.exp(l3 - mx)
    inv = 1.0 / (e1 + e2 + e3)

    mix = o3 + _spread(e1 * inv, e_ref) * (o1 - o3) + _spread(e2 * inv, e_ref) * (o2 - o3)
    out_ref[...] = h_ref[...] + jnp.dot(mix.astype(BF16), w_ref[...].astype(BF16), preferred_element_type=F32)


def _head_spread_matrix():
    hd = DIL_HEADS * HEAD_DIM
    lane = jnp.arange(LANES)[:, None]
    col = jnp.arange(hd)[None, :]
    head = col // HEAD_DIM
    src = jnp.where(head % 2 == 0, head // 2, HEAD_DIM + head // 2)
    return (lane == src).astype(BF16)


def dilated_combine(outs, lses, w_o, h, *, tm=512):
    b, s, d = h.shape
    hd = w_o.shape[0]

    def regrouped(arr):
        r = arr.shape[1]
        return pl.BlockSpec((None, r, tm // r, arr.shape[3]), lambda bi, i: (bi, 0, i, 0))

    return pl.pallas_call(
        _dil_combine_body,
        grid=(b, s // tm),
        in_specs=[regrouped(a) for a in (*outs, *lses)] + [
            pl.BlockSpec((2 * LANES, hd), lambda bi, i: (0, 0)),
            pl.BlockSpec(w_o.shape, lambda bi, i: (0, 0)),
            pl.BlockSpec((None, tm, d), lambda bi, i: (bi, i, 0))],
        out_specs=pl.BlockSpec((None, tm, d), lambda bi, i: (bi, i, 0)),
        out_shape=jax.ShapeDtypeStruct((b, s, d), F32),
        scratch_shapes=[pltpu.VMEM((2, hd // LANES, tm, LANES), F32), pltpu.VMEM((2, tm, LANES), F32)],
        compiler_params=_cparams(2, VMEM_COMBINE),
        name="dilated_combine",
    )(*outs, *lses, _stacked(_head_spread_matrix()), w_o, h)


def _diff_attn_body(qi_ref, ki_ref, q_ref, k_ref, v_ref, bias_ref, lam_ref, g_ref, o_ref,
                    m_ref, acc_ref, *, lambda_init):
    p = pl.program_id(1)
    qi = qi_ref[p]
    ki = ki_ref[p]
    nbatch, t, _ = q_ref.shape
    hw = 2 * HEAD_DIM
    nheads = q_ref.shape[2] // hw
    nt = (((1,), (1,)), ((), ()))
    low = lax.broadcasted_iota(jnp.int32, (1, hw), 1) < HEAD_DIM

    @pl.when(ki == 0)
    def _():
        m_ref[...] = jnp.full(m_ref.shape, MASK_VALUE, F32)
        acc_ref[...] = jnp.zeros_like(acc_ref)

    for bi in range(nbatch):
        for hh in range(nheads):
            hs = slice(hh * hw, (hh + 1) * hw)
            q = q_ref[bi, :, hs]
            k = k_ref[bi, :, hs]
            v1 = jnp.concatenate([v_ref[bi, :, hs], jnp.ones((t, hw), BF16)], axis=1)
            for mp in range(2):
                qm = jnp.where(low, q, 0) if mp == 0 else jnp.where(low, 0, q)
                s = lax.dot_general(qm, k, nt, preferred_element_type=F32) + bias_ref[hh, mp]
                m_old = m_ref[bi, hh, mp]
                m_new = jnp.maximum(m_old, jnp.max(s, axis=-1, keepdims=True))
                alpha = jnp.exp2(m_old - m_new)
                pr = jnp.exp2(s - jnp.concatenate([m_new] * (t // hw), axis=1))
                acc_ref[bi, hh, mp] = (jnp.concatenate([alpha, alpha], axis=1) * acc_ref[bi, hh, mp]
                                       + jnp.dot(pr.astype(BF16), v1, preferred_element_type=F32))
                m_ref[bi, hh, mp] = m_new

    @pl.when(ki == qi)
    def _():
        lam = (jnp.exp(jnp.sum(lam_ref[0:1, :] * lam_ref[1:2, :], axis=-1, keepdims=True))
               - jnp.exp(jnp.sum(lam_ref[2:3, :] * lam_ref[3:4, :], axis=-1, keepdims=True)) + lambda_init)
        gain = g_ref[...] * (1.0 - lambda_init)

        def finish(bi, carry):
            for hh in range(nheads):
                a0 = acc_ref[bi, hh, 0]
                a1 = acc_ref[bi, hh, 1]
                o = a0[:, :hw] / a0[:, hw:] - lam * (a1[:, :hw] / a1[:, hw:])
                o_ref[bi, :, hh * hw:(hh + 1) * hw] = _rms(o, gain).astype(o_ref.dtype)
            return carry

        lax.fori_loop(0, nbatch, finish, 0)


def diff_attention(qkv, bias, lam_vecs, subln, lambda_init):
    b, s, _ = qkv.shape
    t = DIFF_TILE
    nq = s // t
    n_delta = bias.shape[2]
    pairs = [(qi, ki) for qi in range(nq) for ki in range(qi + 1)]
    qi_arr = jnp.asarray([pq for pq, _ in pairs], jnp.int32)
    ki_arr = jnp.asarray([pk for _, pk in pairs], jnp.int32)
    hw = 2 * HEAD_DIM
    nh = DIFF_HEADS
    hps = DIFF_HEADS_PER_STEP
    ng = nh // hps
    gw = hps * hw
    grid_spec = pltpu.PrefetchScalarGridSpec(
        num_scalar_prefetch=2,
        grid=(ng, len(pairs)),
        in_specs=[
            pl.BlockSpec((b, t, gw), lambda h, p, qa, ka: (0, qa[p], h)),
            pl.BlockSpec((b, t, gw), lambda h, p, qa, ka: (0, ka[p], ng + h)),
            pl.BlockSpec((b, t, gw), lambda h, p, qa, ka: (0, ka[p], 2 * ng + h)),
            pl.BlockSpec((hps, 2, None, t, t),
                         lambda h, p, qa, ka: (h, 0, jnp.minimum(qa[p] - ka[p], n_delta - 1), 0, 0)),
            pl.BlockSpec((4, HEAD_DIM), lambda h, p, qa, ka: (0, 0)),
            pl.BlockSpec((1, hw), lambda h, p, qa, ka: (0, 0)),
        ],
        out_specs=pl.BlockSpec((b, t, gw), lambda h, p, qa, ka: (0, qa[p], h)),
        scratch_shapes=[pltpu.VMEM((b, hps, 2, t, hw), F32), pltpu.VMEM((b, hps, 2, t, 2 * hw), F32)],
    )
    return pl.pallas_call(
        functools.partial(_diff_attn_body, lambda_init=lambda_init),
        grid_spec=grid_spec,
        out_shape=jax.ShapeDtypeStruct((b, s, nh * hw), BF16),
        compiler_params=_cparams(2, VMEM_DIFF_ATTENTION),
        name="diff_attention",
    )(qi_arr, ki_arr, qkv, qkv, qkv, bias, lam_vecs, subln.reshape(1, hw))


def _silu(x):
    h = 0.5 * x
    return h + h * jnp.tanh(h)


def _conv_silu(cur, win_ref, shift_ref, w_ref, b_ref, cols):
    l = cur.shape[0]
    lagged = jnp.dot(shift_ref[...], win_ref[:, cols], preferred_element_type=F32)
    acc = b_ref[:, cols] + w_ref[SSM_CONV - 1:SSM_CONV, cols] * cur
    for k in range(SSM_CONV - 1):
        acc = acc + w_ref[k:k + 1, cols] * lagged[k * l:(k + 1) * l]
    return _silu(acc)


def _shift_matrix(l):
    row = jnp.arange((SSM_CONV - 1) * l)[:, None]
    col = jnp.arange(2 * l)[None, :]
    k, t = row // l, row % l
    return (col == l + t - (SSM_CONV - 1 - k)).astype(BF16)


def _spread(w, e_ref):
    hi = w.astype(BF16)
    lo = (w - hi.astype(F32)).astype(BF16)
    return jnp.dot(jnp.concatenate([hi, lo], axis=1), e_ref[...], preferred_element_type=F32)


def _stacked(e):
    return jnp.concatenate([e, e], axis=0)


def _cumsum_rows(a):
    rows = a.shape[0]
    ridx = lax.broadcasted_iota(jnp.int32, a.shape, 0)
    sh = 1
    while sh < rows:
        a = a + jnp.where(ridx >= sh, pltpu.roll(a, sh, 0), 0.0)
        sh *= 2
    return a


def _ssd_body(z_ref, x_ref, bm_ref, cm_ref, dt_ref, cw_ref, cb_ref, dtb_ref, alog_ref, dsk_ref, gn_ref, e_ref,
              sh_ref, y_ref, state_ref, win_ref):
    c = pl.program_id(1)
    l = SSM_CHUNK
    ns = SSM_D_STATE
    di = SSM_D_INNER
    gn = SSM_GROUPS * ns
    hg = SSM_HEADS // SSM_GROUPS
    gw = hg * SSM_HEAD_DIM
    tail = 2 * SUBLANES

    @pl.when(c == 0)
    def _():
        state_ref[...] = jnp.zeros_like(state_ref)
        win_ref[0:l, :] = jnp.zeros((l, win_ref.shape[1]), BF16)

    win_ref[l:, 0:di] = x_ref[...].astype(BF16)
    win_ref[l:, di:di + gn] = bm_ref[...].astype(BF16)
    win_ref[l:, di + gn:] = cm_ref[...].astype(BF16)
    xs = [_conv_silu(x_ref[:, g * gw:(g + 1) * gw], win_ref, sh_ref, cw_ref, cb_ref, slice(g * gw, (g + 1) * gw))
          for g in range(SSM_GROUPS)]
    bmat = _conv_silu(bm_ref[...], win_ref, sh_ref, cw_ref, cb_ref, slice(di, di + gn)).astype(BF16)
    cmat = _conv_silu(cm_ref[...], win_ref, sh_ref, cw_ref, cb_ref, slice(di + gn, di + 2 * gn)).astype(BF16)
    win_ref[l - tail:l, :] = win_ref[2 * l - tail:, :]

    dt = jax.nn.softplus(dt_ref[...] + dtb_ref[...])
    a_cs = _cumsum_rows(dt * (-jnp.exp(alog_ref[...])))
    a_cs_t = a_cs.T
    last = a_cs[l - 1:l, :]
    dt_e = _spread(dt, e_ref)
    exp_a_e = _spread(jnp.exp(a_cs), e_ref)
    state_decay_e = _spread(jnp.exp(last - a_cs), e_ref)

    causal = (lax.broadcasted_iota(jnp.int32, (l, l), 0) >= lax.broadcasted_iota(jnp.int32, (l, l), 1))
    lane_head = lax.broadcasted_iota(jnp.int32, (1, gw), 1) // SSM_HEAD_DIM
    tn_dims = (((0,), (0,)), ((), ()))
    nt_dims = (((1,), (1,)), ((), ()))

    for g in range(SSM_GROUPS):
        gs = slice(g * gw, (g + 1) * gw)
        cg = cmat[:, g * ns:(g + 1) * ns]
        bg = bmat[:, g * ns:(g + 1) * ns]
        xg = xs[g]
        xdt = xg * dt_e[:, gs]
        xdt_b = xdt.astype(BF16)
        cb = lax.dot_general(cg, bg, nt_dims, preferred_element_type=F32)
        weights, operands = [], []
        for j in range(hg):
            hh = hg * g + j
            seg = a_cs[:, hh:hh + 1] - a_cs_t[hh:hh + 1, :]
            decay = jnp.where(causal, jnp.exp(seg), 0.0)
            weights.append((cb * decay).astype(BF16))
            operands.append(jnp.where(lane_head == j, xdt_b, 0))
        y_diag = jnp.dot(jnp.concatenate(weights, axis=1), jnp.concatenate(operands, axis=0),
                         preferred_element_type=F32)
        prev = state_ref[g]
        y_off = jnp.dot(cg, prev.astype(BF16), preferred_element_type=F32) * exp_a_e[:, gs]
        new = lax.dot_general(bg, (xdt * state_decay_e[:, gs]).astype(BF16), tn_dims,
                              preferred_element_type=F32)
        state_ref[g] = prev * exp_a_e[l - 1:l, gs] + new
        y = y_diag + y_off + xg * dsk_ref[:, gs]
        zg = z_ref[:, gs]
        y = y * _silu(zg)
        y_ref[:, gs] = _rms(y, gn_ref[:, gs]).astype(y_ref.dtype)


def ssd_core(zx, dt_raw, conv_w, conv_b, dt_bias, a_log, d_skip, gate_norm):
    b, s, _ = zx.shape
    l = SSM_CHUNK
    di = SSM_D_INNER
    gn = SSM_GROUPS * SSM_D_STATE
    conv_dim = di + 2 * gn
    z_spec = pl.BlockSpec((None, l, di), lambda bi, c: (bi, c, 0))
    x_spec = pl.BlockSpec((None, l, di), lambda bi, c: (bi, c, 1))
    b_spec = pl.BlockSpec((None, l, gn), lambda bi, c: (bi, c, 2 * di // gn))
    c_spec = pl.BlockSpec((None, l, gn), lambda bi, c: (bi, c, 2 * di // gn + 1))

    def whole(shape):
        return pl.BlockSpec(shape, lambda bi, c: (0, 0))

    head_lanes = (jnp.arange(LANES)[:, None] == jnp.arange(di)[None, :] // SSM_HEAD_DIM).astype(BF16)
    return pl.pallas_call(
        _ssd_body,
        grid=(b, s // l),
        in_specs=[z_spec, x_spec, b_spec, c_spec,
                  pl.BlockSpec((None, l, LANES), lambda bi, c: (bi, c, 0)),
                  whole((SSM_CONV, conv_dim)), whole((1, conv_dim)),
                  whole((1, LANES)), whole((1, LANES)), whole((1, di)), whole((1, di)), whole((2 * LANES, di)),
                  whole(((SSM_CONV - 1) * l, 2 * l))],
        out_specs=pl.BlockSpec((None, l, di), lambda bi, c: (bi, c, 0)),
        out_shape=jax.ShapeDtypeStruct((b, s, di), BF16),
        scratch_shapes=[pltpu.VMEM((SSM_GROUPS, SSM_D_STATE, (SSM_HEADS // SSM_GROUPS) * SSM_HEAD_DIM), F32),
                        pltpu.VMEM((2 * l, conv_dim), BF16)],
        compiler_params=_cparams(2, VMEM_SSD),
        name="ssd_core",
    )(zx, zx, zx, zx, dt_raw, conv_w, conv_b, dt_bias, a_log, d_skip, gate_norm, _stacked(head_lanes), _shift_matrix(l))


def _dilated_layer(h, norm_g, w_qkv, w_o, dil_biases):
    b, s, d = h.shape
    width = 3 * DIL_HEADS * HEAD_DIM
    outs, lses = [], []
    for g, (_, dilation) in enumerate(DIL_PATTERNS):
        wide = dilation == 1
        qkv = norm_matmul(h, norm_g, w_qkv, n=width, tn=width, col_block=g, dilation=dilation,
                          rows=1024 if wide else 512, q_cols=width // 3,
                          vmem_bytes=VMEM_PROJECTION_WIDE if wide else VMEM_PROJECTION)
        o, lse = dilated_attention(qkv, dil_biases[g], dilation)
        outs.append(o)
        lses.append(lse)
    return dilated_combine(outs, lses, w_o, h)


def _diff_layer(h, norm_g, w_qkv, lam_q1, lam_k1, lam_q2, lam_k2, subln, w_o, bias, lambda_init):
    b, s, d = h.shape
    width = w_qkv.shape[1]
    qkv = norm_matmul(h, norm_g, w_qkv, n=width, tn=width, rows=1024, q_cols=width // 3,
                      vmem_bytes=VMEM_PROJECTION_WIDE).reshape(b, s, width)
    lam_vecs = jnp.stack([lam_q1, lam_k1, lam_q2, lam_k2])
    o = diff_attention(qkv, bias, lam_vecs, subln, lambda_init)
    return matmul_residual(o.reshape(b * s, -1), w_o, h.reshape(b * s, d)).reshape(b, s, d)


def _ssd_layer(h, norm_g, w_in, conv_w, conv_b, dt_bias, a_log, d_skip, gate_norm, w_out):
    b, s, d = h.shape
    wide = SSM_D_INNER + conv_w.shape[1]
    pad = LANES - SSM_HEADS
    zx = norm_matmul(h, norm_g, w_in, n=wide, tn=wide // 2, out_dtype=F32).reshape(b, s, wide)
    dt_raw = norm_matmul(h, norm_g, w_in, n=LANES, tn=LANES, col_block=wide // LANES,
                         out_dtype=F32).reshape(b, s, LANES)
    y = ssd_core(zx, dt_raw, conv_w, conv_b.reshape(1, -1),
                 jnp.pad(dt_bias, (0, pad)).reshape(1, LANES), jnp.pad(a_log, (0, pad)).reshape(1, LANES),
                 jnp.repeat(d_skip, SSM_HEAD_DIM).reshape(1, SSM_D_INNER), gate_norm.reshape(1, SSM_D_INNER))
    return matmul_residual(y.reshape(b * s, -1), w_out, h.reshape(b * s, d)).reshape(b, s, d)


def _diff_lambda_init(layer):
    return 0.8 - 0.6 * math.exp(-0.3 * layer)


def kernel(x, rel_bias,
           l0_mix_norm, l0_dil_w_qkv, l0_dil_w_o, l0_mlp_norm, l0_mlp_w_up, l0_mlp_w_down,
           l1_mix_norm, l1_diff_w_qkv, l1_diff_lam_q1, l1_diff_lam_k1, l1_diff_lam_q2, l1_diff_lam_k2,
           l1_diff_subln, l1_diff_w_o, l1_mlp_norm, l1_mlp_w_up, l1_mlp_w_down,
           l2_mix_norm, l2_ssm_w_in, l2_ssm_conv_w, l2_ssm_conv_b, l2_ssm_dt_bias, l2_ssm_A_log, l2_ssm_D,
           l2_ssm_gate_norm, l2_ssm_w_out, l2_mlp_norm, l2_mlp_w_up, l2_mlp_w_down,
           l3_mix_norm, l3_dil_w_qkv, l3_dil_w_o, l3_mlp_norm, l3_mlp_w_up, l3_mlp_w_down,
           final_norm):
    b, s, d = x.shape
    dil_biases = [dilated_bias(rel_bias, dilation) for _, dilation in DIL_PATTERNS]
    dif_bias = diff_bias(rel_bias, s)

    def mlp(h, g, w_up, w_down, final_gain=None):
        return mlp_block(h.reshape(b * s, d), g, w_up, w_down, final_gain).reshape(b, s, d)

    h = _dilated_layer(x, l0_mix_norm, l0_dil_w_qkv, l0_dil_w_o, dil_biases)
    h = mlp(h, l0_mlp_norm, l0_mlp_w_up, l0_mlp_w_down)
    h = _diff_layer(h, l1_mix_norm, l1_diff_w_qkv, l1_diff_lam_q1, l1_diff_lam_k1, l1_diff_lam_q2,
                    l1_diff_lam_k2, l1_diff_subln, l1_diff_w_o, dif_bias, _diff_lambda_init(1))
    h = mlp(h, l1_mlp_norm, l1_mlp_w_up, l1_mlp_w_down)
    h = _ssd_layer(h, l2_mix_norm, l2_ssm_w_in, l2_ssm_conv_w, l2_ssm_conv_b, l2_ssm_dt_bias, l2_ssm_A_log,
                   l2_ssm_D, l2_ssm_gate_norm, l2_ssm_w_out)
    h = mlp(h, l2_mlp_norm, l2_mlp_w_up, l2_mlp_w_down)
    h = _dilated_layer(h, l3_mix_norm, l3_dil_w_qkv, l3_dil_w_o, dil_biases)
    return mlp(h, l3_mlp_norm, l3_mlp_w_up, l3_mlp_w_down, final_norm)
```

```python
import functools
import math

import jax
import jax.numpy as jnp
from jax import lax
from jax.experimental import pallas as pl
from jax.experimental.pallas import tpu as pltpu

F32 = jnp.float32
BF16 = jnp.bfloat16

NORM_EPS = 1e-5
HEAD_DIM = 64
DIL_PATTERNS = ((128, 1), (512, 4), (2048, 16))
DIL_HEADS = 16
DIL_STEPS = 128
DIFF_HEADS = 8
N_BUCKETS = 32
MAX_DISTANCE = 2048
SSM_HEADS = 32
SSM_GROUPS = 8
SSM_D_STATE = 128
SSM_HEAD_DIM = 64
SSM_D_INNER = SSM_HEADS * SSM_HEAD_DIM
SSM_CONV = 4
SSM_CHUNK = 128
LANES = 128
SUBLANES = 8
MASK_VALUE = -1e30
LOG2E = math.log2(math.e)
LN2 = math.log(2.0)
Q_SCALE = LOG2E * HEAD_DIM ** -0.5
MIB = 1024 * 1024
VMEM_PROJECTION = 34 * MIB
VMEM_PROJECTION_WIDE = 42 * MIB
VMEM_MLP = 52 * MIB
VMEM_PROJ_MLP = 58 * MIB
VMEM_RESIDUAL = 36 * MIB
VMEM_COMBINE = 40 * MIB
VMEM_DIFF_ATTENTION = 36 * MIB
VMEM_SSD = 20 * MIB
VMEM_DIFF_BIAS = 14 * MIB
VMEM_DIL_ATTENTION = 40 * MIB
VMEM_SMALL = 6 * MIB

DIL_BLOCKS_PER_STEP = 4
DIFF_TILE = 512
DIFF_HEADS_PER_STEP = 2


def _bucket_thresholds():
    exact = N_BUCKETS // 2

    def bucket(d):
        if d < exact:
            return d
        big = exact + int(math.log(d / exact) / math.log(MAX_DISTANCE / exact) * (N_BUCKETS - exact))
        return min(big, N_BUCKETS - 1)

    thr = {}
    d = 0
    for b in range(1, N_BUCKETS):
        while bucket(d) < b:
            d += 1
        thr[b] = d
    return thr, bucket


_BUCKET_THR, _bucket_of = _bucket_thresholds()


def _cparams(grid_rank, vmem_bytes):
    return pltpu.CompilerParams(dimension_semantics=("arbitrary",) * grid_rank, vmem_limit_bytes=vmem_bytes)


def _rms(x, gain):
    ms = jnp.mean(x * x, axis=-1, keepdims=True)
    return x * lax.rsqrt(ms + NORM_EPS) * gain


def _norm_matmul_body(x_ref, g_ref, w_ref, o_ref, *slab_ref, r, valid_cols, q_cols):
    rows, d = x_ref.shape
    if r == 1:
        x = x_ref[...]
    else:
        slab = slab_ref[0]
        per = rows // r
        for c in range(d // LANES):
            slab[c] = x_ref[:, c * LANES:(c + 1) * LANES]
        x = jnp.concatenate(
            [jnp.concatenate([slab[c, pl.ds(rho, per, stride=r), :] for c in range(d // LANES)], axis=1)
             for rho in range(r)], axis=0)
    xn = _rms(x, g_ref[...]).astype(BF16)
    w = w_ref[...]
    if valid_cols is not None:
        w = jnp.where(lax.broadcasted_iota(jnp.int32, (1, w.shape[1]), 1) < valid_cols, w, 0.0)
    res = jnp.dot(xn, w.astype(BF16), preferred_element_type=F32)
    if q_cols:
        res = jnp.concatenate([res[:, :q_cols] * Q_SCALE, res[:, q_cols:]], axis=1)
    o_ref[...] = res.astype(o_ref.dtype).reshape(o_ref.shape)


def norm_matmul(x, gain, w, *, n, tn, col_block=0, dilation=1, rows=512, out_dtype=BF16,
                vmem_bytes=VMEM_PROJECTION, q_cols=0):
    first_col = col_block * tn
    valid_cols = w.shape[1] - first_col if first_col + n > w.shape[1] else None
    assert valid_cols is None or n == tn
    b, s, d = x.shape
    r = dilation
    per = rows // r
    scratch = [pltpu.VMEM((d // LANES, rows, LANES), F32)] if r > 1 else []
    return pl.pallas_call(
        functools.partial(_norm_matmul_body, r=r, valid_cols=valid_cols, q_cols=q_cols),
        grid=(n // tn, b, s // rows),
        in_specs=[
            pl.BlockSpec((None, rows, d), lambda j, bi, i: (bi, i, 0)),
            pl.BlockSpec((1, d), lambda j, bi, i: (0, 0)),
            pl.BlockSpec((d, tn), lambda j, bi, i: (0, col_block + j), pipeline_mode=pl.Buffered(1)),
        ],
        out_specs=pl.BlockSpec((None, r, per, tn), lambda j, bi, i: (bi, 0, i, j)),
        out_shape=jax.ShapeDtypeStruct((b, r, s // r, n), out_dtype),
        scratch_shapes=scratch,
        compiler_params=_cparams(3, vmem_bytes),
        name="norm_matmul",
    )(x, gain.reshape(1, d), w)


def _matmul_res_body(a_ref, w_ref, r_ref, o_ref):
    o_ref[...] = r_ref[...] + jnp.dot(a_ref[...], w_ref[...].astype(BF16), preferred_element_type=F32)


def matmul_residual(a, w, res, *, tm=512):
    m, k = a.shape
    n = w.shape[1]
    return pl.pallas_call(
        _matmul_res_body,
        grid=(m // tm,),
        in_specs=[
            pl.BlockSpec((tm, k), lambda i: (i, 0)),
            pl.BlockSpec((k, n), lambda i: (0, 0)),
            pl.BlockSpec((tm, n), lambda i: (i, 0)),
        ],
        out_specs=pl.BlockSpec((tm, n), lambda i: (i, 0)),
        out_shape=jax.ShapeDtypeStruct((m, n), F32),
        compiler_params=_cparams(1, VMEM_RESIDUAL),
        name="matmul_residual",
    )(a, w, res)


def _mlp_body(h_ref, g_ref, wu_ref, wd_ref, fg_ref, o_ref, *, final_norm):
    f = pl.program_id(1)
    last = pl.num_programs(1) - 1

    def contribution():
        xn = _rms(h_ref[...], g_ref[...]).astype(BF16)
        u = jnp.maximum(jnp.dot(xn, wu_ref[...].astype(BF16), preferred_element_type=F32), 0.0)
        return jnp.dot((u * u).astype(BF16), wd_ref[...].astype(BF16), preferred_element_type=F32)

    @pl.when(f == 0)
    def _():
        o_ref[...] = h_ref[...] + contribution()

    @pl.when((f > 0) & (f < last))
    def _():
        o_ref[...] += contribution()

    @pl.when(f == last)
    def _():
        out = o_ref[...] + contribution()
        if final_norm:
            out = _rms(out, fg_ref[...])
        o_ref[...] = out


def mlp_block(h, gain, w_up, w_down, final_gain=None, *, tm=1024, tf=1024):
    m, d = h.shape
    f = w_up.shape[1]
    assert f // tf >= 2
    final_norm = final_gain is not None
    fg = (final_gain if final_norm else gain).reshape(1, d)
    return pl.pallas_call(
        functools.partial(_mlp_body, final_norm=final_norm),
        grid=(m // tm, f // tf),
        in_specs=[
            pl.BlockSpec((tm, d), lambda i, j: (i, 0)),
            pl.BlockSpec((1, d), lambda i, j: (0, 0)),
            pl.BlockSpec((d, tf), lambda i, j: (0, j)),
            pl.BlockSpec((tf, d), lambda i, j: (j, 0)),
            pl.BlockSpec((1, d), lambda i, j: (0, 0)),
        ],
        out_specs=pl.BlockSpec((tm, d), lambda i, j: (i, 0)),
        out_shape=jax.ShapeDtypeStruct((m, d), F32),
        compiler_params=_cparams(2, VMEM_MLP),
        name="mlp_block",
    )(h, gain.reshape(1, d), w_up, w_down, fg)


def _proj_mlp_body(h_ref, a_ref, wr_ref, g_ref, wu_ref, wd_ref, o_ref, mid_ref):
    f = pl.program_id(1)

    def contribution(mid):
        xn = _rms(mid, g_ref[...]).astype(BF16)
        u = jnp.maximum(jnp.dot(xn, wu_ref[...].astype(BF16), preferred_element_type=F32), 0.0)
        return jnp.dot((u * u).astype(BF16), wd_ref[...].astype(BF16), preferred_element_type=F32)

    @pl.when(f == 0)
    def _():
        mid = h_ref[...] + jnp.dot(a_ref[...], wr_ref[...].astype(BF16), preferred_element_type=F32)
        mid_ref[...] = mid
        o_ref[...] = mid + contribution(mid)

    @pl.when(f > 0)
    def _():
        o_ref[...] += contribution(mid_ref[...])


def proj_mlp_block(h, a, w_res, gain, w_up, w_down, *, tm=1024, tf=1024):
    m, d = h.shape
    k = a.shape[1]
    f = w_up.shape[1]
    return pl.pallas_call(
        _proj_mlp_body,
        grid=(m // tm, f // tf),
        in_specs=[
            pl.BlockSpec((tm, d), lambda i, j: (i, 0)),
            pl.BlockSpec((tm, k), lambda i, j: (i, 0)),
            pl.BlockSpec((k, d), lambda i, j: (0, 0), pipeline_mode=pl.Buffered(1)),
            pl.BlockSpec((1, d), lambda i, j: (0, 0)),
            pl.BlockSpec((d, tf), lambda i, j: (0, j)),
            pl.BlockSpec((tf, d), lambda i, j: (j, 0)),
        ],
        out_specs=pl.BlockSpec((tm, d), lambda i, j: (i, 0)),
        out_shape=jax.ShapeDtypeStruct((m, d), F32),
        scratch_shapes=[pltpu.VMEM((tm, d), F32)],
        compiler_params=_cparams(2, VMEM_PROJ_MLP),
        name="proj_mlp_block",
    )(h, a, w_res, gain.reshape(1, d), w_up, w_down)


def _bias_from_dist(dist, col_of_bucket, d_lo, d_hi):
    val = jnp.full(dist.shape, col_of_bucket(_bucket_of(d_lo)) * LOG2E, F32)
    for b in range(1, N_BUCKETS):
        t = _BUCKET_THR[b]
        if d_lo < t <= d_hi:
            val = jnp.where(dist >= t, col_of_bucket(b) * LOG2E, val)
    return val


def _dil_bias_body(tab_ref, o_ref, *, dilation):
    c = pl.program_id(0)
    n = DIL_STEPS
    qi = lax.broadcasted_iota(jnp.int32, (n, 2 * n), 0)
    kj = lax.broadcasted_iota(jnp.int32, (n, 2 * n), 1)
    steps = n + qi - kj
    band = (steps >= 0) & (steps <= n)
    dist = jnp.clip(steps, 0, n) * dilation
    val = _bias_from_dist(dist, lambda b: tab_ref[b, c], 0, n * dilation)
    later = jnp.where(band, val, MASK_VALUE)
    o_ref[1] = later
    o_ref[0] = jnp.where(kj >= n, later, MASK_VALUE)


def dilated_bias(rel_bias, dilation):
    n = DIL_STEPS
    return pl.pallas_call(
        functools.partial(_dil_bias_body, dilation=dilation),
        grid=(DIL_HEADS,),
        in_specs=[pl.BlockSpec(memory_space=pltpu.SMEM)],
        out_specs=pl.BlockSpec((2, None, n, 2 * n), lambda c: (0, c, 0, 0)),
        out_shape=jax.ShapeDtypeStruct((2, DIL_HEADS, n, 2 * n), F32),
        compiler_params=_cparams(1, VMEM_SMALL),
        name="dilated_bias",
    )(rel_bias)


def _diff_bias_body(tab_ref, o_ref, *, n_delta):
    m = pl.program_id(0)
    h = pl.program_id(1)
    c = m * DIFF_HEADS + h
    t = DIFF_TILE
    qi = lax.broadcasted_iota(jnp.int32, (t, t), 0)
    kj = lax.broadcasted_iota(jnp.int32, (t, t), 1)
    for delta in range(n_delta):
        dist = delta * t + qi - kj
        d_lo = max(delta * t - (t - 1), 0)
        d_hi = delta * t + t - 1
        val = _bias_from_dist(dist, lambda b: tab_ref[b, c], d_lo, d_hi)
        if delta == 0:
            val = jnp.where(dist >= 0, val, MASK_VALUE)
        o_ref[delta] = val


def _diff_first_constant_delta():
    t = DIFF_TILE
    last_thr = _BUCKET_THR[N_BUCKETS - 1]
    delta = 1
    while delta * t - (t - 1) < last_thr:
        delta += 1
    return delta


def _diff_n_delta(s):
    return min(_diff_first_constant_delta() + 1, s // DIFF_TILE)


def diff_bias(rel_bias, s):
    t = DIFF_TILE
    n_delta = _diff_n_delta(s)
    return pl.pallas_call(
        functools.partial(_diff_bias_body, n_delta=n_delta),
        grid=(2, DIFF_HEADS),
        in_specs=[pl.BlockSpec(memory_space=pltpu.SMEM)],
        out_specs=pl.BlockSpec((None, None, n_delta, t, t), lambda m, h: (h, m, 0, 0, 0)),
        out_shape=jax.ShapeDtypeStruct((DIFF_HEADS, 2, n_delta, t, t), F32),
        compiler_params=_cparams(2, VMEM_DIFF_BIAS),
        name="diff_bias",
    )(rel_bias)


def _dil_attn_body(q_ref, kc_ref, vc_ref, bias_ref, o_ref, lse_ref, kp_ref, vp_ref):
    n = DIL_STEPS
    nblk = q_ref.shape[0] // n
    hd = DIL_HEADS * HEAD_DIM
    nt = (((1,), (1,)), ((), ()))
    lane = lax.broadcasted_iota(jnp.int32, (1, LANES), 1)
    low = lane < HEAD_DIM
    ones_low = jnp.where(low, 1, 0).astype(BF16) + jnp.zeros((2 * n, LANES), BF16)
    ones_high = jnp.where(low, 0, 1).astype(BF16) + jnp.zeros((2 * n, LANES), BF16)

    @pl.when(pl.program_id(2) == 0)
    def _():
        kp_ref[...] = jnp.zeros_like(kp_ref)
        vp_ref[...] = jnp.zeros_like(vp_ref)

    npairs = hd // LANES
    first_variant = jnp.minimum(pl.program_id(2), 1)

    def keys_values(blk, cs):
        if blk == 0:
            return (jnp.concatenate([kp_ref[:, cs], kc_ref[0:n, cs]], axis=0),
                    jnp.concatenate([vp_ref[:, cs], vc_ref[0:n, cs]], axis=0))
        return kc_ref[(blk - 1) * n:(blk + 1) * n, cs], vc_ref[(blk - 1) * n:(blk + 1) * n, cs]

    lse_tiles = [jnp.zeros((n, LANES), F32) for _ in range(nblk)]
    for pr in range(npairs):
        cs = slice(pr * LANES, (pr + 1) * LANES)
        for blk in range(nblk):
            rows = slice(blk * n, (blk + 1) * n)
            k2, v2 = keys_values(blk, cs)
            q2 = q_ref[rows, cs]
            qs = jnp.concatenate([jnp.where(low, q2, 0), jnp.where(low, 0, q2)], axis=0)
            variant = first_variant if blk == 0 else 1
            bias = bias_ref[variant, 2 * pr:2 * pr + 2].reshape(2 * n, 2 * n)
            s = lax.dot_general(qs, k2, nt, preferred_element_type=F32) + bias
            m = jnp.max(jnp.maximum(s[:, :n], s[:, n:]), axis=-1, keepdims=True)
            p = jnp.exp2(s - m).astype(BF16)
            w = jnp.concatenate(
                [jnp.concatenate([jnp.where(low, v2, 0), ones_low], axis=1),
                 jnp.concatenate([jnp.where(low, 0, v2), ones_high], axis=1)], axis=0)
            r = jnp.dot(jnp.concatenate([p[:n], p[n:]], axis=1), w, preferred_element_type=F32)
            den = r[:, LANES:]
            o_ref[rows, cs] = (r[:, :LANES] / den).astype(o_ref.dtype)
            lse = jnp.where(low, m[:n], m[n:]) * LN2 + jnp.log(den)
            lse_tiles[blk] = jnp.where((lane == pr) | (lane == HEAD_DIM + pr), lse, lse_tiles[blk])
    for blk in range(nblk):
        lse_ref[blk * n:(blk + 1) * n, :] = lse_tiles[blk]
    kp_ref[...] = kc_ref[(nblk - 1) * n:, :]
    vp_ref[...] = vc_ref[(nblk - 1) * n:, :]


def dilated_attention(qkv, bias, dilation):
    b, r, length, _ = qkv.shape
    n = DIL_STEPS
    hd = DIL_HEADS * HEAD_DIM
    rows = min(DIL_BLOCKS_PER_STEP * n, length)
    assert length % rows == 0

    def cur(col):
        return pl.BlockSpec((None, None, rows, hd), lambda bi, rho, i: (bi, rho, i, col))

    return pl.pallas_call(
        _dil_attn_body,
        grid=(b, r, length // rows),
        in_specs=[cur(0), cur(1), cur(2),
                  pl.BlockSpec((2, DIL_HEADS, n, 2 * n), lambda bi, rho, i: (0, 0, 0, 0))],
        out_specs=[pl.BlockSpec((None, None, rows, hd), lambda bi, rho, i: (bi, rho, i, 0)),
                   pl.BlockSpec((None, None, rows, LANES), lambda bi, rho, i: (bi, rho, i, 0))],
        out_shape=[jax.ShapeDtypeStruct((b, r, length, hd), BF16),
                   jax.ShapeDtypeStruct((b, r, length, LANES), F32)],
        scratch_shapes=[pltpu.VMEM((n, hd), BF16), pltpu.VMEM((n, hd), BF16)],
        compiler_params=_cparams(3, VMEM_DIL_ATTENTION),
        name=f"dilated_attention_r{dilation}",
    )(qkv, qkv, qkv, bias)


def _dil_combine_body(o1_ref, o2_ref, o3_ref, l1_ref, l2_ref, l3_ref, e_ref, w_ref, h_ref, out_ref,
                      oslab_ref, lslab_ref):
    tm = out_ref.shape[0]
    nslab = oslab_ref.shape[1]

    def token_order(o_ref, l_ref, k):
        r = o_ref.shape[0]
        if r == 1:
            return o_ref[0].astype(F32), l_ref[0]
        per = tm // r
        for rho in range(r):
            blk = o_ref[rho].astype(F32)
            for c in range(nslab):
                oslab_ref[k, c, pl.ds(rho, per, stride=r), :] = blk[:, c * LANES:(c + 1) * LANES]
            lslab_ref[k, pl.ds(rho, per, stride=r), :] = l_ref[rho]
        return jnp.concatenate([oslab_ref[k, c] for c in range(nslab)], axis=1), lslab_ref[k]

    o1, l1 = token_order(o1_ref, l1_ref, 0)
    o2, l2 = token_order(o2_ref, l2_ref, 0)
    o3, l3 = token_order(o3_ref, l3_ref, 1)
    mx = jnp.maximum(jnp.maximum(l1, l2), l3)
    e1, e2, e3 = jnp.exp(l1 - mx), jnp.exp(l2 - mx), jnp.exp(l3 - mx)
    inv = 1.0 / (e1 + e2 + e3)

    mix = o3 + _spread(e1 * inv, e_ref) * (o1 - o3) + _spread(e2 * inv, e_ref) * (o2 - o3)
    out_ref[...] = h_ref[...] + jnp.dot(mix.astype(BF16), w_ref[...].astype(BF16), preferred_element_type=F32)


def _head_spread_matrix():
    hd = DIL_HEADS * HEAD_DIM
    lane = jnp.arange(LANES)[:, None]
    col = jnp.arange(hd)[None, :]
    head = col // HEAD_DIM
    src = jnp.where(head % 2 == 0, head // 2, HEAD_DIM + head // 2)
    return (lane == src).astype(BF16)


def dilated_combine(outs, lses, w_o, h, *, tm=512):
    b, s, d = h.shape
    hd = w_o.shape[0]

    def regrouped(arr):
        r = arr.shape[1]
        return pl.BlockSpec((None, r, tm // r, arr.shape[3]), lambda bi, i: (bi, 0, i, 0))

    return pl.pallas_call(
        _dil_combine_body,
        grid=(b, s // tm),
        in_specs=[regrouped(a) for a in (*outs, *lses)] + [
            pl.BlockSpec((2 * LANES, hd), lambda bi, i: (0, 0)),
            pl.BlockSpec(w_o.shape, lambda bi, i: (0, 0)),
            pl.BlockSpec((None, tm, d), lambda bi, i: (bi, i, 0))],
        out_specs=pl.BlockSpec((None, tm, d), lambda bi, i: (bi, i, 0)),
        out_shape=jax.ShapeDtypeStruct((b, s, d), F32),
        scratch_shapes=[pltpu.VMEM((2, hd // LANES, tm, LANES), F32), pltpu.VMEM((2, tm, LANES), F32)],
        compiler_params=_cparams(2, VMEM_COMBINE),
        name="dilated_combine",
    )(*outs, *lses, _stacked(_head_spread_matrix()), w_o, h)


def _diff_attn_body(qi_ref, ki_ref, q_ref, k_ref, v_ref, bias_ref, lam_ref, g_ref, o_ref,
                    m_ref, acc_ref, *, lambda_init):
    p = pl.program_id(1)
    qi = qi_ref[p]
    ki = ki_ref[p]
    nbatch, t, _ = q_ref.shape
    hw = 2 * HEAD_DIM
    nheads = q_ref.shape[2] // hw
    nt = (((1,), (1,)), ((), ()))
    low = lax.broadcasted_iota(jnp.int32, (1, hw), 1) < HEAD_DIM

    @pl.when(ki == 0)
    def _():
        m_ref[...] = jnp.full(m_ref.shape, MASK_VALUE, F32)
        acc_ref[...] = jnp.zeros_like(acc_ref)

    for bi in range(nbatch):
        for hh in range(nheads):
            hs = slice(hh * hw, (hh + 1) * hw)
            q = q_ref[bi, :, hs]
            k = k_ref[bi, :, hs]
            v1 = jnp.concatenate([v_ref[bi, :, hs], jnp.ones((t, hw), BF16)], axis=1)
            for mp in range(2):
                qm = jnp.where(low, q, 0) if mp == 0 else jnp.where(low, 0, q)
                s = lax.dot_general(qm, k, nt, preferred_element_type=F32) + bias_ref[hh, mp]
                m_old = m_ref[bi, hh, mp]
                m_new = jnp.maximum(m_old, jnp.max(s, axis=-1, keepdims=True))
                alpha = jnp.exp2(m_old - m_new)
                pr = jnp.exp2(s - jnp.concatenate([m_new] * (t // hw), axis=1))
                acc_ref[bi, hh, mp] = (jnp.concatenate([alpha, alpha], axis=1) * acc_ref[bi, hh, mp]
                                       + jnp.dot(pr.astype(BF16), v1, preferred_element_type=F32))
                m_ref[bi, hh, mp] = m_new

    @pl.when(ki == qi)
    def _():
        lam = (jnp.exp(jnp.sum(lam_ref[0:1, :] * lam_ref[1:2, :], axis=-1, keepdims=True))
               - jnp.exp(jnp.sum(lam_ref[2:3, :] * lam_ref[3:4, :], axis=-1, keepdims=True)) + lambda_init)
        gain = g_ref[...] * (1.0 - lambda_init)

        def finish(bi, carry):
            for hh in range(nheads):
                a0 = acc_ref[bi, hh, 0]
                a1 = acc_ref[bi, hh, 1]
                o = a0[:, :hw] / a0[:, hw:] - lam * (a1[:, :hw] / a1[:, hw:])
                o_ref[bi, :, hh * hw:(hh + 1) * hw] = _rms(o, gain).astype(o_ref.dtype)
            return carry

        lax.fori_loop(0, nbatch, finish, 0)


def diff_attention(qkv, bias, lam_vecs, subln, lambda_init):
    b, s, _ = qkv.shape
    t = DIFF_TILE
    nq = s // t
    n_delta = bias.shape[2]
    pairs = [(qi, ki) for qi in range(nq) for ki in range(qi + 1)]
    qi_arr = jnp.asarray([pq for pq, _ in pairs], jnp.int32)
    ki_arr = jnp.asarray([pk for _, pk in pairs], jnp.int32)
    hw = 2 * HEAD_DIM
    nh = DIFF_HEADS
    hps = DIFF_HEADS_PER_STEP
    ng = nh // hps
    gw = hps * hw
    grid_spec = pltpu.PrefetchScalarGridSpec(
        num_scalar_prefetch=2,
        grid=(ng, len(pairs)),
        in_specs=[
            pl.BlockSpec((b, t, gw), lambda h, p, qa, ka: (0, qa[p], h)),
            pl.BlockSpec((b, t, gw), lambda h, p, qa, ka: (0, ka[p], ng + h)),
            pl.BlockSpec((b, t, gw), lambda h, p, qa, ka: (0, ka[p], 2 * ng + h)),
            pl.BlockSpec((hps, 2, None, t, t),
                         lambda h, p, qa, ka: (h, 0, jnp.minimum(qa[p] - ka[p], n_delta - 1), 0, 0)),
            pl.BlockSpec((4, HEAD_DIM), lambda h, p, qa, ka: (0, 0)),
            pl.BlockSpec((1, hw), lambda h, p, qa, ka: (0, 0)),
        ],
        out_specs=pl.BlockSpec((b, t, gw), lambda h, p, qa, ka: (0, qa[p], h)),
        scratch_shapes=[pltpu.VMEM((b, hps, 2, t, hw), F32), pltpu.VMEM((b, hps, 2, t, 2 * hw), F32)],
    )
    return pl.pallas_call(
        functools.partial(_diff_attn_body, lambda_init=lambda_init),
        grid_spec=grid_spec,
        out_shape=jax.ShapeDtypeStruct((b, s, nh * hw), BF16),
        compiler_params=_cparams(2, VMEM_DIFF_ATTENTION),
        name="diff_attention",
    )(qi_arr, ki_arr, qkv, qkv, qkv, bias, lam_vecs, subln.reshape(1, hw))


def _silu(x):
    h = 0.5 * x
    return h + h * jnp.tanh(h)


def _conv_silu(cur, win_ref, shift_ref, w_ref, b_ref, cols):
    l = cur.shape[0]
    lagged = jnp.dot(shift_ref[...], win_ref[:, cols], preferred_element_type=F32)
    acc = b_ref[:, cols] + w_ref[SSM_CONV - 1:SSM_CONV, cols] * cur
    for k in range(SSM_CONV - 1):
        acc = acc + w_ref[k:k + 1, cols] * lagged[k * l:(k + 1) * l]
    return _silu(acc)


def _shift_matrix(l):
    row = jnp.arange((SSM_CONV - 1) * l)[:, None]
    col = jnp.arange(2 * l)[None, :]
    k, t = row // l, row % l
    return (col == l + t - (SSM_CONV - 1 - k)).astype(BF16)


def _spread(w, e_ref):
    hi = w.astype(BF16)
    lo = (w - hi.astype(F32)).astype(BF16)
    return jnp.dot(jnp.concatenate([hi, lo], axis=1), e_ref[...], preferred_element_type=F32)


def _stacked(e):
    return jnp.concatenate([e, e], axis=0)


def _cumsum_rows(a):
    rows = a.shape[0]
    ridx = lax.broadcasted_iota(jnp.int32, a.shape, 0)
    sh = 1
    while sh < rows:
        a = a + jnp.where(ridx >= sh, pltpu.roll(a, sh, 0), 0.0)
        sh *= 2
    return a


def _ssd_body(z_ref, x_ref, bm_ref, cm_ref, dt_ref, cw_ref, cb_ref, dtb_ref, alog_ref, dsk_ref, gn_ref, e_ref,
              sh_ref, y_ref, state_ref, win_ref):
    c = pl.program_id(1)
    l = SSM_CHUNK
    ns = SSM_D_STATE
    di = SSM_D_INNER
    gn = SSM_GROUPS * ns
    hg = SSM_HEADS // SSM_GROUPS
    gw = hg * SSM_HEAD_DIM
    tail = 2 * SUBLANES

    @pl.when(c == 0)
    def _():
        state_ref[...] = jnp.zeros_like(state_ref)
        win_ref[0:l, :] = jnp.zeros((l, win_ref.shape[1]), BF16)

    win_ref[l:, 0:di] = x_ref[...].astype(BF16)
    win_ref[l:, di:di + gn] = bm_ref[...].astype(BF16)
    win_ref[l:, di + gn:] = cm_ref[...].astype(BF16)
    xs = [_conv_silu(x_ref[:, g * gw:(g + 1) * gw], win_ref, sh_ref, cw_ref, cb_ref, slice(g * gw, (g + 1) * gw))
          for g in range(SSM_GROUPS)]
    bmat = _conv_silu(bm_ref[...], win_ref, sh_ref, cw_ref, cb_ref, slice(di, di + gn)).astype(BF16)
    cmat = _conv_silu(cm_ref[...], win_ref, sh_ref, cw_ref, cb_ref, slice(di + gn, di + 2 * gn)).astype(BF16)
    win_ref[l - tail:l, :] = win_ref[2 * l - tail:, :]

    dt = jax.nn.softplus(dt_ref[...] + dtb_ref[...])
    a_cs = _cumsum_rows(dt * (-jnp.exp(alog_ref[...])))
    a_cs_t = a_cs.T
    last = a_cs[l - 1:l, :]
    dt_e = _spread(dt, e_ref)
    exp_a_e = _spread(jnp.exp(a_cs), e_ref)
    state_decay_e = _spread(jnp.exp(last - a_cs), e_ref)

    causal = (lax.broadcasted_iota(jnp.int32, (l, l), 0) >= lax.broadcasted_iota(jnp.int32, (l, l), 1))
    lane_head = lax.broadcasted_iota(jnp.int32, (1, gw), 1) // SSM_HEAD_DIM
    tn_dims = (((0,), (0,)), ((), ()))
    nt_dims = (((1,), (1,)), ((), ()))

    for g in range(SSM_GROUPS):
        gs = slice(g * gw, (g + 1) * gw)
        cg = cmat[:, g * ns:(g + 1) * ns]
        bg = bmat[:, g * ns:(g + 1) * ns]
        xg = xs[g]
        xdt = xg * dt_e[:, gs]
        xdt_b = xdt.astype(BF16)
        cb = lax.dot_general(cg, bg, nt_dims, preferred_element_type=F32)
        weights, operands = [], []
        for j in range(hg):
            hh = hg * g + j
            seg = a_cs[:, hh:hh + 1] - a_cs_t[hh:hh + 1, :]
            decay = jnp.where(causal, jnp.exp(seg), 0.0)
            weights.append((cb * decay).astype(BF16))
            operands.append(jnp.where(lane_head == j, xdt_b, 0))
        y_diag = jnp.dot(jnp.concatenate(weights, axis=1), jnp.concatenate(operands, axis=0),
                         preferred_element_type=F32)
        prev = state_ref[g]
        y_off = jnp.dot(cg, prev.astype(BF16), preferred_element_type=F32) * exp_a_e[:, gs]
        new = lax.dot_general(bg, (xdt * state_decay_e[:, gs]).astype(BF16), tn_dims,
                              preferred_element_type=F32)
        state_ref[g] = prev * exp_a_e[l - 1:l, gs] + new
        y = y_diag + y_off + xg * dsk_ref[:, gs]
        zg = z_ref[:, gs]
        y = y * _silu(zg)
        y_ref[:, gs] = _rms(y, gn_ref[:, gs]).astype(y_ref.dtype)


def ssd_core(zx, dt_raw, conv_w, conv_b, dt_bias, a_log, d_skip, gate_norm):
    b, s, _ = zx.shape
    l = SSM_CHUNK
    di = SSM_D_INNER
    gn = SSM_GROUPS * SSM_D_STATE
    conv_dim = di + 2 * gn
    z_spec = pl.BlockSpec((None, l, di), lambda bi, c: (bi, c, 0))
    x_spec = pl.BlockSpec((None, l, di), lambda bi, c: (bi, c, 1))
    b_spec = pl.BlockSpec((None, l, gn), lambda bi, c: (bi, c, 2 * di // gn))
    c_spec = pl.BlockSpec((None, l, gn), lambda bi, c: (bi, c, 2 * di // gn + 1))

    def whole(shape):
        return pl.BlockSpec(shape, lambda bi, c: (0, 0))

    head_lanes = (jnp.arange(LANES)[:, None] == jnp.arange(di)[None, :] // SSM_HEAD_DIM).astype(BF16)
    return pl.pallas_call(
        _ssd_body,
        grid=(b, s // l),
        in_specs=[z_spec, x_spec, b_spec, c_spec,
                  pl.BlockSpec((None, l, LANES), lambda bi, c: (bi, c, 0)),
                  whole((SSM_CONV, conv_dim)), whole((1, conv_dim)),
                  whole((1, LANES)), whole((1, LANES)), whole((1, di)), whole((1, di)), whole((2 * LANES, di)),
                  whole(((SSM_CONV - 1) * l, 2 * l))],
        out_specs=pl.BlockSpec((None, l, di), lambda bi, c: (bi, c, 0)),
        out_shape=jax.ShapeDtypeStruct((b, s, di), BF16),
        scratch_shapes=[pltpu.VMEM((SSM_GROUPS, SSM_D_STATE, (SSM_HEADS // SSM_GROUPS) * SSM_HEAD_DIM), F32),
                        pltpu.VMEM((2 * l, conv_dim), BF16)],
        compiler_params=_cparams(2, VMEM_SSD),
        name="ssd_core",
    )(zx, zx, zx, zx, dt_raw, conv_w, conv_b, dt_bias, a_log, d_skip, gate_norm, _stacked(head_lanes), _shift_matrix(l))


def _dilated_layer(h, norm_g, w_qkv, w_o, dil_biases):
    b, s, d = h.shape
    width = 3 * DIL_HEADS * HEAD_DIM
    outs, lses = [], []
    for g, (_, dilation) in enumerate(DIL_PATTERNS):
        wide = dilation == 1
        qkv = norm_matmul(h, norm_g, w_qkv, n=width, tn=width, col_block=g, dilation=dilation,
                          rows=1024 if wide else 512, q_cols=width // 3,
                          vmem_bytes=VMEM_PROJECTION_WIDE if wide else VMEM_PROJECTION)
        o, lse = dilated_attention(qkv, dil_biases[g], dilation)
        outs.append(o)
        lses.append(lse)
    return dilated_combine(outs, lses, w_o, h)


def _diff_layer(h, norm_g, w_qkv, lam_q1, lam_k1, lam_q2, lam_k2, subln, w_o, bias, lambda_init,
                mlp_g, w_up, w_down):
    b, s, d = h.shape
    width = w_qkv.shape[1]
    qkv = norm_matmul(h, norm_g, w_qkv, n=width, tn=width, rows=1024, q_cols=width // 3,
                      vmem_bytes=VMEM_PROJECTION_WIDE).reshape(b, s, width)
    lam_vecs = jnp.stack([lam_q1, lam_k1, lam_q2, lam_k2])
    o = diff_attention(qkv, bias, lam_vecs, subln, lambda_init)
    return proj_mlp_block(h.reshape(b * s, d), o.reshape(b * s, -1), w_o, mlp_g, w_up, w_down).reshape(b, s, d)


def _ssd_layer(h, norm_g, w_in, conv_w, conv_b, dt_bias, a_log, d_skip, gate_norm, w_out):
    b, s, d = h.shape
    wide = SSM_D_INNER + conv_w.shape[1]
    pad = LANES - SSM_HEADS
    zx = norm_matmul(h, norm_g, w_in, n=wide, tn=wide // 2, out_dtype=F32).reshape(b, s, wide)
    dt_raw = norm_matmul(h, norm_g, w_in, n=LANES, tn=LANES, col_block=wide // LANES,
                         out_dtype=F32).reshape(b, s, LANES)
    y = ssd_core(zx, dt_raw, conv_w, conv_b.reshape(1, -1),
                 jnp.pad(dt_bias, (0, pad)).reshape(1, LANES), jnp.pad(a_log, (0, pad)).reshape(1, LANES),
                 jnp.repeat(d_skip, SSM_HEAD_DIM).reshape(1, SSM_D_INNER), gate_norm.reshape(1, SSM_D_INNER))
    return matmul_residual(y.reshape(b * s, -1), w_out, h.reshape(b * s, d)).reshape(b, s, d)


def _diff_lambda_init(layer):
    return 0.8 - 0.6 * math.exp(-0.3 * layer)


def kernel(x, rel_bias,
           l0_mix_norm, l0_dil_w_qkv, l0_dil_w_o, l0_mlp_norm, l0_mlp_w_up, l0_mlp_w_down,
           l1_mix_norm, l1_diff_w_qkv, l1_diff_lam_q1, l1_diff_lam_k1, l1_diff_lam_q2, l1_diff_lam_k2,
           l1_diff_subln, l1_diff_w_o, l1_mlp_norm, l1_mlp_w_up, l1_mlp_w_down,
           l2_mix_norm, l2_ssm_w_in, l2_ssm_conv_w, l2_ssm_conv_b, l2_ssm_dt_bias, l2_ssm_A_log, l2_ssm_D,
           l2_ssm_gate_norm, l2_ssm_w_out, l2_mlp_norm, l2_mlp_w_up, l2_mlp_w_down,
           l3_mix_norm, l3_dil_w_qkv, l3_dil_w_o, l3_mlp_norm, l3_mlp_w_up, l3_mlp_w_down,
           final_norm):
    b, s, d = x.shape
    dil_biases = [dilated_bias(rel_bias, dilation) for _, dilation in DIL_PATTERNS]
    dif_bias = diff_bias(rel_bias, s)

    def mlp(h, g, w_up, w_down, final_gain=None):
        return mlp_block(h.reshape(b * s, d), g, w_up, w_down, final_gain).reshape(b, s, d)

    h = _dilated_layer(x, l0_mix_norm, l0_dil_w_qkv, l0_dil_w_o, dil_biases)
    h = mlp(h, l0_mlp_norm, l0_mlp_w_up, l0_mlp_w_down)
    h = _diff_layer(h, l1_mix_norm, l1_diff_w_qkv, l1_diff_lam_q1, l1_diff_lam_k1, l1_diff_lam_q2,
                    l1_diff_lam_k2, l1_diff_subln, l1_diff_w_o, dif_bias, _diff_lambda_init(1),
                    l1_mlp_norm, l1_mlp_w_up, l1_mlp_w_down)
    h = _ssd_layer(h, l2_mix_norm, l2_ssm_w_in, l2_ssm_conv_w, l2_ssm_conv_b, l2_ssm_dt_bias, l2_ssm_A_log,
                   l2_ssm_D, l2_ssm_gate_norm, l2_ssm_w_out)
    h = mlp(h, l2_mlp_norm, l2_mlp_w_up, l2_mlp_w_down)
    h = _dilated_layer(h, l3_mix_norm, l3_dil_w_qkv, l3_dil_w_o, dil_biases)
    return mlp(h, l3_mlp_norm, l3_mlp_w_up, l3_mlp_w_down, final_norm)
```
